```python
import math
import jax
import jax.numpy as jnp
from jax import lax
import numpy as np


D_MODEL = 1024
BATCH = 16
SEQ = 4096
DEPTH = 2
DEC_BATCH = 4
DEC_SEQ = 8192
PAST_LEN = 128

N_HEADS_A = 8
KV_HEADS_A = 2
HEAD_DIM_A = 64
GQA_GROUP = N_HEADS_A // KV_HEADS_A
WINDOW = 128
ATT_BLOCK = 128
SSD_HEADS = 4
SSD_HEAD_DIM = 64
D_INNER = SSD_HEADS * SSD_HEAD_DIM
SSD_GROUPS = 2
D_STATE = 64
CONV_K = 3
CHUNK = 128
DIFF_HEADS = 4
DIFF_QK_DIM = 32
DIFF_V_DIM = 64
Q_BLOCK = 128
WIDTH_A = N_HEADS_A * HEAD_DIM_A
WIDTH_C = DIFF_HEADS * DIFF_V_DIM
MIX_WIDTH = WIDTH_A + D_INNER + WIDTH_C
CONV_DIM = D_INNER + 2 * SSD_GROUPS * D_STATE
SPLIT_SIZES = (WIDTH_A, KV_HEADS_A * HEAD_DIM_A, KV_HEADS_A * HEAD_DIM_A,
               D_INNER, CONV_DIM, 2 * SSD_HEADS,
               DIFF_HEADS * 2 * DIFF_QK_DIM, DIFF_HEADS * 2 * DIFF_QK_DIM, WIDTH_C)
IN_DIM = sum(SPLIT_SIZES)
D_FF = 2816
N_EXPERTS = 8
TOP_K = 2
D_FF_EXPERT = 3584
EXPERT_BLOCK = 128
N_DENSE = (DEPTH + 1) // 2
N_MOE = DEPTH // 2
EPS = 1e-6

kernel_name = 'hybrid_bidir_encoder_swa_ssd_diffattn'


def rms_norm(x, g):
    xf = x.astype(jnp.float32)
    y = xf * lax.rsqrt(jnp.mean(xf * xf, axis=-1, keepdims=True) + EPS)
    return (y * g.astype(jnp.float32)).astype(x.dtype)


def split_cols(t, sizes):
    outs = []
    start = 0
    for n in sizes:
        outs.append(t[..., start:start + n])
        start += n
    return outs


def alibi_slopes(n):
    return jnp.exp2(-8.0 * jnp.arange(1, n + 1, dtype=jnp.float32) / n)


def swiglu(h, wg, wu, wd):
    return (jax.nn.silu(h @ wg) * (h @ wu)) @ wd


def windowed_sink_attention(q, k, v, sink, slopes):
    b, s = q.shape[:2]
    nb = s // ATT_BLOCK
    qb = q.reshape(b, nb, ATT_BLOCK, KV_HEADS_A, GQA_GROUP, HEAD_DIM_A)

    def band(t):
        pad = jnp.zeros((b, ATT_BLOCK) + t.shape[2:], t.dtype)
        tb = jnp.concatenate([pad, t, pad], axis=1).reshape((b, nb + 2, ATT_BLOCK) + t.shape[2:])
        return jnp.concatenate([tb[:, :-2], tb[:, 1:-1], tb[:, 2:]], axis=2)

    kw, vw = band(k), band(v)
    scores = jnp.einsum('bnqkgd,bnskd->bnkgqs', qb, kw).astype(jnp.float32) * (HEAD_DIM_A ** -0.5)
    rel = ATT_BLOCK + jnp.arange(ATT_BLOCK)[:, None] - jnp.arange(3 * ATT_BLOCK)[None, :]
    kpos = (jnp.arange(nb)[:, None] - 1) * ATT_BLOCK + jnp.arange(3 * ATT_BLOCK)[None, :]
    mask = (jnp.abs(rel) <= WINDOW)[None] & ((kpos >= 0) & (kpos < s))[:, None, :]
    dist = jnp.abs(rel).astype(jnp.float32)
    bias = -slopes.reshape(KV_HEADS_A, GQA_GROUP)[:, :, None, None] * dist
    logits = jnp.where(mask[None, :, None, None], scores + bias, -jnp.inf)
    sink_l = sink.astype(jnp.float32).reshape(1, 1, KV_HEADS_A, GQA_GROUP, 1, 1)
    m = jnp.maximum(logits.max(axis=-1, keepdims=True), sink_l)
    p = jnp.exp(logits - m)
    probs = p / (p.sum(axis=-1, keepdims=True) + jnp.exp(sink_l - m))
    out = jnp.einsum('bnkgqs,bnskd->bnqkgd', probs.astype(v.dtype), vw)
    return out.reshape(b, s, WIDTH_A)


def mixer_a(qa, ka, va, qg, kg, sink):
    b, s = qa.shape[:2]
    q = rms_norm(qa.reshape(b, s, N_HEADS_A, HEAD_DIM_A), qg)
    k = rms_norm(ka.reshape(b, s, KV_HEADS_A, HEAD_DIM_A), kg)
    v = va.reshape(b, s, KV_HEADS_A, HEAD_DIM_A)
    return windowed_sink_attention(q, k, v, sink, alibi_slopes(N_HEADS_A))


def centred_depthwise_conv(u, w, bias):
    out = lax.conv_general_dilated(u, w[:, None, :].astype(u.dtype), window_strides=(1,),
                                   padding=[(CONV_K // 2, CONV_K // 2)],
                                   dimension_numbers=('NWC', 'WIO', 'NWC'),
                                   feature_group_count=u.shape[-1])
    return out + bias.astype(u.dtype)


def ssd_chunked(x, dt, a, bm, cm):
    b, s, h, p = x.shape
    c = s // CHUNK
    xdt = (x * dt[..., None]).reshape(b, c, CHUNK, h, p)
    da = (dt * a).reshape(b, c, CHUNK, h).transpose(0, 3, 1, 2)
    bc = bm.reshape(b, c, CHUNK, h, D_STATE)
    cc = cm.reshape(b, c, CHUNK, h, D_STATE)
    cs = jnp.cumsum(da, axis=-1)
    lower = jnp.tril(jnp.ones((CHUNK, CHUNK), dtype=bool))
    decay = jnp.exp(jnp.where(lower, cs[..., :, None] - cs[..., None, :], -jnp.inf))
    scores = jnp.einsum('bclhn,bcshn->bhcls', cc, bc) * decay
    y_diag = jnp.einsum('bhcls,bcshp->bclhp', scores, xdt)
    to_end = jnp.exp(cs[..., -1:] - cs)
    states = jnp.einsum('bclhn,bhcl,bclhp->bchpn', bc, to_end, xdt)
    chunk_decay = jnp.exp(cs[..., -1])

    def step(carry, inp):
        st, dec = inp
        return carry * dec[..., None, None] + st, carry

    _, prev = lax.scan(step, jnp.zeros((b, h, p, D_STATE), jnp.float32),
                       (states.transpose(1, 0, 2, 3, 4), chunk_decay.transpose(2, 0, 1)))
    prev = prev.transpose(1, 0, 2, 3, 4)
    y_off = jnp.einsum('bclhn,bchpn,bhcl->bclhp', cc, prev, jnp.exp(cs))
    return (y_diag + y_off).reshape(b, s, h, p)


def mixer_b(zb, xbc, dtb, conv_w, conv_b, dt_bias, a_log, d_skip, norm_g):
    b, s = zb.shape[:2]
    u = jax.nn.silu(centred_depthwise_conv(xbc, conv_w, conv_b)).astype(jnp.float32)
    xs, bm, cm = split_cols(u, (D_INNER, SSD_GROUPS * D_STATE, SSD_GROUPS * D_STATE))
    xs = xs.reshape(b, s, SSD_HEADS, SSD_HEAD_DIM)
    rep = SSD_HEADS // SSD_GROUPS
    bm = jnp.repeat(bm.reshape(b, s, SSD_GROUPS, D_STATE), rep, axis=2)
    cm = jnp.repeat(cm.reshape(b, s, SSD_GROUPS, D_STATE), rep, axis=2)
    dt = jax.nn.softplus(dtb.astype(jnp.float32).reshape(b, s, 2, SSD_HEADS) + dt_bias.astype(jnp.float32))
    a = -jnp.exp(a_log.astype(jnp.float32))
    flip = lambda t: jnp.flip(t, axis=1)
    y_fwd = ssd_chunked(xs, dt[:, :, 0], a[0], bm, cm)
    y_bwd = flip(ssd_chunked(flip(xs), flip(dt[:, :, 1]), a[1], flip(bm), flip(cm)))
    y = y_fwd + y_bwd + xs * d_skip.astype(jnp.float32)[:, None]
    y = y.reshape(b, s, D_INNER) * jax.nn.silu(zb.astype(jnp.float32))
    return rms_norm(y, norm_g).astype(zb.dtype)


def diff_attention(q, k, v, lam, slopes):
    b, s = q.shape[:2]
    nb = s // Q_BLOCK
    qb = jnp.moveaxis(q.reshape(b, nb, Q_BLOCK, DIFF_HEADS, 2, DIFF_QK_DIM), 1, 0)
    kpos = jnp.arange(s)
    scale = DIFF_QK_DIM ** -0.5

    def one_block(args):
        qblk, i = args
        sc = jnp.einsum('bqhcd,bkhcd->bhcqk', qblk, k).astype(jnp.float32) * scale
        qpos = i * Q_BLOCK + jnp.arange(Q_BLOCK)
        dist = jnp.abs(qpos[:, None] - kpos[None, :]).astype(jnp.float32)
        sc = sc - slopes[None, :, None, None, None] * dist
        p = jax.nn.softmax(sc, axis=-1)
        w = p[:, :, 0] - lam * p[:, :, 1]
        return jnp.einsum('bhqk,bkhd->bqhd', w.astype(v.dtype), v)

    out = lax.map(one_block, (qb, jnp.arange(nb)))
    return jnp.moveaxis(out, 0, 1).reshape(b, s, DIFF_HEADS, DIFF_V_DIM)


def mixer_c(qc, kc, vc, qg, kg, lam_c, norm_g, lam_init):
    b, s = qc.shape[:2]
    q = rms_norm(qc.reshape(b, s, DIFF_HEADS, 2, DIFF_QK_DIM), qg)
    k = rms_norm(kc.reshape(b, s, DIFF_HEADS, 2, DIFF_QK_DIM), kg)
    v = vc.reshape(b, s, DIFF_HEADS, DIFF_V_DIM)
    lc = lam_c.astype(jnp.float32)
    lam = jnp.exp(jnp.sum(lc[0] * lc[1])) - jnp.exp(jnp.sum(lc[2] * lc[3])) + lam_init
    o = diff_attention(q, k, v, lam, alibi_slopes(DIFF_HEADS))
    o = rms_norm(o, norm_g) * (1.0 - lam_init)
    return o.reshape(b, s, WIDTH_C).astype(vc.dtype)


def moe_swiglu(x, w_router, w_gate, w_up, w_down):
    b, s, d = x.shape
    m = b * s
    xf = x.reshape(m, d)
    logits = (xf @ w_router).astype(jnp.float32)
    top_v, top_i = lax.top_k(logits, TOP_K)
    gates = jax.nn.softmax(top_v, axis=-1)
    flat_e = top_i.reshape(-1)
    flat_t = jnp.repeat(jnp.arange(m, dtype=jnp.int32), TOP_K)
    flat_g = gates.reshape(-1)
    order = jnp.argsort(flat_e)
    se, st, sg = flat_e[order], flat_t[order], flat_g[order]
    counts = jnp.bincount(flat_e, length=N_EXPERTS)
    starts = jnp.cumsum(counts) - counts
    padded = ((counts + EXPERT_BLOCK - 1) // EXPERT_BLOCK) * EXPERT_BLOCK
    pad_end = jnp.cumsum(padded)
    pad_start = pad_end - padded
    dest = pad_start[se] + (jnp.arange(m * TOP_K) - starts[se])
    n_blocks = (m * TOP_K) // EXPERT_BLOCK + N_EXPERTS
    rows = n_blocks * EXPERT_BLOCK
    row_tok = jnp.full((rows,), m, jnp.int32).at[dest].set(st)
    row_gate = jnp.zeros((rows,), jnp.float32).at[dest].set(sg)
    blk_exp = jnp.minimum(jnp.searchsorted(pad_end, jnp.arange(n_blocks) * EXPERT_BLOCK, side='right'),
                          N_EXPERTS - 1)
    x_pad = jnp.concatenate([xf, jnp.zeros((1, d), xf.dtype)], axis=0)
    xg = x_pad[row_tok].reshape(n_blocks, EXPERT_BLOCK, d)

    def expert_block(args):
        xb, e = args
        return swiglu(xb, w_gate[e], w_up[e], w_down[e])

    yb = lax.map(expert_block, (xg, blk_exp)).reshape(rows, d)
    y = jnp.zeros((m + 1, d), x.dtype).at[row_tok].add(yb * row_gate[:, None].astype(x.dtype))
    return y[:m].reshape(b, s, d)


def encoder_trunk(x, norm_mix_g, w_in, qnorm_a, knorm_a, sink_a, conv_w, conv_b, dt_bias, a_log,
                  d_skip, ssd_norm_g, qnorm_c, knorm_c, lam_c, diff_norm_g, w_out, norm_ffn_g,
                  ffn_w_gate, ffn_w_up, ffn_w_down, router_w, moe_w_gate, moe_w_up, moe_w_down):
    for l in range(DEPTH):
        h = rms_norm(x, norm_mix_g[l])
        qa, ka, va, zb, xbc, dtb, qc, kc, vc = split_cols(h @ w_in[l], SPLIT_SIZES)
        oa = mixer_a(qa, ka, va, qnorm_a[l], knorm_a[l], sink_a[l])
        ob = mixer_b(zb, xbc, dtb, conv_w[l], conv_b[l], dt_bias[l], a_log[l], d_skip[l], ssd_norm_g[l])
        lam_init = 0.8 - 0.6 * math.exp(-0.3 * l)
        oc = mixer_c(qc, kc, vc, qnorm_c[l], knorm_c[l], lam_c[l], diff_norm_g[l], lam_init)
        x = x + jnp.concatenate([oa, ob, oc], axis=-1) @ w_out[l]
        h = rms_norm(x, norm_ffn_g[l])
        i = l // 2
        if l % 2 == 0:
            x = x + swiglu(h, ffn_w_gate[i], ffn_w_up[i], ffn_w_down[i])
        else:
            x = x + moe_swiglu(h, router_w[i], moe_w_gate[i], moe_w_up[i], moe_w_down[i])
    return x


def setup_inputs(seed: int = 0) -> dict:
    key = jax.random.key(seed)
    ks = jax.random.split(key, 32)
    f32 = jnp.float32
    nrm = lambda k, shape, scale: jax.random.normal(k, shape, f32) * scale
    gain = lambda k, shape: 1.0 + 0.02 * jax.random.normal(k, shape, f32)
    dt0 = jnp.exp(jax.random.uniform(ks[9], (DEPTH, 2, SSD_HEADS), f32) * (math.log(0.1) - math.log(0.001))
                  + math.log(0.001))
    return {
        'x_prompt': nrm(ks[0], (BATCH, SEQ, D_MODEL), 1.0),
        'x_sample': nrm(ks[1], (DEC_BATCH, DEC_SEQ, D_MODEL), 1.0),
        'norm_mix_g': gain(ks[2], (DEPTH, D_MODEL)),
        'w_in': nrm(ks[3], (DEPTH, D_MODEL, IN_DIM), D_MODEL ** -0.5),
        'qnorm_a': gain(ks[4], (DEPTH, HEAD_DIM_A)),
        'knorm_a': gain(ks[5], (DEPTH, HEAD_DIM_A)),
        'sink_a': nrm(ks[6], (DEPTH, N_HEADS_A), 0.5),
        'conv_w': nrm(ks[7], (DEPTH, CONV_K, CONV_DIM), CONV_K ** -0.5),
        'conv_b': nrm(ks[8], (DEPTH, CONV_DIM), 0.01),
        'dt_bias': dt0 + jnp.log(-jnp.expm1(-dt0)),
        'a_log': jnp.log(jax.random.uniform(ks[10], (DEPTH, 2, SSD_HEADS), f32, 1.0, 16.0)),
        'd_skip': 1.0 + nrm(ks[11], (DEPTH, SSD_HEADS), 0.1),
        'ssd_norm_g': gain(ks[12], (DEPTH, D_INNER)),
        'qnorm_c': gain(ks[13], (DEPTH, DIFF_QK_DIM)),
        'knorm_c': gain(ks[14], (DEPTH, DIFF_QK_DIM)),
        'lam_c': nrm(ks[15], (DEPTH, 4, DIFF_QK_DIM), 0.1),
        'diff_norm_g': gain(ks[16], (DEPTH, DIFF_V_DIM)),
        'w_out': nrm(ks[17], (DEPTH, MIX_WIDTH, D_MODEL), MIX_WIDTH ** -0.5),
        'norm_ffn_g': gain(ks[18], (DEPTH, D_MODEL)),
        'ffn_w_gate': nrm(ks[19], (N_DENSE, D_MODEL, D_FF), D_MODEL ** -0.5),
        'ffn_w_up': nrm(ks[20], (N_DENSE, D_MODEL, D_FF), D_MODEL ** -0.5),
        'ffn_w_down': nrm(ks[21], (N_DENSE, D_FF, D_MODEL), D_FF ** -0.5),
        'router_w': nrm(ks[22], (N_MOE, D_MODEL, N_EXPERTS), D_MODEL ** -0.5),
        'moe_w_gate': nrm(ks[23], (N_MOE, N_EXPERTS, D_MODEL, D_FF_EXPERT), D_MODEL ** -0.5),
        'moe_w_up': nrm(ks[24], (N_MOE, N_EXPERTS, D_MODEL, D_FF_EXPERT), D_MODEL ** -0.5),
        'moe_w_down': nrm(ks[25], (N_MOE, N_EXPERTS, D_FF_EXPERT, D_MODEL), D_FF_EXPERT ** -0.5),
    }


def reference(x_prompt, x_sample, norm_mix_g, w_in, qnorm_a, knorm_a, sink_a, conv_w, conv_b, dt_bias,
              a_log, d_skip, ssd_norm_g, qnorm_c, knorm_c, lam_c, diff_norm_g, w_out, norm_ffn_g,
              ffn_w_gate, ffn_w_up, ffn_w_down, router_w, moe_w_gate, moe_w_up, moe_w_down):
    y_prompt = encoder_trunk(x_prompt, norm_mix_g, w_in, qnorm_a, knorm_a, sink_a, conv_w, conv_b, dt_bias,
                             a_log, d_skip, ssd_norm_g, qnorm_c, knorm_c, lam_c, diff_norm_g, w_out,
                             norm_ffn_g, ffn_w_gate, ffn_w_up, ffn_w_down, router_w, moe_w_gate, moe_w_up,
                             moe_w_down)
    y_sample = encoder_trunk(x_sample, norm_mix_g, w_in, qnorm_a, knorm_a, sink_a, conv_w, conv_b, dt_bias,
                             a_log, d_skip, ssd_norm_g, qnorm_c, knorm_c, lam_c, diff_norm_g, w_out,
                             norm_ffn_g, ffn_w_gate, ffn_w_up, ffn_w_down, router_w, moe_w_gate, moe_w_up,
                             moe_w_down)
    return (y_prompt, y_sample)
```

```python
import functools
import math

import numpy as np
import jax
import jax.numpy as jnp
from jax import lax
from jax.experimental import pallas as pl
from jax.experimental.pallas import tpu as pltpu

F32 = jnp.float32
BF16 = jnp.bfloat16

D_MODEL = 1024
DEPTH = 2
N_HEADS_A = 8
KV_HEADS_A = 2
HEAD_DIM_A = 64
WINDOW = 128
ATT_BLOCK = 128
SSD_HEADS = 4
SSD_HEAD_DIM = 64
D_INNER = SSD_HEADS * SSD_HEAD_DIM
SSD_GROUPS = 2
D_STATE = 64
CONV_K = 3
CHUNK = 128
DIFF_HEADS = 4
DIFF_QK_DIM = 32
DIFF_V_DIM = 64
WIDTH_A = N_HEADS_A * HEAD_DIM_A
WIDTH_C = DIFF_HEADS * DIFF_V_DIM
CONV_DIM = D_INNER + 2 * SSD_GROUPS * D_STATE
SPLIT_SIZES = (WIDTH_A, KV_HEADS_A * HEAD_DIM_A, KV_HEADS_A * HEAD_DIM_A,
               D_INNER, CONV_DIM, 2 * SSD_HEADS,
               DIFF_HEADS * 2 * DIFF_QK_DIM, DIFF_HEADS * 2 * DIFF_QK_DIM, WIDTH_C)
D_FF = 2816
N_EXPERTS = 8
D_FF_EXPERT = 3584
EPS = 1e-6

LOG2E = 1.4426950408889634
NEG_BIG = -1e30

LANES = 128
VMEM_LIMIT = 56 * 1024 * 1024

PA_W = 1024
PB_W = 768
PC_W = 768
DT_W = 128
PROJ_W = PA_W + PB_W + PC_W + DT_W

TM = 512
MOE_BLK = 512
FF_CHUNKS = 2


def _cparams(sem, vmem=VMEM_LIMIT):
    return pltpu.CompilerParams(dimension_semantics=sem, vmem_limit_bytes=vmem)


def _const_spec(shape):
    nd = len(shape)
    return pl.BlockSpec(shape, lambda *_: (0,) * nd)


def _resident_spec(shape):
    nd = len(shape)
    return pl.BlockSpec(shape, lambda *_: (0,) * nd, pipeline_mode=pl.Buffered(1))


def _block_diag(n, seg):
    idx = np.arange(n) // seg
    return jnp.asarray((idx[:, None] == idx[None, :]).astype(np.float32), dtype=BF16)


def _sigmoid(x):
    return 1.0 / (1.0 + jnp.exp(-x))


def _softplus(x):
    return jnp.maximum(x, 0.0) + jnp.log(1.0 + jnp.exp(-jnp.abs(x)))


def _dot(a, b):
    return jnp.dot(a, b, preferred_element_type=F32)


def _dot_nt(a, b):
    return lax.dot_general(a, b, (((1,), (1,)), ((), ())), preferred_element_type=F32)


def _dot_tn(a, b):
    return lax.dot_general(a, b, (((0,), (0,)), ((), ())), preferred_element_type=F32)


def _dot_hi(a, b):
    return jnp.dot(a, b, preferred_element_type=F32, precision=lax.Precision.HIGHEST)


def _seg_norm_store(y, lo, hi, seg, bd_ref, gain_ref, out_ref):
    for c0 in range(lo, hi, 256):
        blk = y[:, c0:c0 + 256]
        ss = _dot((blk * blk).astype(BF16), bd_ref[...]) * (1.0 / seg)
        out = blk * lax.rsqrt(ss + EPS) * gain_ref[:, c0:c0 + 256]
        out_ref[:, c0:c0 + 256] = out.astype(BF16)


def _norm_proj_kernel(x_ref, g_ref, w_ref, bd64_ref, bd32_ref, ga_ref, gc_ref,
                      pa_ref, pb_ref, pc_ref, dt_ref):
    x = x_ref[...]
    ms = jnp.mean(x * x, axis=-1, keepdims=True)
    h = (x * lax.rsqrt(ms + EPS) * g_ref[...]).astype(BF16)
    ya = _dot(h, w_ref[:, 0:PA_W])
    _seg_norm_store(ya, 0, 768, HEAD_DIM_A, bd64_ref, ga_ref, pa_ref)
    pa_ref[:, 768:PA_W] = ya[:, 768:PA_W].astype(BF16)
    pb_ref[...] = _dot(h, w_ref[:, PA_W:PA_W + PB_W]).astype(BF16)
    yc = _dot(h, w_ref[:, PA_W + PB_W:PA_W + PB_W + PC_W])
    _seg_norm_store(yc, 0, 512, DIFF_QK_DIM, bd32_ref, gc_ref, pc_ref)
    pc_ref[:, 512:PC_W] = yc[:, 512:PC_W].astype(BF16)
    dt_ref[...] = _dot(h, w_ref[:, PA_W + PB_W + PC_W:PROJ_W])


def _norm_proj(x, g, w_cat, ga, gc):
    t = x.shape[0]
    row = lambda w: pl.BlockSpec((TM, w), lambda i: (i, 0))
    return pl.pallas_call(
        _norm_proj_kernel,
        grid=(t // TM,),
        in_specs=[row(D_MODEL), _const_spec((1, D_MODEL)), _resident_spec((D_MODEL, PROJ_W)),
                  _const_spec((256, 256)), _const_spec((256, 256)),
                  _const_spec((1, 768)), _const_spec((1, 512))],
        out_specs=[row(PA_W), row(PB_W), row(PC_W), row(DT_W)],
        out_shape=[jax.ShapeDtypeStruct((t, PA_W), BF16), jax.ShapeDtypeStruct((t, PB_W), BF16),
                   jax.ShapeDtypeStruct((t, PC_W), BF16), jax.ShapeDtypeStruct((t, DT_W), F32)],
        compiler_params=_cparams(("parallel",)),
        name="norm_proj",
    )(x, g, w_cat, _block_diag(256, HEAD_DIM_A), _block_diag(256, DIFF_QK_DIM), ga, gc)


QB_A = 512


def _attn_a_kernel(cur_ref, prev_ref, next_ref, bias_ref, sink_ref, o_ref, k_s, v_s, *, n_blocks):
    i = pl.program_id(1)
    nsub = QB_A // ATT_BLOCK
    k_s[0:ATT_BLOCK] = prev_ref[:, 0:256]
    v_s[0:ATT_BLOCK] = prev_ref[:, 256:512]
    k_s[ATT_BLOCK:ATT_BLOCK + QB_A] = cur_ref[:, 512:768]
    v_s[ATT_BLOCK:ATT_BLOCK + QB_A] = cur_ref[:, 768:1024]
    k_s[ATT_BLOCK + QB_A:] = next_ref[:, 0:256]
    v_s[ATT_BLOCK + QB_A:] = next_ref[:, 256:512]
    lo = lax.broadcasted_iota(jnp.int32, (ATT_BLOCK, LANES), 1) < HEAD_DIM_A
    zero = jnp.zeros((ATT_BLOCK, LANES), BF16)
    for t in range(nsub):
        gb = i * nsub + t
        var = jnp.where(gb == 0, 0, jnp.where(gb == n_blocks - 1, 2, 1))
        r0 = t * ATT_BLOCK
        for j in range(KV_HEADS_A):
            q0 = cur_ref[r0:r0 + ATT_BLOCK, (2 * j) * LANES:(2 * j + 1) * LANES]
            q1 = cur_ref[r0:r0 + ATT_BLOCK, (2 * j + 1) * LANES:(2 * j + 2) * LANES]
            qs = jnp.concatenate([jnp.where(lo, q0, zero), jnp.where(lo, zero, q0),
                                  jnp.where(lo, q1, zero), jnp.where(lo, zero, q1)], axis=0)
            kw = k_s[r0:r0 + 3 * ATT_BLOCK, j * LANES:(j + 1) * LANES]
            vw = v_s[r0:r0 + 3 * ATT_BLOCK, j * LANES:(j + 1) * LANES]
            s = _dot_nt(qs, kw) + bias_ref[var, j]
            sk = sink_ref[j]
            m = jnp.maximum(jnp.max(s, axis=-1, keepdims=True), sk)
            p = jnp.exp2(s - m)
            den = jnp.sum(p, axis=-1, keepdims=True) + jnp.exp2(sk - m)
            o = _dot(p.astype(BF16), vw) * (1.0 / den)
            b = ATT_BLOCK
            o_ref[r0:r0 + b, (2 * j) * LANES:(2 * j + 1) * LANES] = (
                jnp.where(lo, o[0:b], o[b:2 * b]).astype(BF16))
            o_ref[r0:r0 + b, (2 * j + 1) * LANES:(2 * j + 2) * LANES] = (
                jnp.where(lo, o[2 * b:3 * b], o[3 * b:4 * b]).astype(BF16))


def _attn_a_consts(sink):
    b = ATT_BLOCK
    g = N_HEADS_A // KV_HEADS_A
    qi = np.arange(b)[:, None]
    s = np.arange(3 * b)[None, :]
    dist = np.abs(b + qi - s).astype(np.float32)
    in_win = dist <= WINDOW
    slopes = jnp.exp2(-8.0 * jnp.arange(1, N_HEADS_A + 1, dtype=F32) / N_HEADS_A) * LOG2E
    alibi = -slopes.reshape(KV_HEADS_A, g, 1, 1) * jnp.asarray(dist)[None, None]
    variants = []
    for valid in (s >= b, s >= 0, s < 2 * b):
        mask = jnp.asarray(in_win & valid)[None, None]
        variants.append(jnp.where(mask, alibi, NEG_BIG).reshape(KV_HEADS_A, g * b, 3 * b))
    bias = jnp.stack(variants)
    sink_rows = jnp.repeat(sink.astype(F32).reshape(KV_HEADS_A, g) * LOG2E, b, axis=1)[..., None]
    return bias, sink_rows


def _attn_a(pa, sink, batch, seq):
    t = batch * seq
    nq = seq // QB_A
    nblk = seq // ATT_BLOCK
    per = QB_A // ATT_BLOCK
    bias, sink_rows = _attn_a_consts(sink)
    g = N_HEADS_A // KV_HEADS_A
    return pl.pallas_call(
        functools.partial(_attn_a_kernel, n_blocks=nblk),
        grid=(batch, nq),
        in_specs=[
            pl.BlockSpec((QB_A, PA_W), lambda b, i: (b * nq + i, 0)),
            pl.BlockSpec((ATT_BLOCK, 512), lambda b, i: (b * nblk + jnp.maximum(i * per - 1, 0), 1)),
            pl.BlockSpec((ATT_BLOCK, 512), lambda b, i: (b * nblk + jnp.minimum((i + 1) * per, nblk - 1), 1)),
            _const_spec((3, KV_HEADS_A, g * ATT_BLOCK, 3 * ATT_BLOCK)),
            _const_spec((KV_HEADS_A, g * ATT_BLOCK, 1)),
        ],
        out_specs=pl.BlockSpec((QB_A, WIDTH_A), lambda b, i: (b * nq + i, 0)),
        out_shape=jax.ShapeDtypeStruct((t, WIDTH_A), BF16),
        scratch_shapes=[pltpu.VMEM((QB_A + 2 * ATT_BLOCK, 256), BF16),
                        pltpu.VMEM((QB_A + 2 * ATT_BLOCK, 256), BF16)],
        compiler_params=_cparams(("parallel", "parallel")),
        name="attn_a",
    )(pa, pa, pa, bias, sink_rows)


RB_S = 512
HALO = 16


def _ssd_kernel(*refs, rev, n_steps):
    if rev:
        (cur_ref, prev_ref, next_ref, dt_ref, dtt_ref, cw_ref, cb_ref, dtb_c_ref, dtb_r_ref,
         alog_c_ref, alog_r_ref, ltri_ref, utri_ref, dskip_ref, g_ref, yf_ref, out_ref, st_ref) = refs
    else:
        (cur_ref, prev_ref, next_ref, dt_ref, dtt_ref, cw_ref, cb_ref, dtb_c_ref, dtb_r_ref,
         alog_c_ref, alog_r_ref, ltri_ref, utri_ref, out_ref, st_ref) = refs
    i = pl.program_id(1)
    ii = (n_steps - 1 - i) if rev else i
    d = 1 if rev else 0

    @pl.when(i == 0)
    def _():
        st_ref[...] = jnp.zeros_like(st_ref)

    xc = cur_ref[:, 0:CONV_DIM].astype(F32)
    prow = jnp.where(ii == 0, 0.0, prev_ref[...].astype(F32)[HALO - 1:HALO, :])
    nrow = jnp.where(ii == n_steps - 1, 0.0, next_ref[...].astype(F32)[0:1, :])
    rid = lax.broadcasted_iota(jnp.int32, (RB_S, CONV_DIM), 0)
    xm1 = jnp.where(rid == 0, prow, pltpu.roll(xc, 1, 0))
    xp1 = jnp.where(rid == RB_S - 1, nrow, pltpu.roll(xc, RB_S - 1, 0))
    u = xm1 * cw_ref[0:1, :] + xc * cw_ref[1:2, :] + xp1 * cw_ref[2:3, :] + cb_ref[...]
    u = u * _sigmoid(u)

    dt_c = _softplus(dt_ref[:, 0:2 * SSD_HEADS] + dtb_c_ref[...])
    dt_r = _softplus(dtt_ref[...] + dtb_r_ref[...])
    da_c = dt_c * (-jnp.exp(alog_c_ref[...]))
    da_r = dt_r * (-jnp.exp(alog_r_ref[...]))

    lane = lax.broadcasted_iota(jnp.int32, (CHUNK, LANES), 1)
    lo = lane < SSD_HEAD_DIM
    lo_row = lax.broadcasted_iota(jnp.int32, (1, LANES), 1) < SSD_HEAD_DIM
    li = lax.broadcasted_iota(jnp.int32, (CHUNK, CHUNK), 0)
    si = lax.broadcasted_iota(jnp.int32, (CHUNK, CHUNK), 1)
    tri = (si >= li) if rev else (si <= li)

    n_chunks = RB_S // CHUNK
    order = range(n_chunks - 1, -1, -1) if rev else range(n_chunks)
    for c in order:
        r0 = c * CHUNK
        dac = da_c[r0:r0 + CHUNK]
        dar = da_r[:, r0:r0 + CHUNK]
        cs_c = _dot_hi(ltri_ref[...], dac)
        cs_r = _dot_hi(dar, utri_ref[...])
        tot = cs_c[CHUNK - 1:CHUNK, :]
        if rev:
            e_c = cs_c - dac
            e_r = cs_r - dar
            w_b = jnp.exp(e_c)
            w_c = jnp.exp(tot - e_c)
        else:
            e_c = cs_c
            e_r = cs_r
            w_b = jnp.exp(tot - cs_c)
            w_c = jnp.exp(cs_c)
        dec = jnp.exp(tot)
        dtc = dt_c[r0:r0 + CHUNK]
        uc = u[r0:r0 + CHUNK]
        bm = uc[:, D_INNER:D_INNER + LANES]
        cm = uc[:, D_INNER + LANES:D_INNER + 2 * LANES]
        for pr in range(SSD_GROUPS):
            gmask = lo if pr == 0 else jnp.logical_not(lo)
            cg = jnp.where(gmask, cm, 0.0).astype(BF16)
            bg = jnp.where(gmask, bm, 0.0).astype(BF16)
            gmat = _dot_nt(cg, bg)
            xp = uc[:, pr * LANES:(pr + 1) * LANES]
            l0 = d * SSD_HEADS + 2 * pr
            l1 = l0 + 1
            pair = lambda a: jnp.where(lo, a[:, l0:l0 + 1], a[:, l1:l1 + 1])
            xdt = xp * pair(dtc)
            y = jnp.zeros((CHUNK, LANES), F32)
            for hh, dl in ((0, l0), (1, l1)):
                ecol = e_c[:, dl:dl + 1]
                erow = e_r[dl:dl + 1, :]
                diff = (erow - ecol) if rev else (ecol - erow)
                dmat = jnp.where(tri, jnp.exp(jnp.minimum(diff, 0.0)), 0.0)
                hmask = lo if hh == 0 else jnp.logical_not(lo)
                y = y + _dot((gmat * dmat).astype(BF16), jnp.where(hmask, xdt, 0.0).astype(BF16))
            st = st_ref[pr]
            y = y + pair(w_c) * _dot(cg, st.astype(BF16))
            s_new = _dot_tn(bg, (xdt * pair(w_b)).astype(BF16))
            dec_pair = jnp.where(lo_row, dec[:, l0:l0 + 1], dec[:, l1:l1 + 1])
            st_ref[pr] = st * dec_pair + s_new
            if rev:
                cols = slice(pr * LANES, (pr + 1) * LANES)
                y = y + yf_ref[r0:r0 + CHUNK, cols] + xp * dskip_ref[:, cols]
            out_ref[r0:r0 + CHUNK, pr * LANES:(pr + 1) * LANES] = y

    if rev:
        z = cur_ref[:, CONV_DIM:PB_W].astype(F32)
        yz = out_ref[...] * (z * _sigmoid(z))
        ms = jnp.mean(yz * yz, axis=-1, keepdims=True)
        out_ref[...] = yz * lax.rsqrt(ms + EPS) * g_ref[...]


def _ssd(pb, dt, dtt, params, yf, batch, seq):
    rev = yf is not None
    t = batch * seq
    ns = seq // RB_S
    per = RB_S // HALO
    nh = seq // HALO
    pos = (lambda i: ns - 1 - i) if rev else (lambda i: i)
    ltri = jnp.asarray(np.tril(np.ones((CHUNK, CHUNK), np.float32)))
    in_specs = [
        pl.BlockSpec((RB_S, PB_W), lambda b, i: (b * ns + pos(i), 0)),
        pl.BlockSpec((HALO, CONV_DIM), lambda b, i: (b * nh + jnp.maximum(pos(i) * per - 1, 0), 0)),
        pl.BlockSpec((HALO, CONV_DIM), lambda b, i: (b * nh + jnp.minimum((pos(i) + 1) * per, nh - 1), 0)),
        pl.BlockSpec((RB_S, DT_W), lambda b, i: (b * ns + pos(i), 0)),
        pl.BlockSpec((2 * SSD_HEADS, RB_S), lambda b, i: (0, b * ns + pos(i))),
        _const_spec((CONV_K, CONV_DIM)), _const_spec((1, CONV_DIM)),
        _const_spec((1, 2 * SSD_HEADS)), _const_spec((2 * SSD_HEADS, 1)),
        _const_spec((1, 2 * SSD_HEADS)), _const_spec((2 * SSD_HEADS, 1)),
        _const_spec((CHUNK, CHUNK)), _const_spec((CHUNK, CHUNK)),
    ]
    args = [pb, pb, pb, dt, dtt, params["conv_w"], params["conv_b"], params["dtb_c"], params["dtb_r"],
            params["alog_c"], params["alog_r"], ltri, ltri.T]
    if rev:
        in_specs += [_const_spec((1, D_INNER)), _const_spec((1, D_INNER)),
                     pl.BlockSpec((RB_S, D_INNER), lambda b, i: (b * ns + pos(i), 0))]
        args += [params["dskip"], params["norm_g"], yf]
    return pl.pallas_call(
        functools.partial(_ssd_kernel, rev=rev, n_steps=ns),
        grid=(batch, ns),
        in_specs=in_specs,
        out_specs=pl.BlockSpec((RB_S, D_INNER), lambda b, i: (b * ns + pos(i), 0)),
        out_shape=jax.ShapeDtypeStruct((t, D_INNER), F32),
        scratch_shapes=[pltpu.VMEM((SSD_GROUPS, LANES, LANES), F32)],
        compiler_params=_cparams(("parallel", "arbitrary")),
        name="ssd_bwd" if rev else "ssd_fwd",
    )(*args)


TQ_C = 512
TK_C = 512
N_MAPS = 4


def _attn_c_kernel(slope_ref, lam_ref, q_ref, k_ref, v_ref, bd_ref, g_ref, o_ref, m_s, l_s, acc_s):
    hp = pl.program_id(1)
    qi = pl.program_id(2)
    ki = pl.program_id(3)

    @pl.when(ki == 0)
    def _():
        m_s[...] = jnp.full_like(m_s, NEG_BIG)
        l_s[...] = jnp.zeros_like(l_s)
        acc_s[...] = jnp.zeros_like(acc_s)

    q = q_ref[...]
    k = k_ref[...]
    v = v_ref[...]
    seg = lax.broadcasted_iota(jnp.int32, (TQ_C, LANES), 1) // DIFF_QK_DIM
    rel = (lax.broadcasted_iota(jnp.int32, (TQ_C, TK_C), 0)
           - lax.broadcasted_iota(jnp.int32, (TQ_C, TK_C), 1) + (qi * TQ_C - ki * TK_C))
    dist = jnp.abs(rel).astype(F32)
    zero = jnp.zeros((TQ_C, LANES), BF16)
    for hh in range(2):
        bias = dist * slope_ref[2 * hp + hh]
        for c in range(2):
            idx = 2 * hh + c
            s = _dot_nt(jnp.where(seg == idx, q, zero), k) - bias
            m_old = m_s[idx]
            m_new = jnp.maximum(m_old, jnp.max(s, axis=-1, keepdims=True))
            alpha = jnp.exp2(m_old - m_new)
            p = jnp.exp2(s - m_new)
            l_s[idx] = alpha * l_s[idx] + jnp.sum(p, axis=-1, keepdims=True)
            acc_s[idx] = alpha * acc_s[idx] + _dot(p.astype(BF16), v)
            m_s[idx] = m_new

    @pl.when(ki == pl.num_programs(3) - 1)
    def _():
        lam = lam_ref[0]
        outs = []
        for hh in range(2):
            o0 = acc_s[2 * hh] * (1.0 / l_s[2 * hh])
            o1 = acc_s[2 * hh + 1] * (1.0 / l_s[2 * hh + 1])
            outs.append(o0 - lam * o1)
        lo = lax.broadcasted_iota(jnp.int32, (TQ_C, LANES), 1) < DIFF_V_DIM
        o = jnp.where(lo, outs[0], outs[1])
        ss = _dot_hi(o * o, bd_ref[...]) * (1.0 / DIFF_V_DIM)
        o_ref[...] = (o * lax.rsqrt(ss + EPS) * g_ref[...]).astype(BF16)


def _attn_c(pc, slopes, lam, gain, batch, seq):
    t = batch * seq
    nq = seq // TQ_C
    nk = seq // TK_C
    bd = _block_diag(LANES, DIFF_V_DIM).astype(F32)
    grid_spec = pltpu.PrefetchScalarGridSpec(
        num_scalar_prefetch=2,
        grid=(batch, DIFF_HEADS // 2, nq, nk),
        in_specs=[
            pl.BlockSpec((TQ_C, LANES), lambda b, h, i, j, *_: (b * nq + i, h)),
            pl.BlockSpec((TK_C, LANES), lambda b, h, i, j, *_: (b * nk + j, 2 + h)),
            pl.BlockSpec((TK_C, LANES), lambda b, h, i, j, *_: (b * nk + j, 4 + h)),
            _const_spec((LANES, LANES)), _const_spec((1, LANES)),
        ],
        out_specs=pl.BlockSpec((TQ_C, LANES), lambda b, h, i, j, *_: (b * nq + i, h)),
        scratch_shapes=[pltpu.VMEM((N_MAPS, TQ_C, 1), F32), pltpu.VMEM((N_MAPS, TQ_C, 1), F32),
                        pltpu.VMEM((N_MAPS, TQ_C, LANES), F32)],
    )
    return pl.pallas_call(
        _attn_c_kernel,
        grid_spec=grid_spec,
        out_shape=jax.ShapeDtypeStruct((t, WIDTH_C), BF16),
        compiler_params=_cparams(("parallel", "parallel", "parallel", "arbitrary")),
        name="attn_c",
    )(slopes, lam, pc, pc, pc, bd, gain)


def _out_proj(oa_ref, ob_ref, oc_ref, x_ref, wo_ref):
    o = jnp.concatenate([oa_ref[...], ob_ref[...].astype(BF16), oc_ref[...]], axis=-1)
    return x_ref[...] + _dot(o, wo_ref[...])


def _rms(x, g_ref):
    ms = jnp.mean(x * x, axis=-1, keepdims=True)
    return x * lax.rsqrt(ms + EPS) * g_ref[...]


def _swiglu_acc(h, wg_ref, wu_ref, wd_ref, acc, d_ff):
    step = d_ff // FF_CHUNKS
    for c0 in range(0, d_ff, step):
        gate = _dot(h, wg_ref[:, c0:c0 + step])
        up = _dot(h, wu_ref[:, c0:c0 + step])
        act = (gate * _sigmoid(gate) * up).astype(BF16)
        acc = acc + _dot(act, wd_ref[c0:c0 + step, :])
    return acc


def _out_ffn_kernel(oa_ref, ob_ref, oc_ref, x_ref, wo_ref, g_ref, wg_ref, wu_ref, wd_ref, out_ref):
    x1 = _out_proj(oa_ref, ob_ref, oc_ref, x_ref, wo_ref)
    h = _rms(x1, g_ref).astype(BF16)
    out_ref[...] = _swiglu_acc(h, wg_ref, wu_ref, wd_ref, x1, D_FF)


def _mixer_out_specs():
    row = lambda w: pl.BlockSpec((TM, w), lambda i: (i, 0))
    return [row(WIDTH_A), row(D_INNER), row(WIDTH_C), row(D_MODEL), _resident_spec((D_MODEL, D_MODEL)),
            _const_spec((1, D_MODEL))]


def _out_ffn(oa, ob, oc, x, w_out, g, wg, wu, wd):
    t = x.shape[0]
    return pl.pallas_call(
        _out_ffn_kernel,
        grid=(t // TM,),
        in_specs=_mixer_out_specs() + [_resident_spec((D_MODEL, D_FF)), _resident_spec((D_MODEL, D_FF)),
                                       _resident_spec((D_FF, D_MODEL))],
        out_specs=pl.BlockSpec((TM, D_MODEL), lambda i: (i, 0)),
        out_shape=jax.ShapeDtypeStruct((t, D_MODEL), F32),
        compiler_params=_cparams(("parallel",)),
        name="out_ffn",
    )(oa, ob, oc, x, w_out, g, wg, wu, wd)


def _out_router_kernel(oa_ref, ob_ref, oc_ref, x_ref, wo_ref, g_ref, wr_hi_ref, wr_lo_ref,
                       x1_ref, h_ref, sel_ref, gate_ref):
    x1 = _out_proj(oa_ref, ob_ref, oc_ref, x_ref, wo_ref)
    x1_ref[...] = x1
    h = _rms(x1, g_ref)
    h_hi = h.astype(BF16)
    h_ref[...] = h_hi
    h_lo = (h - h_hi.astype(F32)).astype(BF16)
    logits = _dot(h_hi, wr_hi_ref[...]) + (_dot(h_lo, wr_hi_ref[...]) + _dot(h_hi, wr_lo_ref[...]))
    lane_i = lax.broadcasted_iota(jnp.int32, (TM, LANES), 1)
    lane = lane_i.astype(F32)
    logits = jnp.where(lane_i < N_EXPERTS, logits, -jnp.inf)
    v1 = jnp.max(logits, axis=-1, keepdims=True)
    i1 = jnp.min(jnp.where(logits == v1, lane, float(LANES)), axis=-1, keepdims=True)
    rest = jnp.where(lane == i1, -jnp.inf, logits)
    v2 = jnp.max(rest, axis=-1, keepdims=True)
    i2 = jnp.min(jnp.where(rest == v2, lane, float(LANES)), axis=-1, keepdims=True)
    e = jnp.exp(v2 - v1)
    g1 = 1.0 / (1.0 + e)
    sel_ref[...] = jnp.where(lane_i == 0, i1, jnp.where(lane_i == 1, i2, 0.0)).astype(jnp.int32)
    gate_ref[...] = jnp.where(lane_i == 0, g1, jnp.where(lane_i == 1, e * g1, 0.0))


def _out_router(oa, ob, oc, x, w_out, g, wr_hi, wr_lo):
    t = x.shape[0]
    row = lambda w: pl.BlockSpec((TM, w), lambda i: (i, 0))
    return pl.pallas_call(
        _out_router_kernel,
        grid=(t // TM,),
        in_specs=_mixer_out_specs() + [_const_spec((D_MODEL, LANES)), _const_spec((D_MODEL, LANES))],
        out_specs=[row(D_MODEL), row(D_MODEL), row(LANES), row(LANES)],
        out_shape=[jax.ShapeDtypeStruct((t, D_MODEL), F32), jax.ShapeDtypeStruct((t, D_MODEL), BF16),
                   jax.ShapeDtypeStruct((t, LANES), jnp.int32), jax.ShapeDtypeStruct((t, LANES), F32)],
        compiler_params=_cparams(("parallel",)),
        name="out_router",
    )(oa, ob, oc, x, w_out, g, wr_hi, wr_lo)


def _moe_ffn_kernel(blk_exp_ref, n_used_ref, x_ref, wg_ref, wu_ref, wd_ref, out_ref):
    i = pl.program_id(0)

    @pl.when(i < n_used_ref[0])
    def _():
        acc = jnp.zeros((MOE_BLK, D_MODEL), F32)
        out_ref[...] = _swiglu_acc(x_ref[...], wg_ref.at[0], wu_ref.at[0], wd_ref.at[0], acc, D_FF_EXPERT)

    @pl.when(i >= n_used_ref[0])
    def _():
        out_ref[...] = jnp.zeros_like(out_ref)


def _moe_ffn(xg, blk_exp, n_used, wg, wu, wd):
    rows = xg.shape[0]
    wspec = lambda shape: pl.BlockSpec((1,) + shape, lambda i, be, nu: (be[i], 0, 0),
                                       pipeline_mode=pl.Buffered(1))
    grid_spec = pltpu.PrefetchScalarGridSpec(
        num_scalar_prefetch=2,
        grid=(rows // MOE_BLK,),
        in_specs=[pl.BlockSpec((MOE_BLK, D_MODEL), lambda i, be, nu: (i, 0)),
                  wspec((D_MODEL, D_FF_EXPERT)), wspec((D_MODEL, D_FF_EXPERT)),
                  wspec((D_FF_EXPERT, D_MODEL))],
        out_specs=pl.BlockSpec((MOE_BLK, D_MODEL), lambda i, be, nu: (i, 0)),
    )
    return pl.pallas_call(
        _moe_ffn_kernel,
        grid_spec=grid_spec,
        out_shape=jax.ShapeDtypeStruct((rows, D_MODEL), F32),
        compiler_params=_cparams(("arbitrary",)),
        name="moe_ffn",
    )(blk_exp, n_used, xg, wg, wu, wd)


def _moe(x1, h, sel, gates, wg, wu, wd):
    t = x1.shape[0]
    i1, i2 = sel[:, 0], sel[:, 1]
    g1, g2 = gates[:, 0:1], gates[:, 1:2]
    experts = jnp.arange(N_EXPERTS, dtype=jnp.int32)
    onehot = ((i1[:, None] == experts) | (i2[:, None] == experts)).astype(jnp.int32)
    rank = jnp.cumsum(onehot, axis=0) - onehot
    counts = jnp.sum(onehot, axis=0)
    padded = ((counts + MOE_BLK - 1) // MOE_BLK) * MOE_BLK
    pad_end = jnp.cumsum(padded)
    pad_start = pad_end - padded
    dest = pad_start[None, :] + rank
    d1 = jnp.take_along_axis(dest, i1[:, None], axis=1)[:, 0]
    d2 = jnp.take_along_axis(dest, i2[:, None], axis=1)[:, 0]
    n_blocks = (2 * t) // MOE_BLK + N_EXPERTS
    rows = n_blocks * MOE_BLK
    tok = jnp.arange(t, dtype=jnp.int32)
    row_tok = jnp.full((rows,), t, jnp.int32).at[d1].set(tok).at[d2].set(tok)
    blk_exp = jnp.minimum(jnp.searchsorted(pad_end, jnp.arange(n_blocks, dtype=jnp.int32) * MOE_BLK,
                                           side="right"), N_EXPERTS - 1).astype(jnp.int32)
    n_used = (pad_end[-1:] // MOE_BLK).astype(jnp.int32)
    h_pad = jnp.concatenate([h, jnp.zeros((1, D_MODEL), h.dtype)], axis=0)
    yb = _moe_ffn(h_pad[row_tok], blk_exp, n_used, wg, wu, wd)
    return x1 + g1 * yb[d1] + g2 * yb[d2]


def _split_cols(w):
    outs, start = [], 0
    for n in SPLIT_SIZES:
        outs.append(w[:, start:start + n])
        start += n
    return outs


def _layer_params(l, p):
    qa, ka, va, zb, xbc, dtb, qc, kc, vc = _split_cols(p["w_in"][l])
    dup = lambda w: jnp.concatenate([w[:, :HEAD_DIM_A], w[:, :HEAD_DIM_A], w[:, HEAD_DIM_A:], w[:, HEAD_DIM_A:]], axis=1)
    w_cat = jnp.concatenate(
        [qa, dup(ka), dup(va), xbc, zb, qc, kc, vc, dtb, jnp.zeros((D_MODEL, DT_W - 2 * SSD_HEADS), F32)],
        axis=1).astype(BF16)
    ga = jnp.concatenate([jnp.tile(p["qnorm_a"][l] * (HEAD_DIM_A ** -0.5 * LOG2E), N_HEADS_A),
                          jnp.tile(p["knorm_a"][l], 2 * KV_HEADS_A)])[None, :]
    gc = jnp.concatenate([jnp.tile(p["qnorm_c"][l] * (DIFF_QK_DIM ** -0.5 * LOG2E), 2 * DIFF_HEADS),
                          jnp.tile(p["knorm_c"][l], 2 * DIFF_HEADS)])[None, :]
    lam_init = 0.8 - 0.6 * math.exp(-0.3 * l)
    lc = p["lam_c"][l].astype(F32)
    lam = jnp.exp(jnp.sum(lc[0] * lc[1])) - jnp.exp(jnp.sum(lc[2] * lc[3])) + lam_init
    out = {
        "norm_mix_g": p["norm_mix_g"][l][None, :], "w_cat": w_cat, "ga": ga, "gc": gc,
        "sink": p["sink_a"][l],
        "ssd": {
            "conv_w": p["conv_w"][l], "conv_b": p["conv_b"][l][None, :],
            "dtb_c": p["dt_bias"][l].reshape(1, -1), "dtb_r": p["dt_bias"][l].reshape(-1, 1),
            "alog_c": p["a_log"][l].reshape(1, -1), "alog_r": p["a_log"][l].reshape(-1, 1),
            "dskip": jnp.repeat(p["d_skip"][l], SSD_HEAD_DIM)[None, :],
            "norm_g": p["ssd_norm_g"][l][None, :],
        },
        "slopes_c": jnp.exp2(-8.0 * jnp.arange(1, DIFF_HEADS + 1, dtype=F32) / DIFF_HEADS) * LOG2E,
        "lam": jnp.reshape(lam, (1,)).astype(F32),
        "gain_c": (jnp.tile(p["diff_norm_g"][l], 2) * (1.0 - lam_init))[None, :],
        "w_out": p["w_out"][l].astype(BF16),
        "norm_ffn_g": p["norm_ffn_g"][l][None, :],
    }
    i = l // 2
    if l % 2 == 0:
        out["ffn"] = (p["ffn_w_gate"][i].astype(BF16), p["ffn_w_up"][i].astype(BF16),
                      p["ffn_w_down"][i].astype(BF16))
    else:
        wr = jnp.pad(p["router_w"][i], ((0, 0), (0, LANES - N_EXPERTS)))
        wr_hi = wr.astype(BF16)
        out["router"] = (wr_hi, (wr - wr_hi.astype(F32)).astype(BF16))
        out["moe"] = (p["moe_w_gate"][i].astype(BF16), p["moe_w_up"][i].astype(BF16),
                      p["moe_w_down"][i].astype(BF16))
    return out


def _trunk(x3, layers):
    batch, seq, _ = x3.shape
    x = x3.reshape(batch * seq, D_MODEL)
    for l, lp in enumerate(layers):
        pa, pb, pc, dt = _norm_proj(x, lp["norm_mix_g"], lp["w_cat"], lp["ga"], lp["gc"])
        oa = _attn_a(pa, lp["sink"], batch, seq)
        dtt = dt[:, :2 * SSD_HEADS].T
        yf = _ssd(pb, dt, dtt, lp["ssd"], None, batch, seq)
        ob = _ssd(pb, dt, dtt, lp["ssd"], yf, batch, seq)
        oc = _attn_c(pc, lp["slopes_c"], lp["lam"], lp["gain_c"], batch, seq)
        if l % 2 == 0:
            x = _out_ffn(oa, ob, oc, x, lp["w_out"], lp["norm_ffn_g"], *lp["ffn"])
        else:
            x1, h, sel, gates = _out_router(oa, ob, oc, x, lp["w_out"], lp["norm_ffn_g"], *lp["router"])
            x = _moe(x1, h, sel, gates, *lp["moe"])
    return x.reshape(batch, seq, D_MODEL)


def kernel(x_prompt, x_sample, norm_mix_g, w_in, qnorm_a, knorm_a, sink_a, conv_w, conv_b, dt_bias, a_log,
           d_skip, ssd_norm_g, qnorm_c, knorm_c, lam_c, diff_norm_g, w_out, norm_ffn_g, ffn_w_gate, ffn_w_up,
           ffn_w_down, router_w, moe_w_gate, moe_w_up, moe_w_down):
    p = dict(norm_mix_g=norm_mix_g, w_in=w_in, qnorm_a=qnorm_a, knorm_a=knorm_a, sink_a=sink_a,
             conv_w=conv_w, conv_b=conv_b, dt_bias=dt_bias, a_log=a_log, d_skip=d_skip,
             ssd_norm_g=ssd_norm_g, qnorm_c=qnorm_c, knorm_c=knorm_c, lam_c=lam_c, diff_norm_g=diff_norm_g,
             w_out=w_out, norm_ffn_g=norm_ffn_g, ffn_w_gate=ffn_w_gate, ffn_w_up=ffn_w_up,
             ffn_w_down=ffn_w_down, router_w=router_w, moe_w_gate=moe_w_gate, moe_w_up=moe_w_up,
             moe_w_down=moe_w_down)
    layers = [_layer_params(l, p) for l in range(DEPTH)]
    return _trunk(x_prompt, layers), _trunk(x_sample, layers)
```

```python
import functools
import math

import numpy as np
import jax
import jax.numpy as jnp
from jax import lax
from jax.experimental import pallas as pl
from jax.experimental.pallas import tpu as pltpu

F32 = jnp.float32
BF16 = jnp.bfloat16

D_MODEL = 1024
DEPTH = 2
N_HEADS_A = 8
KV_HEADS_A = 2
HEAD_DIM_A = 64
WINDOW = 128
ATT_BLOCK = 128
SSD_HEADS = 4
SSD_HEAD_DIM = 64
D_INNER = SSD_HEADS * SSD_HEAD_DIM
SSD_GROUPS = 2
D_STATE = 64
CONV_K = 3
CHUNK = 128
DIFF_HEADS = 4
DIFF_QK_DIM = 32
DIFF_V_DIM = 64
WIDTH_A = N_HEADS_A * HEAD_DIM_A
WIDTH_C = DIFF_HEADS * DIFF_V_DIM
CONV_DIM = D_INNER + 2 * SSD_GROUPS * D_STATE
SPLIT_SIZES = (WIDTH_A, KV_HEADS_A * HEAD_DIM_A, KV_HEADS_A * HEAD_DIM_A,
               D_INNER, CONV_DIM, 2 * SSD_HEADS,
               DIFF_HEADS * 2 * DIFF_QK_DIM, DIFF_HEADS * 2 * DIFF_QK_DIM, WIDTH_C)
D_FF = 2816
N_EXPERTS = 8
D_FF_EXPERT = 3584
EPS = 1e-6

LOG2E = 1.4426950408889634
NEG_BIG = -1e30

LANES = 128
VMEM_LIMIT = 56 * 1024 * 1024

PA_W = 768
PA_NORM_W = 640
VT_ROWS = 80
PB_W = 768
PC_W = 768
DT_W = 128
PROJ_W = PA_W + PB_W + PC_W + DT_W

TM = 512
MOE_BLK = 512
FF_CHUNKS = 2


def _cparams(sem, vmem=VMEM_LIMIT):
    return pltpu.CompilerParams(dimension_semantics=sem, vmem_limit_bytes=vmem)


def _const_spec(shape):
    nd = len(shape)
    return pl.BlockSpec(shape, lambda *_: (0,) * nd)


def _resident_spec(shape):
    nd = len(shape)
    return pl.BlockSpec(shape, lambda *_: (0,) * nd, pipeline_mode=pl.Buffered(1))


def _block_diag(n, seg):
    idx = np.arange(n) // seg
    return jnp.asarray((idx[:, None] == idx[None, :]).astype(np.float32), dtype=BF16)


def _sigmoid(x):
    return 1.0 / (1.0 + jnp.exp(-x))


def _softplus(x):
    return jnp.maximum(x, 0.0) + jnp.log(1.0 + jnp.exp(-jnp.abs(x)))


def _dot(a, b):
    return jnp.dot(a, b, preferred_element_type=F32)


def _dot_nt(a, b):
    return lax.dot_general(a, b, (((1,), (1,)), ((), ())), preferred_element_type=F32)


def _dot_tn(a, b):
    return lax.dot_general(a, b, (((0,), (0,)), ((), ())), preferred_element_type=F32)


def _dot_hi(a, b):
    return jnp.dot(a, b, preferred_element_type=F32, precision=lax.Precision.HIGHEST)


def _seg_norm_store(y, lo, hi, seg, bd_ref, gain_ref, out_ref):
    for c0 in range(lo, hi, 256):
        w = min(256, hi - c0)
        blk = y[:, c0:c0 + w]
        ss = _dot((blk * blk).astype(BF16), bd_ref[0:w, 0:w]) * (1.0 / seg)
        out = blk * lax.rsqrt(ss + EPS) * gain_ref[:, c0:c0 + w]
        out_ref[:, c0:c0 + w] = out.astype(BF16)


def _norm_proj_kernel(x_ref, g_ref, w_ref, bd64_ref, bd32_ref, ga_ref, gc_ref,
                      pa_ref, pb_ref, pc_ref, dt_ref):
    x = x_ref[...]
    ms = jnp.mean(x * x, axis=-1, keepdims=True)
    h = (x * lax.rsqrt(ms + EPS) * g_ref[...]).astype(BF16)
    ya = _dot(h, w_ref[:, 0:PA_W])
    _seg_norm_store(ya, 0, PA_NORM_W, HEAD_DIM_A, bd64_ref, ga_ref, pa_ref)
    pa_ref[:, PA_NORM_W:PA_W] = ya[:, PA_NORM_W:PA_W].astype(BF16)
    pb_ref[...] = _dot(h, w_ref[:, PA_W:PA_W + PB_W]).astype(BF16)
    yc = _dot(h, w_ref[:, PA_W + PB_W:PA_W + PB_W + PC_W])
    _seg_norm_store(yc, 0, 512, DIFF_QK_DIM, bd32_ref, gc_ref, pc_ref)
    pc_ref[:, 512:PC_W] = yc[:, 512:PC_W].astype(BF16)
    dt_ref[...] = _dot(h, w_ref[:, PA_W + PB_W + PC_W:PROJ_W])


def _norm_proj(x, g, w_cat, ga, gc):
    t = x.shape[0]
    row = lambda w: pl.BlockSpec((TM, w), lambda i: (i, 0))
    return pl.pallas_call(
        _norm_proj_kernel,
        grid=(t // TM,),
        in_specs=[row(D_MODEL), _const_spec((1, D_MODEL)), _resident_spec((D_MODEL, PROJ_W)),
                  _const_spec((256, 256)), _const_spec((256, 256)),
                  _const_spec((1, PA_NORM_W)), _const_spec((1, 512))],
        out_specs=[row(PA_W), row(PB_W), row(PC_W), row(DT_W)],
        out_shape=[jax.ShapeDtypeStruct((t, PA_W), BF16), jax.ShapeDtypeStruct((t, PB_W), BF16),
                   jax.ShapeDtypeStruct((t, PC_W), BF16), jax.ShapeDtypeStruct((t, DT_W), F32)],
        compiler_params=_cparams(("parallel",)),
        name="norm_proj",
    )(x, g, w_cat, _block_diag(256, HEAD_DIM_A), _block_diag(256, DIFF_QK_DIM), ga, gc)


QB_A = 512


GQA = N_HEADS_A // KV_HEADS_A
WIN_A = 3 * ATT_BLOCK
NQ_A = GQA * ATT_BLOCK


def _attn_a_kernel(qt_ref, k_ref, kp_ref, kn_ref, vt_ref, vtp_ref, vtn_ref, bias_ref, sink_ref, o_ref,
                   k_s, vt_s, rhs_s, *, n_blocks):
    i = pl.program_id(1)
    nsub = QB_A // ATT_BLOCK
    b = ATT_BLOCK
    k_s[0:b] = kp_ref[...]
    k_s[b:b + QB_A] = k_ref[...]
    k_s[b + QB_A:] = kn_ref[...]
    vt_s[:, 0:b] = vtp_ref[...]
    vt_s[:, b:b + QB_A] = vt_ref[...]
    vt_s[:, b + QB_A:] = vtn_ref[...]
    units = [(t, j) for t in range(nsub) for j in range(KV_HEADS_A)]

    def scores(u):
        t, j = units[u]
        other = 1 - j
        rhs_s[u, other * HEAD_DIM_A:(other + 1) * HEAD_DIM_A, :] = jnp.zeros((HEAD_DIM_A, NQ_A), BF16)
        for g in range(GQA):
            h = GQA * j + g
            rhs_s[u, j * HEAD_DIM_A:(j + 1) * HEAD_DIM_A, g * b:(g + 1) * b] = (
                qt_ref[h * HEAD_DIM_A:(h + 1) * HEAD_DIM_A, t * b:(t + 1) * b])
        gb = i * nsub + t
        var = jnp.where(gb == 0, 0, jnp.where(gb == n_blocks - 1, 2, 1))
        return _dot(k_s[t * b:t * b + WIN_A, :], rhs_s[u]) + bias_ref[var, j]

    s_next = scores(0)
    for u, (t, j) in enumerate(units):
        s = s_next
        if u + 1 < len(units):
            s_next = scores(u + 1)
        sk = sink_ref[j]
        m = jnp.maximum(jnp.max(s, axis=0, keepdims=True), sk)
        p = jnp.exp2(s - m)
        r = _dot(vt_s[j * VT_ROWS:(j + 1) * VT_ROWS, t * b:t * b + WIN_A], p.astype(BF16))
        den = r[HEAD_DIM_A:HEAD_DIM_A + 1] + jnp.exp2(sk - m)
        o = (r[0:HEAD_DIM_A] * (1.0 / den)).astype(BF16)
        for g in range(GQA):
            h = GQA * j + g
            o_ref[h * HEAD_DIM_A:(h + 1) * HEAD_DIM_A, t * b:(t + 1) * b] = o[:, g * b:(g + 1) * b]


def _attn_a_consts(sink):
    b = ATT_BLOCK
    qi = np.arange(b)[None, :]
    s = np.arange(WIN_A)[:, None]
    dist = np.abs(b + qi - s).astype(np.float32)
    in_win = dist <= WINDOW
    slopes = jnp.exp2(-8.0 * jnp.arange(1, N_HEADS_A + 1, dtype=F32) / N_HEADS_A) * LOG2E
    alibi = -slopes.reshape(KV_HEADS_A, 1, GQA, 1) * jnp.asarray(dist)[None, :, None, :]
    variants = []
    for valid in (s >= b, s >= 0, s < 2 * b):
        mask = jnp.asarray(in_win & valid)[None, :, None, :]
        variants.append(jnp.where(mask, alibi, NEG_BIG).reshape(KV_HEADS_A, WIN_A, NQ_A))
    bias = jnp.stack(variants)
    sink_cols = jnp.repeat(sink.astype(F32).reshape(KV_HEADS_A, GQA) * LOG2E, b, axis=1)[:, None, :]
    return bias, sink_cols


def _with_ones_rows(vt, heads, head_dim):
    batch, _, seq = vt.shape
    vt = vt.reshape(batch, heads, head_dim, seq)
    vt = jnp.concatenate([vt, jnp.ones((batch, heads, 1, seq), BF16),
                          jnp.zeros((batch, heads, VT_ROWS - head_dim - 1, seq), BF16)], axis=2)
    return vt.reshape(batch, heads * VT_ROWS, seq)


def _attn_a(pa, sink, batch, seq):
    nq = seq // QB_A
    nblk = seq // ATT_BLOCK
    per = QB_A // ATT_BLOCK
    bias, sink_cols = _attn_a_consts(sink)
    p3 = pa.reshape(batch, seq, PA_W)
    qt = jnp.swapaxes(p3[:, :, 0:WIDTH_A], 1, 2)
    vt = _with_ones_rows(jnp.swapaxes(p3[:, :, WIDTH_A + LANES:PA_W], 1, 2), KV_HEADS_A, HEAD_DIM_A)
    kcol = WIDTH_A // LANES
    prev = lambda i: jnp.maximum(i * per - 1, 0)
    nxt = lambda i: jnp.minimum((i + 1) * per, nblk - 1)
    vrows = KV_HEADS_A * VT_ROWS
    ot = pl.pallas_call(
        functools.partial(_attn_a_kernel, n_blocks=nblk),
        grid=(batch, nq),
        in_specs=[
            pl.BlockSpec((None, WIDTH_A, QB_A), lambda b, i: (b, 0, i)),
            pl.BlockSpec((QB_A, LANES), lambda b, i: (b * nq + i, kcol)),
            pl.BlockSpec((ATT_BLOCK, LANES), lambda b, i: (b * nblk + prev(i), kcol)),
            pl.BlockSpec((ATT_BLOCK, LANES), lambda b, i: (b * nblk + nxt(i), kcol)),
            pl.BlockSpec((None, vrows, QB_A), lambda b, i: (b, 0, i)),
            pl.BlockSpec((None, vrows, ATT_BLOCK), lambda b, i: (b, 0, prev(i))),
            pl.BlockSpec((None, vrows, ATT_BLOCK), lambda b, i: (b, 0, nxt(i))),
            _const_spec((3, KV_HEADS_A, WIN_A, NQ_A)),
            _const_spec((KV_HEADS_A, 1, NQ_A)),
        ],
        out_specs=pl.BlockSpec((None, WIDTH_A, QB_A), lambda b, i: (b, 0, i)),
        out_shape=jax.ShapeDtypeStruct((batch, WIDTH_A, seq), BF16),
        scratch_shapes=[pltpu.VMEM((QB_A + 2 * ATT_BLOCK, LANES), BF16),
                        pltpu.VMEM((vrows, QB_A + 2 * ATT_BLOCK), BF16),
                        pltpu.VMEM((per * KV_HEADS_A, LANES, NQ_A), BF16)],
        compiler_params=_cparams(("parallel", "parallel")),
        name="attn_a",
    )(qt, pa, pa, pa, vt, vt, vt, bias, sink_cols)
    return jnp.swapaxes(ot, 1, 2).reshape(batch * seq, WIDTH_A)


RB_S = 512
HALO = 16


def _ssd_kernel(*refs, rev, n_steps):
    if rev:
        (cur_ref, prev_ref, next_ref, dt_ref, dtt_ref, cw_ref, cb_ref, dtb_c_ref, dtb_r_ref,
         alog_c_ref, alog_r_ref, ltri_ref, utri_ref, dskip_ref, g_ref, yf_ref, out_ref, st_ref) = refs
    else:
        (cur_ref, prev_ref, next_ref, dt_ref, dtt_ref, cw_ref, cb_ref, dtb_c_ref, dtb_r_ref,
         alog_c_ref, alog_r_ref, ltri_ref, utri_ref, out_ref, st_ref) = refs
    i = pl.program_id(1)
    ii = (n_steps - 1 - i) if rev else i
    d = 1 if rev else 0

    @pl.when(i == 0)
    def _():
        st_ref[...] = jnp.zeros_like(st_ref)

    xc = cur_ref[:, 0:CONV_DIM].astype(F32)
    prow = jnp.where(ii == 0, 0.0, prev_ref[...].astype(F32)[HALO - 1:HALO, :])
    nrow = jnp.where(ii == n_steps - 1, 0.0, next_ref[...].astype(F32)[0:1, :])
    rid = lax.broadcasted_iota(jnp.int32, (RB_S, CONV_DIM), 0)
    xm1 = jnp.where(rid == 0, prow, pltpu.roll(xc, 1, 0))
    xp1 = jnp.where(rid == RB_S - 1, nrow, pltpu.roll(xc, RB_S - 1, 0))
    u = xm1 * cw_ref[0:1, :] + xc * cw_ref[1:2, :] + xp1 * cw_ref[2:3, :] + cb_ref[...]
    u = u * _sigmoid(u)

    dt_c = _softplus(dt_ref[:, 0:2 * SSD_HEADS] + dtb_c_ref[...])
    dt_r = _softplus(dtt_ref[...] + dtb_r_ref[...])
    da_c = dt_c * (-jnp.exp(alog_c_ref[...]))
    da_r = dt_r * (-jnp.exp(alog_r_ref[...]))

    lane = lax.broadcasted_iota(jnp.int32, (CHUNK, LANES), 1)
    lo = lane < SSD_HEAD_DIM
    lo_row = lax.broadcasted_iota(jnp.int32, (1, LANES), 1) < SSD_HEAD_DIM
    li = lax.broadcasted_iota(jnp.int32, (CHUNK, CHUNK), 0)
    si = lax.broadcasted_iota(jnp.int32, (CHUNK, CHUNK), 1)
    tri = (si >= li) if rev else (si <= li)

    n_chunks = RB_S // CHUNK
    order = range(n_chunks - 1, -1, -1) if rev else range(n_chunks)
    for c in order:
        r0 = c * CHUNK
        dac = da_c[r0:r0 + CHUNK]
        dar = da_r[:, r0:r0 + CHUNK]
        cs_c = _dot_hi(ltri_ref[...], dac)
        cs_r = _dot_hi(dar, utri_ref[...])
        tot = cs_c[CHUNK - 1:CHUNK, :]
        if rev:
            e_c = cs_c - dac
            e_r = cs_r - dar
            w_b = jnp.exp(e_c)
            w_c = jnp.exp(tot - e_c)
        else:
            e_c = cs_c
            e_r = cs_r
            w_b = jnp.exp(tot - cs_c)
            w_c = jnp.exp(cs_c)
        dec = jnp.exp(tot)
        dtc = dt_c[r0:r0 + CHUNK]
        uc = u[r0:r0 + CHUNK]
        bm = uc[:, D_INNER:D_INNER + LANES]
        cm = uc[:, D_INNER + LANES:D_INNER + 2 * LANES]
        for pr in range(SSD_GROUPS):
            gmask = lo if pr == 0 else jnp.logical_not(lo)
            cg = jnp.where(gmask, cm, 0.0).astype(BF16)
            bg = jnp.where(gmask, bm, 0.0).astype(BF16)
            gmat = _dot_nt(cg, bg)
            xp = uc[:, pr * LANES:(pr + 1) * LANES]
            l0 = d * SSD_HEADS + 2 * pr
            l1 = l0 + 1
            pair = lambda a: jnp.where(lo, a[:, l0:l0 + 1], a[:, l1:l1 + 1])
            xdt = xp * pair(dtc)
            y = jnp.zeros((CHUNK, LANES), F32)
            for hh, dl in ((0, l0), (1, l1)):
                ecol = e_c[:, dl:dl + 1]
                erow = e_r[dl:dl + 1, :]
                diff = (erow - ecol) if rev else (ecol - erow)
                dmat = jnp.where(tri, jnp.exp(jnp.minimum(diff, 0.0)), 0.0)
                hmask = lo if hh == 0 else jnp.logical_not(lo)
                y = y + _dot((gmat * dmat).astype(BF16), jnp.where(hmask, xdt, 0.0).astype(BF16))
            st = st_ref[pr]
            y = y + pair(w_c) * _dot(cg, st.astype(BF16))
            s_new = _dot_tn(bg, (xdt * pair(w_b)).astype(BF16))
            dec_pair = jnp.where(lo_row, dec[:, l0:l0 + 1], dec[:, l1:l1 + 1])
            st_ref[pr] = st * dec_pair + s_new
            if rev:
                cols = slice(pr * LANES, (pr + 1) * LANES)
                y = y + yf_ref[r0:r0 + CHUNK, cols] + xp * dskip_ref[:, cols]
            out_ref[r0:r0 + CHUNK, pr * LANES:(pr + 1) * LANES] = y

    if rev:
        z = cur_ref[:, CONV_DIM:PB_W].astype(F32)
        yz = out_ref[...] * (z * _sigmoid(z))
        ms = jnp.mean(yz * yz, axis=-1, keepdims=True)
        out_ref[...] = yz * lax.rsqrt(ms + EPS) * g_ref[...]


def _ssd(pb, dt, dtt, params, yf, batch, seq):
    rev = yf is not None
    t = batch * seq
    ns = seq // RB_S
    per = RB_S // HALO
    nh = seq // HALO
    pos = (lambda i: ns - 1 - i) if rev else (lambda i: i)
    ltri = jnp.asarray(np.tril(np.ones((CHUNK, CHUNK), np.float32)))
    in_specs = [
        pl.BlockSpec((RB_S, PB_W), lambda b, i: (b * ns + pos(i), 0)),
        pl.BlockSpec((HALO, CONV_DIM), lambda b, i: (b * nh + jnp.maximum(pos(i) * per - 1, 0), 0)),
        pl.BlockSpec((HALO, CONV_DIM), lambda b, i: (b * nh + jnp.minimum((pos(i) + 1) * per, nh - 1), 0)),
        pl.BlockSpec((RB_S, DT_W), lambda b, i: (b * ns + pos(i), 0)),
        pl.BlockSpec((2 * SSD_HEADS, RB_S), lambda b, i: (0, b * ns + pos(i))),
        _const_spec((CONV_K, CONV_DIM)), _const_spec((1, CONV_DIM)),
        _const_spec((1, 2 * SSD_HEADS)), _const_spec((2 * SSD_HEADS, 1)),
        _const_spec((1, 2 * SSD_HEADS)), _const_spec((2 * SSD_HEADS, 1)),
        _const_spec((CHUNK, CHUNK)), _const_spec((CHUNK, CHUNK)),
    ]
    args = [pb, pb, pb, dt, dtt, params["conv_w"], params["conv_b"], params["dtb_c"], params["dtb_r"],
            params["alog_c"], params["alog_r"], ltri, ltri.T]
    if rev:
        in_specs += [_const_spec((1, D_INNER)), _const_spec((1, D_INNER)),
                     pl.BlockSpec((RB_S, D_INNER), lambda b, i: (b * ns + pos(i), 0))]
        args += [params["dskip"], params["norm_g"], yf]
    return pl.pallas_call(
        functools.partial(_ssd_kernel, rev=rev, n_steps=ns),
        grid=(batch, ns),
        in_specs=in_specs,
        out_specs=pl.BlockSpec((RB_S, D_INNER), lambda b, i: (b * ns + pos(i), 0)),
        out_shape=jax.ShapeDtypeStruct((t, D_INNER), F32),
        scratch_shapes=[pltpu.VMEM((SSD_GROUPS, LANES, LANES), F32)],
        compiler_params=_cparams(("parallel", "arbitrary")),
        name="ssd_bwd" if rev else "ssd_fwd",
    )(*args)


TQ_C = 512
TK_C = 512
N_MAPS = 4
KV_SUB = 4


def _attn_c_kernel(slope_ref, lam_ref, qt_ref, k_ref, vt_ref, bias_ref, g_ref, o_ref, qz_s, m_s, acc_s):
    hp = pl.program_id(1)
    qi = pl.program_id(2)
    ki = pl.program_id(3)

    @pl.when(ki == 0)
    def _():
        m_s[...] = jnp.full_like(m_s, NEG_BIG)
        acc_s[...] = jnp.zeros_like(acc_s)
        qt = qt_ref[...]
        seg = lax.broadcasted_iota(jnp.int32, (LANES, TQ_C), 0) // DIFF_QK_DIM
        for idx in range(N_MAPS):
            qz_s[idx] = jnp.where(seg == idx, qt, jnp.zeros_like(qt))

    units = [(s, hh, c) for s in range(KV_SUB) for hh in range(2) for c in range(2)]

    def scores(u):
        s, hh, c = units[u]
        kt = ki * KV_SUB + s
        var = jnp.where(kt < qi, 0, jnp.where(kt == qi, 1, 2))
        return _dot(k_ref[s * TK_C:(s + 1) * TK_C, :], qz_s[2 * hh + c]) + bias_ref[var, hh]

    t_next = scores(0)
    for u, (s, hh, c) in enumerate(units):
        t = t_next
        if u + 1 < len(units):
            t_next = scores(u + 1)
        idx = 2 * hh + c
        kt = ki * KV_SUB + s
        c_tile = -jnp.abs(qi * TQ_C - kt * TK_C).astype(F32) * slope_ref[2 * hp + hh]
        m_old = m_s[idx]
        m_new = jnp.maximum(m_old, jnp.max(t, axis=0, keepdims=True) + c_tile)
        alpha = jnp.exp2(m_old - m_new)
        p = jnp.exp2(t - (m_new - c_tile))
        vt = vt_ref[hh * VT_ROWS:(hh + 1) * VT_ROWS, s * TK_C:(s + 1) * TK_C]
        acc_s[idx] = alpha * acc_s[idx] + _dot(vt, p.astype(BF16))
        m_s[idx] = m_new

    @pl.when(ki == pl.num_programs(3) - 1)
    def _():
        lam = lam_ref[0]
        for hh in range(2):
            a0 = acc_s[2 * hh]
            a1 = acc_s[2 * hh + 1]
            o = (a0[0:DIFF_V_DIM] * (1.0 / a0[DIFF_V_DIM:DIFF_V_DIM + 1])
                 - lam * (a1[0:DIFF_V_DIM] * (1.0 / a1[DIFF_V_DIM:DIFF_V_DIM + 1])))
            ms = jnp.mean(o * o, axis=0, keepdims=True)
            rows = slice(hh * DIFF_V_DIM, (hh + 1) * DIFF_V_DIM)
            o_ref[rows, :] = (o * lax.rsqrt(ms + EPS) * g_ref[rows, :]).astype(BF16)


def _attn_c_bias(slopes):
    rel = (np.arange(TQ_C)[None, :] - np.arange(TK_C)[:, None]).astype(np.float32)
    tables = jnp.asarray(np.stack([-rel, -np.abs(rel), rel]))
    s = slopes.reshape(DIFF_HEADS // 2, 1, 2, 1, 1)
    return s * tables[None, :, None]


def _attn_c(pc, slopes, lam, gain, batch, seq):
    nq = seq // TQ_C
    kv_step = TK_C * KV_SUB
    assert seq % kv_step == 0 and seq % TQ_C == 0
    nk = seq // kv_step
    p3 = pc.reshape(batch, seq, PC_W)
    qt = jnp.swapaxes(p3[:, :, 0:256], 1, 2)
    vt = _with_ones_rows(jnp.swapaxes(p3[:, :, 512:768], 1, 2), DIFF_HEADS, DIFF_V_DIM)
    gain_b = jnp.broadcast_to(gain.reshape(LANES, 1), (LANES, TQ_C))
    grid_spec = pltpu.PrefetchScalarGridSpec(
        num_scalar_prefetch=2,
        grid=(batch, DIFF_HEADS // 2, nq, nk),
        in_specs=[
            pl.BlockSpec((None, LANES, TQ_C), lambda b, h, i, j, *_: (b, h, i)),
            pl.BlockSpec((kv_step, LANES), lambda b, h, i, j, *_: (b * nk + j, 2 + h)),
            pl.BlockSpec((None, 2 * VT_ROWS, kv_step), lambda b, h, i, j, *_: (b, h, j)),
            pl.BlockSpec((None, 3, 2, TK_C, TQ_C), lambda b, h, i, j, *_: (h, 0, 0, 0, 0)),
            _const_spec((LANES, TQ_C)),
        ],
        out_specs=pl.BlockSpec((None, LANES, TQ_C), lambda b, h, i, j, *_: (b, h, i)),
        scratch_shapes=[pltpu.VMEM((N_MAPS, LANES, TQ_C), BF16), pltpu.VMEM((N_MAPS, 1, TQ_C), F32),
                        pltpu.VMEM((N_MAPS, VT_ROWS, TQ_C), F32)],
    )
    ot = pl.pallas_call(
        _attn_c_kernel,
        grid_spec=grid_spec,
        out_shape=jax.ShapeDtypeStruct((batch, WIDTH_C, seq), BF16),
        compiler_params=_cparams(("parallel", "parallel", "parallel", "arbitrary")),
        name="attn_c",
    )(slopes, lam, qt, pc, vt, _attn_c_bias(slopes), gain_b)
    return jnp.swapaxes(ot, 1, 2).reshape(batch * seq, WIDTH_C)


def _out_proj(oa_ref, ob_ref, oc_ref, x_ref, wo_ref):
    o = jnp.concatenate([oa_ref[...], ob_ref[...].astype(BF16), oc_ref[...]], axis=-1)
    return x_ref[...] + _dot(o, wo_ref[...])


def _rms(x, g_ref):
    ms = jnp.mean(x * x, axis=-1, keepdims=True)
    return x * lax.rsqrt(ms + EPS) * g_ref[...]


def _swiglu_acc(h, wg_ref, wu_ref, wd_ref, acc, d_ff):
    step = d_ff // FF_CHUNKS
    for c0 in range(0, d_ff, step):
        gate = _dot(h, wg_ref[:, c0:c0 + step])
        up = _dot(h, wu_ref[:, c0:c0 + step])
        act = (gate * _sigmoid(gate) * up).astype(BF16)
        acc = acc + _dot(act, wd_ref[c0:c0 + step, :])
    return acc


def _out_ffn_kernel(oa_ref, ob_ref, oc_ref, x_ref, wo_ref, g_ref, wg_ref, wu_ref, wd_ref, out_ref):
    x1 = _out_proj(oa_ref, ob_ref, oc_ref, x_ref, wo_ref)
    h = _rms(x1, g_ref).astype(BF16)
    out_ref[...] = _swiglu_acc(h, wg_ref, wu_ref, wd_ref, x1, D_FF)


def _mixer_out_specs():
    row = lambda w: pl.BlockSpec((TM, w), lambda i: (i, 0))
    return [row(WIDTH_A), row(D_INNER), row(WIDTH_C), row(D_MODEL), _resident_spec((D_MODEL, D_MODEL)),
            _const_spec((1, D_MODEL))]


def _out_ffn(oa, ob, oc, x, w_out, g, wg, wu, wd):
    t = x.shape[0]
    return pl.pallas_call(
        _out_ffn_kernel,
        grid=(t // TM,),
        in_specs=_mixer_out_specs() + [_resident_spec((D_MODEL, D_FF)), _resident_spec((D_MODEL, D_FF)),
                                       _resident_spec((D_FF, D_MODEL))],
        out_specs=pl.BlockSpec((TM, D_MODEL), lambda i: (i, 0)),
        out_shape=jax.ShapeDtypeStruct((t, D_MODEL), F32),
        compiler_params=_cparams(("parallel",)),
        name="out_ffn",
    )(oa, ob, oc, x, w_out, g, wg, wu, wd)


def _out_router_kernel(oa_ref, ob_ref, oc_ref, x_ref, wo_ref, g_ref, wr_hi_ref, wr_lo_ref,
                       x1_ref, h_ref, sel_ref, gate_ref):
    x1 = _out_proj(oa_ref, ob_ref, oc_ref, x_ref, wo_ref)
    x1_ref[...] = x1
    h = _rms(x1, g_ref)
    h_hi = h.astype(BF16)
    h_ref[...] = h_hi
    h_lo = (h - h_hi.astype(F32)).astype(BF16)
    logits = _dot(h_hi, wr_hi_ref[...]) + (_dot(h_lo, wr_hi_ref[...]) + _dot(h_hi, wr_lo_ref[...]))
    lane_i = lax.broadcasted_iota(jnp.int32, (TM, LANES), 1)
    lane = lane_i.astype(F32)
    logits = jnp.where(lane_i < N_EXPERTS, logits, -jnp.inf)
    v1 = jnp.max(logits, axis=-1, keepdims=True)
    i1 = jnp.min(jnp.where(logits == v1, lane, float(LANES)), axis=-1, keepdims=True)
    rest = jnp.where(lane == i1, -jnp.inf, logits)
    v2 = jnp.max(rest, axis=-1, keepdims=True)
    i2 = jnp.min(jnp.where(rest == v2, lane, float(LANES)), axis=-1, keepdims=True)
    e = jnp.exp(v2 - v1)
    g1 = 1.0 / (1.0 + e)
    sel_ref[...] = jnp.where(lane_i == 0, i1, jnp.where(lane_i == 1, i2, 0.0)).astype(jnp.int32)
    gate_ref[...] = jnp.where(lane_i == 0, g1, jnp.where(lane_i == 1, e * g1, 0.0))


def _out_router(oa, ob, oc, x, w_out, g, wr_hi, wr_lo):
    t = x.shape[0]
    row = lambda w: pl.BlockSpec((TM, w), lambda i: (i, 0))
    return pl.pallas_call(
        _out_router_kernel,
        grid=(t // TM,),
        in_specs=_mixer_out_specs() + [_const_spec((D_MODEL, LANES)), _const_spec((D_MODEL, LANES))],
        out_specs=[row(D_MODEL), row(D_MODEL), row(LANES), row(LANES)],
        out_shape=[jax.ShapeDtypeStruct((t, D_MODEL), F32), jax.ShapeDtypeStruct((t, D_MODEL), BF16),
                   jax.ShapeDtypeStruct((t, LANES), jnp.int32), jax.ShapeDtypeStruct((t, LANES), F32)],
        compiler_params=_cparams(("parallel",)),
        name="out_router",
    )(oa, ob, oc, x, w_out, g, wr_hi, wr_lo)


def _moe_ffn_kernel(blk_exp_ref, n_used_ref, x_ref, wg_ref, wu_ref, wd_ref, out_ref):
    i = pl.program_id(0)

    @pl.when(i < n_used_ref[0])
    def _():
        acc = jnp.zeros((MOE_BLK, D_MODEL), F32)
        out_ref[...] = _swiglu_acc(x_ref[...], wg_ref.at[0], wu_ref.at[0], wd_ref.at[0], acc, D_FF_EXPERT)

    @pl.when(i >= n_used_ref[0])
    def _():
        out_ref[...] = jnp.zeros_like(out_ref)


def _moe_ffn(xg, blk_exp, n_used, wg, wu, wd):
    rows = xg.shape[0]
    wspec = lambda shape: pl.BlockSpec((1,) + shape, lambda i, be, nu: (be[i], 0, 0),
                                       pipeline_mode=pl.Buffered(1))
    grid_spec = pltpu.PrefetchScalarGridSpec(
        num_scalar_prefetch=2,
        grid=(rows // MOE_BLK,),
        in_specs=[pl.BlockSpec((MOE_BLK, D_MODEL), lambda i, be, nu: (i, 0)),
                  wspec((D_MODEL, D_FF_EXPERT)), wspec((D_MODEL, D_FF_EXPERT)),
                  wspec((D_FF_EXPERT, D_MODEL))],
        out_specs=pl.BlockSpec((MOE_BLK, D_MODEL), lambda i, be, nu: (i, 0)),
    )
    return pl.pallas_call(
        _moe_ffn_kernel,
        grid_spec=grid_spec,
        out_shape=jax.ShapeDtypeStruct((rows, D_MODEL), F32),
        compiler_params=_cparams(("arbitrary",)),
        name="moe_ffn",
    )(blk_exp, n_used, xg, wg, wu, wd)


def _moe(x1, h, sel, gates, wg, wu, wd):
    t = x1.shape[0]
    i1, i2 = sel[:, 0], sel[:, 1]
    g1, g2 = gates[:, 0:1], gates[:, 1:2]
    experts = jnp.arange(N_EXPERTS, dtype=jnp.int32)
    onehot = ((i1[:, None] == experts) | (i2[:, None] == experts)).astype(jnp.int32)
    rank = jnp.cumsum(onehot, axis=0) - onehot
    counts = jnp.sum(onehot, axis=0)
    padded = ((counts + MOE_BLK - 1) // MOE_BLK) * MOE_BLK
    pad_end = jnp.cumsum(padded)
    pad_start = pad_end - padded
    dest = pad_start[None, :] + rank
    d1 = jnp.take_along_axis(dest, i1[:, None], axis=1)[:, 0]
    d2 = jnp.take_along_axis(dest, i2[:, None], axis=1)[:, 0]
    n_blocks = (2 * t) // MOE_BLK + N_EXPERTS
    rows = n_blocks * MOE_BLK
    tok = jnp.arange(t, dtype=jnp.int32)
    row_tok = jnp.full((rows,), t, jnp.int32).at[d1].set(tok).at[d2].set(tok)
    blk_start = jnp.arange(n_blocks, dtype=jnp.int32) * MOE_BLK
    blk_exp = jnp.minimum(jnp.sum((pad_end[None, :] <= blk_start[:, None]).astype(jnp.int32), axis=1),
                          N_EXPERTS - 1)
    n_used = (pad_end[-1:] // MOE_BLK).astype(jnp.int32)
    h_pad = jnp.concatenate([h, jnp.zeros((1, D_MODEL), h.dtype)], axis=0)
    yb = _moe_ffn(h_pad[row_tok], blk_exp, n_used, wg, wu, wd)
    return x1 + g1 * yb[d1] + g2 * yb[d2]


def _split_cols(w):
    outs, start = [], 0
    for n in SPLIT_SIZES:
        outs.append(w[:, start:start + n])
        start += n
    return outs


def _layer_params(l, p):
    qa, ka, va, zb, xbc, dtb, qc, kc, vc = _split_cols(p["w_in"][l])
    w_cat = jnp.concatenate(
        [qa, ka, va, xbc, zb, qc, kc, vc, dtb, jnp.zeros((D_MODEL, DT_W - 2 * SSD_HEADS), F32)],
        axis=1).astype(BF16)
    ga = jnp.concatenate([jnp.tile(p["qnorm_a"][l] * (HEAD_DIM_A ** -0.5 * LOG2E), N_HEADS_A),
                          jnp.tile(p["knorm_a"][l], KV_HEADS_A)])[None, :]
    gc = jnp.concatenate([jnp.tile(p["qnorm_c"][l] * (DIFF_QK_DIM ** -0.5 * LOG2E), 2 * DIFF_HEADS),
                          jnp.tile(p["knorm_c"][l], 2 * DIFF_HEADS)])[None, :]
    lam_init = 0.8 - 0.6 * math.exp(-0.3 * l)
    lc = p["lam_c"][l].astype(F32)
    lam = jnp.exp(jnp.sum(lc[0] * lc[1])) - jnp.exp(jnp.sum(lc[2] * lc[3])) + lam_init
    out = {
        "norm_mix_g": p["norm_mix_g"][l][None, :], "w_cat": w_cat, "ga": ga, "gc": gc,
        "sink": p["sink_a"][l],
        "ssd": {
            "conv_w": p["conv_w"][l], "conv_b": p["conv_b"][l][None, :],
            "dtb_c": p["dt_bias"][l].reshape(1, -1), "dtb_r": p["dt_bias"][l].reshape(-1, 1),
            "alog_c": p["a_log"][l].reshape(1, -1), "alog_r": p["a_log"][l].reshape(-1, 1),
            "dskip": jnp.repeat(p["d_skip"][l], SSD_HEAD_DIM)[None, :],
            "norm_g": p["ssd_norm_g"][l][None, :],
        },
        "slopes_c": jnp.exp2(-8.0 * jnp.arange(1, DIFF_HEADS + 1, dtype=F32) / DIFF_HEADS) * LOG2E,
        "lam": jnp.reshape(lam, (1,)).astype(F32),
        "gain_c": (jnp.tile(p["diff_norm_g"][l], 2) * (1.0 - lam_init))[None, :],
        "w_out": p["w_out"][l].astype(BF16),
        "norm_ffn_g": p["norm_ffn_g"][l][None, :],
    }
    i = l // 2
    if l % 2 == 0:
        out["ffn"] = (p["ffn_w_gate"][i].astype(BF16), p["ffn_w_up"][i].astype(BF16),
                      p["ffn_w_down"][i].astype(BF16))
    else:
        wr = jnp.pad(p["router_w"][i], ((0, 0), (0, LANES - N_EXPERTS)))
        wr_hi = wr.astype(BF16)
        out["router"] = (wr_hi, (wr - wr_hi.astype(F32)).astype(BF16))
        out["moe"] = (p["moe_w_gate"][i].astype(BF16), p["moe_w_up"][i].astype(BF16),
                      p["moe_w_down"][i].astype(BF16))
    return out


def _trunk(x3, layers):
    batch, seq, _ = x3.shape
    x = x3.reshape(batch * seq, D_MODEL)
    for l, lp in enumerate(layers):
        pa, pb, pc, dt = _norm_proj(x, lp["norm_mix_g"], lp["w_cat"], lp["ga"], lp["gc"])
        oa = _attn_a(pa, lp["sink"], batch, seq)
        dtt = dt[:, :2 * SSD_HEADS].T
        yf = _ssd(pb, dt, dtt, lp["ssd"], None, batch, seq)
        ob = _ssd(pb, dt, dtt, lp["ssd"], yf, batch, seq)
        oc = _attn_c(pc, lp["slopes_c"], lp["lam"], lp["gain_c"], batch, seq)
        if l % 2 == 0:
            x = _out_ffn(oa, ob, oc, x, lp["w_out"], lp["norm_ffn_g"], *lp["ffn"])
        else:
            x1, h, sel, gates = _out_router(oa, ob, oc, x, lp["w_out"], lp["norm_ffn_g"], *lp["router"])
            x = _moe(x1, h, sel, gates, *lp["moe"])
    return x.reshape(batch, seq, D_MODEL)


def kernel(x_prompt, x_sample, norm_mix_g, w_in, qnorm_a, knorm_a, sink_a, conv_w, conv_b, dt_bias, a_log,
           d_skip, ssd_norm_g, qnorm_c, knorm_c, lam_c, diff_norm_g, w_out, norm_ffn_g, ffn_w_gate, ffn_w_up,
           ffn_w_down, router_w, moe_w_gate, moe_w_up, moe_w_down):
    p = dict(norm_mix_g=norm_mix_g, w_in=w_in, qnorm_a=qnorm_a, knorm_a=knorm_a, sink_a=sink_a,
             conv_w=conv_w, conv_b=conv_b, dt_bias=dt_bias, a_log=a_log, d_skip=d_skip,
             ssd_norm_g=ssd_norm_g, qnorm_c=qnorm_c, knorm_c=knorm_c, lam_c=lam_c, diff_norm_g=diff_norm_g,
             w_out=w_out, norm_ffn_g=norm_ffn_g, ffn_w_gate=ffn_w_gate, ffn_w_up=ffn_w_up,
             ffn_w_down=ffn_w_down, router_w=router_w, moe_w_gate=moe_w_gate, moe_w_up=moe_w_up,
             moe_w_down=moe_w_down)
    layers = [_layer_params(l, p) for l in range(DEPTH)]
    return _trunk(x_prompt, layers), _trunk(x_sample, layers)
```

```python
import functools
import math

import numpy as np
import jax
import jax.numpy as jnp
from jax import lax
from jax.experimental import pallas as pl
from jax.experimental.pallas import tpu as pltpu

F32 = jnp.float32
BF16 = jnp.bfloat16

D_MODEL = 1024
DEPTH = 2
N_HEADS_A = 8
KV_HEADS_A = 2
HEAD_DIM_A = 64
WINDOW = 128
ATT_BLOCK = 128
SSD_HEADS = 4
SSD_HEAD_DIM = 64
D_INNER = SSD_HEADS * SSD_HEAD_DIM
SSD_GROUPS = 2
D_STATE = 64
CONV_K = 3
CHUNK = 128
DIFF_HEADS = 4
DIFF_QK_DIM = 32
DIFF_V_DIM = 64
WIDTH_A = N_HEADS_A * HEAD_DIM_A
WIDTH_C = DIFF_HEADS * DIFF_V_DIM
CONV_DIM = D_INNER + 2 * SSD_GROUPS * D_STATE
SPLIT_SIZES = (WIDTH_A, KV_HEADS_A * HEAD_DIM_A, KV_HEADS_A * HEAD_DIM_A,
               D_INNER, CONV_DIM, 2 * SSD_HEADS,
               DIFF_HEADS * 2 * DIFF_QK_DIM, DIFF_HEADS * 2 * DIFF_QK_DIM, WIDTH_C)
D_FF = 2816
N_EXPERTS = 8
D_FF_EXPERT = 3584
EPS = 1e-6

LOG2E = 1.4426950408889634
NEG_BIG = -1e30

LANES = 128
VMEM_LIMIT = 56 * 1024 * 1024

PA_W = 768
PA_NORM_W = 640
VT_ROWS = 80
PB_W = 768
PC_W = 768
DT_W = 128
PROJ_W = PA_W + PB_W + PC_W + DT_W

TM = 512
MOE_BLK = 512
FF_CHUNKS = 2


def _cparams(sem, vmem=VMEM_LIMIT):
    return pltpu.CompilerParams(dimension_semantics=sem, vmem_limit_bytes=vmem)


def _const_spec(shape):
    nd = len(shape)
    return pl.BlockSpec(shape, lambda *_: (0,) * nd)


def _resident_spec(shape):
    nd = len(shape)
    return pl.BlockSpec(shape, lambda *_: (0,) * nd, pipeline_mode=pl.Buffered(1))


def _block_diag(n, seg):
    idx = np.arange(n) // seg
    return jnp.asarray((idx[:, None] == idx[None, :]).astype(np.float32), dtype=BF16)


def _sigmoid(x):
    return 1.0 / (1.0 + jnp.exp(-x))


def _softplus(x):
    return jnp.maximum(x, 0.0) + jnp.log(1.0 + jnp.exp(-jnp.abs(x)))


def _dot(a, b):
    return jnp.dot(a, b, preferred_element_type=F32)


def _dot_nt(a, b):
    return lax.dot_general(a, b, (((1,), (1,)), ((), ())), preferred_element_type=F32)


def _dot_tn(a, b):
    return lax.dot_general(a, b, (((0,), (0,)), ((), ())), preferred_element_type=F32)


def _dot_hi(a, b):
    return jnp.dot(a, b, preferred_element_type=F32, precision=lax.Precision.HIGHEST)


def _seg_norm_store(y, lo, hi, seg, bd_ref, gain_ref, out_ref):
    for c0 in range(lo, hi, 256):
        w = min(256, hi - c0)
        blk = y[:, c0:c0 + w]
        ss = _dot((blk * blk).astype(BF16), bd_ref[0:w, 0:w]) * (1.0 / seg)
        out = blk * lax.rsqrt(ss + EPS) * gain_ref[:, c0:c0 + w]
        out_ref[:, c0:c0 + w] = out.astype(BF16)


def _norm_proj_kernel(x_ref, g_ref, w_ref, bd64_ref, bd32_ref, ga_ref, gc_ref,
                      pa_ref, pb_ref, pc_ref, dt_ref):
    x = x_ref[...]
    ms = jnp.mean(x * x, axis=-1, keepdims=True)
    h = (x * lax.rsqrt(ms + EPS) * g_ref[...]).astype(BF16)
    ya = _dot(h, w_ref[:, 0:PA_W])
    _seg_norm_store(ya, 0, PA_NORM_W, HEAD_DIM_A, bd64_ref, ga_ref, pa_ref)
    pa_ref[:, PA_NORM_W:PA_W] = ya[:, PA_NORM_W:PA_W].astype(BF16)
    pb_ref[...] = _dot(h, w_ref[:, PA_W:PA_W + PB_W]).astype(BF16)
    yc = _dot(h, w_ref[:, PA_W + PB_W:PA_W + PB_W + PC_W])
    _seg_norm_store(yc, 0, 512, DIFF_QK_DIM, bd32_ref, gc_ref, pc_ref)
    pc_ref[:, 512:PC_W] = yc[:, 512:PC_W].astype(BF16)
    dt_ref[...] = _dot(h, w_ref[:, PA_W + PB_W + PC_W:PROJ_W])


def _norm_proj(x, g, w_cat, ga, gc):
    t = x.shape[0]
    row = lambda w: pl.BlockSpec((TM, w), lambda i: (i, 0))
    return pl.pallas_call(
        _norm_proj_kernel,
        grid=(t // TM,),
        in_specs=[row(D_MODEL), _const_spec((1, D_MODEL)), _resident_spec((D_MODEL, PROJ_W)),
                  _const_spec((256, 256)), _const_spec((256, 256)),
                  _const_spec((1, PA_NORM_W)), _const_spec((1, 512))],
        out_specs=[row(PA_W), row(PB_W), row(PC_W), row(DT_W)],
        out_shape=[jax.ShapeDtypeStruct((t, PA_W), BF16), jax.ShapeDtypeStruct((t, PB_W), BF16),
                   jax.ShapeDtypeStruct((t, PC_W), BF16), jax.ShapeDtypeStruct((t, DT_W), F32)],
        compiler_params=_cparams(("parallel",)),
        name="norm_proj",
    )(x, g, w_cat, _block_diag(256, HEAD_DIM_A), _block_diag(256, DIFF_QK_DIM), ga, gc)


QB_A = 512


GQA = N_HEADS_A // KV_HEADS_A
WIN_A = 3 * ATT_BLOCK
NQ_A = GQA * ATT_BLOCK


def _attn_a_kernel(qt_ref, k_ref, kp_ref, kn_ref, vt_ref, vtp_ref, vtn_ref, bias_ref, sink_ref, o_ref,
                   k_s, vt_s, rhs_s, *, n_blocks):
    i = pl.program_id(1)
    nsub = QB_A // ATT_BLOCK
    b = ATT_BLOCK
    k_s[0:b] = kp_ref[...]
    k_s[b:b + QB_A] = k_ref[...]
    k_s[b + QB_A:] = kn_ref[...]
    vt_s[:, 0:b] = vtp_ref[...]
    vt_s[:, b:b + QB_A] = vt_ref[...]
    vt_s[:, b + QB_A:] = vtn_ref[...]
    units = [(t, j) for t in range(nsub) for j in range(KV_HEADS_A)]

    def scores(u):
        t, j = units[u]
        other = 1 - j
        rhs_s[u, other * HEAD_DIM_A:(other + 1) * HEAD_DIM_A, :] = jnp.zeros((HEAD_DIM_A, NQ_A), BF16)
        for g in range(GQA):
            h = GQA * j + g
            rhs_s[u, j * HEAD_DIM_A:(j + 1) * HEAD_DIM_A, g * b:(g + 1) * b] = (
                qt_ref[h * HEAD_DIM_A:(h + 1) * HEAD_DIM_A, t * b:(t + 1) * b])
        gb = i * nsub + t
        var = jnp.where(gb == 0, 0, jnp.where(gb == n_blocks - 1, 2, 1))
        return _dot(k_s[t * b:t * b + WIN_A, :], rhs_s[u]) + bias_ref[var, j]

    s_next = scores(0)
    for u, (t, j) in enumerate(units):
        s = s_next
        if u + 1 < len(units):
            s_next = scores(u + 1)
        sk = sink_ref[j]
        m = jnp.maximum(jnp.max(s, axis=0, keepdims=True), sk)
        p = jnp.exp2(s - m)
        r = _dot(vt_s[j * VT_ROWS:(j + 1) * VT_ROWS, t * b:t * b + WIN_A], p.astype(BF16))
        den = r[HEAD_DIM_A:HEAD_DIM_A + 1] + jnp.exp2(sk - m)
        o = (r[0:HEAD_DIM_A] * (1.0 / den)).astype(BF16)
        for g in range(GQA):
            h = GQA * j + g
            o_ref[h * HEAD_DIM_A:(h + 1) * HEAD_DIM_A, t * b:(t + 1) * b] = o[:, g * b:(g + 1) * b]


def _attn_a_consts(sink):
    b = ATT_BLOCK
    qi = np.arange(b)[None, :]
    s = np.arange(WIN_A)[:, None]
    dist = np.abs(b + qi - s).astype(np.float32)
    in_win = dist <= WINDOW
    slopes = jnp.exp2(-8.0 * jnp.arange(1, N_HEADS_A + 1, dtype=F32) / N_HEADS_A) * LOG2E
    alibi = -slopes.reshape(KV_HEADS_A, 1, GQA, 1) * jnp.asarray(dist)[None, :, None, :]
    variants = []
    for valid in (s >= b, s >= 0, s < 2 * b):
        mask = jnp.asarray(in_win & valid)[None, :, None, :]
        variants.append(jnp.where(mask, alibi, NEG_BIG).reshape(KV_HEADS_A, WIN_A, NQ_A))
    bias = jnp.stack(variants)
    sink_cols = jnp.repeat(sink.astype(F32).reshape(KV_HEADS_A, GQA) * LOG2E, b, axis=1)[:, None, :]
    return bias, sink_cols


def _with_ones_rows(vt, heads, head_dim):
    batch, _, seq = vt.shape
    vt = vt.reshape(batch, heads, head_dim, seq)
    vt = jnp.concatenate([vt, jnp.ones((batch, heads, 1, seq), BF16),
                          jnp.zeros((batch, heads, VT_ROWS - head_dim - 1, seq), BF16)], axis=2)
    return vt.reshape(batch, heads * VT_ROWS, seq)


def _attn_a(pa, sink, batch, seq):
    nq = seq // QB_A
    nblk = seq // ATT_BLOCK
    per = QB_A // ATT_BLOCK
    bias, sink_cols = _attn_a_consts(sink)
    p3 = pa.reshape(batch, seq, PA_W)
    qt = jnp.swapaxes(p3[:, :, 0:WIDTH_A], 1, 2)
    vt = _with_ones_rows(jnp.swapaxes(p3[:, :, WIDTH_A + LANES:PA_W], 1, 2), KV_HEADS_A, HEAD_DIM_A)
    kcol = WIDTH_A // LANES
    prev = lambda i: jnp.maximum(i * per - 1, 0)
    nxt = lambda i: jnp.minimum((i + 1) * per, nblk - 1)
    vrows = KV_HEADS_A * VT_ROWS
    ot = pl.pallas_call(
        functools.partial(_attn_a_kernel, n_blocks=nblk),
        grid=(batch, nq),
        in_specs=[
            pl.BlockSpec((None, WIDTH_A, QB_A), lambda b, i: (b, 0, i)),
            pl.BlockSpec((QB_A, LANES), lambda b, i: (b * nq + i, kcol)),
            pl.BlockSpec((ATT_BLOCK, LANES), lambda b, i: (b * nblk + prev(i), kcol)),
            pl.BlockSpec((ATT_BLOCK, LANES), lambda b, i: (b * nblk + nxt(i), kcol)),
            pl.BlockSpec((None, vrows, QB_A), lambda b, i: (b, 0, i)),
            pl.BlockSpec((None, vrows, ATT_BLOCK), lambda b, i: (b, 0, prev(i))),
            pl.BlockSpec((None, vrows, ATT_BLOCK), lambda b, i: (b, 0, nxt(i))),
            _const_spec((3, KV_HEADS_A, WIN_A, NQ_A)),
            _const_spec((KV_HEADS_A, 1, NQ_A)),
        ],
        out_specs=pl.BlockSpec((None, WIDTH_A, QB_A), lambda b, i: (b, 0, i)),
        out_shape=jax.ShapeDtypeStruct((batch, WIDTH_A, seq), BF16),
        scratch_shapes=[pltpu.VMEM((QB_A + 2 * ATT_BLOCK, LANES), BF16),
                        pltpu.VMEM((vrows, QB_A + 2 * ATT_BLOCK), BF16),
                        pltpu.VMEM((per * KV_HEADS_A, LANES, NQ_A), BF16)],
        compiler_params=_cparams(("parallel", "parallel")),
        name="attn_a",
    )(qt, pa, pa, pa, vt, vt, vt, bias, sink_cols)
    return jnp.swapaxes(ot, 1, 2).reshape(batch * seq, WIDTH_A)


RB_S = 512
HALO = 16


def _ssd_kernel(*refs, rev, n_steps):
    if rev:
        (cur_ref, prev_ref, next_ref, dt_ref, dtt_ref, cw_ref, cb_ref, dtb_c_ref, dtb_r_ref,
         alog_c_ref, alog_r_ref, ltri_ref, utri_ref, dskip_ref, g_ref, yf_ref, out_ref, st_ref) = refs
    else:
        (cur_ref, prev_ref, next_ref, dt_ref, dtt_ref, cw_ref, cb_ref, dtb_c_ref, dtb_r_ref,
         alog_c_ref, alog_r_ref, ltri_ref, utri_ref, out_ref, st_ref) = refs
    i = pl.program_id(1)
    ii = (n_steps - 1 - i) if rev else i
    d = 1 if rev else 0

    @pl.when(i == 0)
    def _():
        st_ref[...] = jnp.zeros_like(st_ref)

    xc = cur_ref[:, 0:CONV_DIM].astype(F32)
    prow = jnp.where(ii == 0, 0.0, prev_ref[...].astype(F32)[HALO - 1:HALO, :])
    nrow = jnp.where(ii == n_steps - 1, 0.0, next_ref[...].astype(F32)[0:1, :])
    rid = lax.broadcasted_iota(jnp.int32, (RB_S, CONV_DIM), 0)
    xm1 = jnp.where(rid == 0, prow, pltpu.roll(xc, 1, 0))
    xp1 = jnp.where(rid == RB_S - 1, nrow, pltpu.roll(xc, RB_S - 1, 0))
    u = xm1 * cw_ref[0:1, :] + xc * cw_ref[1:2, :] + xp1 * cw_ref[2:3, :] + cb_ref[...]
    u = u * _sigmoid(u)

    dt_c = _softplus(dt_ref[:, 0:2 * SSD_HEADS] + dtb_c_ref[...])
    dt_r = _softplus(dtt_ref[...] + dtb_r_ref[...])
    da_c = dt_c * (-jnp.exp(alog_c_ref[...]))
    da_r = dt_r * (-jnp.exp(alog_r_ref[...]))

    lane = lax.broadcasted_iota(jnp.int32, (CHUNK, LANES), 1)
    lo = lane < SSD_HEAD_DIM
    lo_row = lax.broadcasted_iota(jnp.int32, (1, LANES), 1) < SSD_HEAD_DIM
    li = lax.broadcasted_iota(jnp.int32, (CHUNK, CHUNK), 0)
    si = lax.broadcasted_iota(jnp.int32, (CHUNK, CHUNK), 1)
    tri = (si >= li) if rev else (si <= li)

    n_chunks = RB_S // CHUNK
    order = range(n_chunks - 1, -1, -1) if rev else range(n_chunks)
    for c in order:
        r0 = c * CHUNK
        dac = da_c[r0:r0 + CHUNK]
        dar = da_r[:, r0:r0 + CHUNK]
        cs_c = _dot_hi(ltri_ref[...], dac)
        cs_r = _dot_hi(dar, utri_ref[...])
        tot = cs_c[CHUNK - 1:CHUNK, :]
        if rev:
            e_c = cs_c - dac
            e_r = cs_r - dar
            w_b = jnp.exp(e_c)
            w_c = jnp.exp(tot - e_c)
        else:
            e_c = cs_c
            e_r = cs_r
            w_b = jnp.exp(tot - cs_c)
            w_c = jnp.exp(cs_c)
        dec = jnp.exp(tot)
        dtc = dt_c[r0:r0 + CHUNK]
        uc = u[r0:r0 + CHUNK]
        bm = uc[:, D_INNER:D_INNER + LANES]
        cm = uc[:, D_INNER + LANES:D_INNER + 2 * LANES]
        for pr in range(SSD_GROUPS):
            gmask = lo if pr == 0 else jnp.logical_not(lo)
            cg = jnp.where(gmask, cm, 0.0).astype(BF16)
            bg = jnp.where(gmask, bm, 0.0).astype(BF16)
            gmat = _dot_nt(cg, bg)
            xp = uc[:, pr * LANES:(pr + 1) * LANES]
            l0 = d * SSD_HEADS + 2 * pr
            l1 = l0 + 1
            pair = lambda a: jnp.where(lo, a[:, l0:l0 + 1], a[:, l1:l1 + 1])
            xdt = xp * pair(dtc)
            y = jnp.zeros((CHUNK, LANES), F32)
            for hh, dl in ((0, l0), (1, l1)):
                ecol = e_c[:, dl:dl + 1]
                erow = e_r[dl:dl + 1, :]
                diff = (erow - ecol) if rev else (ecol - erow)
                dmat = jnp.where(tri, jnp.exp(jnp.minimum(diff, 0.0)), 0.0)
                hmask = lo if hh == 0 else jnp.logical_not(lo)
                y = y + _dot((gmat * dmat).astype(BF16), jnp.where(hmask, xdt, 0.0).astype(BF16))
            st = st_ref[pr]
            y = y + pair(w_c) * _dot(cg, st.astype(BF16))
            s_new = _dot_tn(bg, (xdt * pair(w_b)).astype(BF16))
            dec_pair = jnp.where(lo_row, dec[:, l0:l0 + 1], dec[:, l1:l1 + 1])
            st_ref[pr] = st * dec_pair + s_new
            if rev:
                cols = slice(pr * LANES, (pr + 1) * LANES)
                y = y + yf_ref[r0:r0 + CHUNK, cols] + xp * dskip_ref[:, cols]
            out_ref[r0:r0 + CHUNK, pr * LANES:(pr + 1) * LANES] = y

    if rev:
        z = cur_ref[:, CONV_DIM:PB_W].astype(F32)
        yz = out_ref[...] * (z * _sigmoid(z))
        ms = jnp.mean(yz * yz, axis=-1, keepdims=True)
        out_ref[...] = yz * lax.rsqrt(ms + EPS) * g_ref[...]


def _ssd(pb, dt, dtt, params, yf, batch, seq):
    rev = yf is not None
    t = batch * seq
    ns = seq // RB_S
    per = RB_S // HALO
    nh = seq // HALO
    pos = (lambda i: ns - 1 - i) if rev else (lambda i: i)
    ltri = jnp.asarray(np.tril(np.ones((CHUNK, CHUNK), np.float32)))
    in_specs = [
        pl.BlockSpec((RB_S, PB_W), lambda b, i: (b * ns + pos(i), 0)),
        pl.BlockSpec((HALO, CONV_DIM), lambda b, i: (b * nh + jnp.maximum(pos(i) * per - 1, 0), 0)),
        pl.BlockSpec((HALO, CONV_DIM), lambda b, i: (b * nh + jnp.minimum((pos(i) + 1) * per, nh - 1), 0)),
        pl.BlockSpec((RB_S, DT_W), lambda b, i: (b * ns + pos(i), 0)),
        pl.BlockSpec((2 * SSD_HEADS, RB_S), lambda b, i: (0, b * ns + pos(i))),
        _const_spec((CONV_K, CONV_DIM)), _const_spec((1, CONV_DIM)),
        _const_spec((1, 2 * SSD_HEADS)), _const_spec((2 * SSD_HEADS, 1)),
        _const_spec((1, 2 * SSD_HEADS)), _const_spec((2 * SSD_HEADS, 1)),
        _const_spec((CHUNK, CHUNK)), _const_spec((CHUNK, CHUNK)),
    ]
    args = [pb, pb, pb, dt, dtt, params["conv_w"], params["conv_b"], params["dtb_c"], params["dtb_r"],
            params["alog_c"], params["alog_r"], ltri, ltri.T]
    if rev:
        in_specs += [_const_spec((1, D_INNER)), _const_spec((1, D_INNER)),
                     pl.BlockSpec((RB_S, D_INNER), lambda b, i: (b * ns + pos(i), 0))]
        args += [params["dskip"], params["norm_g"], yf]
    return pl.pallas_call(
        functools.partial(_ssd_kernel, rev=rev, n_steps=ns),
        grid=(batch, ns),
        in_specs=in_specs,
        out_specs=pl.BlockSpec((RB_S, D_INNER), lambda b, i: (b * ns + pos(i), 0)),
        out_shape=jax.ShapeDtypeStruct((t, D_INNER), F32),
        scratch_shapes=[pltpu.VMEM((SSD_GROUPS, LANES, LANES), F32)],
        compiler_params=_cparams(("parallel", "arbitrary")),
        name="ssd_bwd" if rev else "ssd_fwd",
    )(*args)


TQ_C = 512
TK_C = 512
N_MAPS = 4
KV_SUB = 4


def _attn_c_kernel(slope_ref, lam_ref, qt_ref, k_ref, vt_ref, bias_ref, g_ref, o_ref, qz_s, m_s, acc_s):
    hp = pl.program_id(1)
    qi = pl.program_id(2)
    ki = pl.program_id(3)

    @pl.when(ki == 0)
    def _():
        m_s[...] = jnp.full_like(m_s, NEG_BIG)
        acc_s[...] = jnp.zeros_like(acc_s)
        qt = qt_ref[...]
        seg = lax.broadcasted_iota(jnp.int32, (LANES, TQ_C), 0) // DIFF_QK_DIM
        for idx in range(N_MAPS):
            qz_s[idx] = jnp.where(seg == idx, qt, jnp.zeros_like(qt))

    units = [(s, hh, c) for s in range(KV_SUB) for hh in range(2) for c in range(2)]

    def scores(u):
        s, hh, c = units[u]
        kt = ki * KV_SUB + s
        var = jnp.where(kt < qi, 0, jnp.where(kt == qi, 1, 2))
        return _dot(k_ref[s * TK_C:(s + 1) * TK_C, :], qz_s[2 * hh + c]) + bias_ref[var, hh]

    t_next = scores(0)
    for u, (s, hh, c) in enumerate(units):
        t = t_next
        if u + 1 < len(units):
            t_next = scores(u + 1)
        idx = 2 * hh + c
        kt = ki * KV_SUB + s
        c_tile = -jnp.abs(qi * TQ_C - kt * TK_C).astype(F32) * slope_ref[2 * hp + hh]
        m_old = m_s[idx]
        m_new = jnp.maximum(m_old, jnp.max(t, axis=0, keepdims=True) + c_tile)
        alpha = jnp.exp2(m_old - m_new)
        p = jnp.exp2(t - (m_new - c_tile))
        vt = vt_ref[hh * VT_ROWS:(hh + 1) * VT_ROWS, s * TK_C:(s + 1) * TK_C]
        acc_s[idx] = alpha * acc_s[idx] + _dot(vt, p.astype(BF16))
        m_s[idx] = m_new

    @pl.when(ki == pl.num_programs(3) - 1)
    def _():
        _attn_c_finalize(lam_ref, g_ref, o_ref, acc_s)


def _attn_c_finalize(lam_ref, g_ref, o_ref, acc_s):
    lam = lam_ref[0]
    for hh in range(2):
        a0 = acc_s[2 * hh]
        a1 = acc_s[2 * hh + 1]
        o = (a0[0:DIFF_V_DIM] * (1.0 / a0[DIFF_V_DIM:DIFF_V_DIM + 1])
             - lam * (a1[0:DIFF_V_DIM] * (1.0 / a1[DIFF_V_DIM:DIFF_V_DIM + 1])))
        ms = jnp.mean(o * o, axis=0, keepdims=True)
        rows = slice(hh * DIFF_V_DIM, (hh + 1) * DIFF_V_DIM)
        o_ref[rows, :] = (o * lax.rsqrt(ms + EPS) * g_ref[rows, :]).astype(BF16)


AUG_ROWS = 16
AUG_LANES = 64
BOUND_SLACK = 1.0 + 1e-3
BOUND_LIMIT = 100.0


def _split3(x):
    hi = x.astype(BF16).astype(F32)
    r = x - hi
    mid = r.astype(BF16).astype(F32)
    return hi, mid, (r - mid).astype(BF16).astype(F32)


def _attn_c_bounded_kernel(slope_ref, lam_ref, kmax_ref, qt_ref, k_ref, vt_ref, dtab_ref, g_ref, o_ref,
                           rhs_s, mref_s, acc_s):
    b = pl.program_id(0)
    hp = pl.program_id(1)
    qi = pl.program_id(2)
    ki = pl.program_id(3)
    units = [(s, hh, c) for s in range(KV_SUB) for hh in range(2) for c in range(2)]

    @pl.when(ki == 0)
    def _():
        acc_s[...] = jnp.zeros_like(acc_s)
        rhs_s[...] = jnp.zeros_like(rhs_s)
        for u, (s, hh, c) in enumerate(units):
            idx = 2 * hh + c
            qseg = qt_ref[idx * DIFF_QK_DIM:(idx + 1) * DIFF_QK_DIM, :]
            rhs_s[u, c * DIFF_QK_DIM:(c + 1) * DIFF_QK_DIM, :] = qseg
            if s == 0:
                qf = qseg.astype(F32)
                kmax = kmax_ref[(b * 2 + hp) * N_MAPS + idx] * BOUND_SLACK
                mref_s[idx] = jnp.sqrt(jnp.sum(qf * qf, axis=0, keepdims=True)) * kmax

    qloc = lax.broadcasted_iota(jnp.int32, (1, TQ_C), 1).astype(F32)
    row = lax.broadcasted_iota(jnp.int32, (AUG_ROWS, TQ_C), 0)
    grp = row // 3
    part = row - 3 * grp

    def scores(u):
        s, hh, c = units[u]
        kt = ki * KV_SUB + s
        h = 2 * hp + hh
        sgn = jnp.where(kt < qi, 1.0, jnp.where(kt == qi, 0.0, -1.0)).astype(F32)
        slope = slope_ref[4 * h + 3]
        tile_dist = jnp.abs(qi * TQ_C - kt * TK_C).astype(F32)
        shift = mref_s[2 * hh + c] + slope * tile_dist + (sgn * slope) * qloc
        hi, mid, lo = _split3(shift)
        shift_part = jnp.where(part == 0, hi, jnp.where(part == 1, mid, lo))
        slope_part = sgn * jnp.where(part == 0, slope_ref[4 * h], jnp.where(part == 1, slope_ref[4 * h + 1],
                                                                           slope_ref[4 * h + 2]))
        aug = jnp.where(grp == 0, -shift_part, jnp.where(grp <= 2, slope_part, 0.0))
        rhs_s[u, 2 * DIFF_QK_DIM:2 * DIFF_QK_DIM + AUG_ROWS, :] = aug.astype(BF16)
        on_diag = (kt == qi).astype(jnp.int32)
        return _dot(k_ref[s * TK_C:(s + 1) * TK_C, hh * LANES:(hh + 1) * LANES], rhs_s[u]) + dtab_ref[on_diag, hh]

    t_next = scores(0)
    for u, (s, hh, c) in enumerate(units):
        t = t_next
        if u + 1 < len(units):
            t_next = scores(u + 1)
        idx = 2 * hh + c
        vt = vt_ref[hh * VT_ROWS:(hh + 1) * VT_ROWS, s * TK_C:(s + 1) * TK_C]
        acc_s[idx] = acc_s[idx] + _dot(vt, jnp.exp2(t).astype(BF16))

    @pl.when(ki == pl.num_programs(3) - 1)
    def _():
        _attn_c_finalize(lam_ref, g_ref, o_ref, acc_s)


def _attn_c_bias(slopes):
    rel = (np.arange(TQ_C)[None, :] - np.arange(TK_C)[:, None]).astype(np.float32)
    tables = jnp.asarray(np.stack([-rel, -np.abs(rel), rel]))
    s = slopes.reshape(DIFF_HEADS // 2, 1, 2, 1, 1)
    return s * tables[None, :, None]


def _attn_c_call(body, prefetch, tensors, k_width, k_col0, table_variants, scratch, batch, seq, name):
    nq = seq // TQ_C
    kv_step = TK_C * KV_SUB
    nk = seq // kv_step
    n_pre = len(prefetch)
    grid_spec = pltpu.PrefetchScalarGridSpec(
        num_scalar_prefetch=n_pre,
        grid=(batch, DIFF_HEADS // 2, nq, nk),
        in_specs=[
            pl.BlockSpec((None, LANES, TQ_C), lambda b, h, i, j, *_: (b, h, i)),
            pl.BlockSpec((kv_step, k_width), lambda b, h, i, j, *_: (b * nk + j, k_col0 + h)),
            pl.BlockSpec((None, 2 * VT_ROWS, kv_step), lambda b, h, i, j, *_: (b, h, j)),
            pl.BlockSpec((None, table_variants, 2, TK_C, TQ_C), lambda b, h, i, j, *_: (h, 0, 0, 0, 0)),
            _const_spec((LANES, TQ_C)),
        ],
        out_specs=pl.BlockSpec((None, LANES, TQ_C), lambda b, h, i, j, *_: (b, h, i)),
        scratch_shapes=scratch + [pltpu.VMEM((N_MAPS, 1, TQ_C), F32), pltpu.VMEM((N_MAPS, VT_ROWS, TQ_C), F32)],
    )
    return pl.pallas_call(
        body,
        grid_spec=grid_spec,
        out_shape=jax.ShapeDtypeStruct((batch, WIDTH_C, seq), BF16),
        compiler_params=_cparams(("parallel", "parallel", "parallel", "arbitrary")),
        name=name,
    )(*prefetch, *tensors)


def _seg_norm_max(x, batch, seq):
    xf = x.astype(F32).reshape(batch, seq, 2 * DIFF_HEADS, DIFF_QK_DIM)
    return jnp.sqrt(jnp.max(jnp.sum(xf * xf, axis=-1), axis=1))


def _attn_c(pc, slopes, lam, gain, batch, seq):
    kv_step = TK_C * KV_SUB
    assert seq % kv_step == 0 and seq % TQ_C == 0
    p3 = pc.reshape(batch, seq, PC_W)
    q, k = p3[:, :, 0:256], p3[:, :, 256:512]
    qt = jnp.swapaxes(q, 1, 2)
    vt = _with_ones_rows(jnp.swapaxes(p3[:, :, 512:768], 1, 2), DIFF_HEADS, DIFF_V_DIM)
    gain_b = jnp.broadcast_to(gain.reshape(LANES, 1), (LANES, TQ_C))
    tables = _attn_c_bias(slopes)

    def online():
        scratch = [pltpu.VMEM((N_MAPS, LANES, TQ_C), BF16)]
        return _attn_c_call(_attn_c_kernel, (slopes, lam), (qt, pc, vt, tables, gain_b), LANES, 2, 3, scratch,
                            batch, seq, "attn_c_online")

    def bounded():
        pos = np.arange(seq) % TK_C
        cols = np.zeros((seq, AUG_LANES), np.float32)
        cols[:, 0:3] = 1.0
        cols[:, 3:6] = (pos // 256 * 256)[:, None]
        cols[:, 6:9] = (pos % 256)[:, None]
        aug = jnp.broadcast_to(jnp.asarray(cols, dtype=BF16)[None, :, None, :], (batch, seq, DIFF_HEADS, AUG_LANES))
        k_aug = jnp.concatenate([k.reshape(batch, seq, DIFF_HEADS, 2 * DIFF_QK_DIM), aug], axis=-1)
        k_aug = k_aug.reshape(batch * seq, DIFF_HEADS * LANES)
        slope4 = jnp.stack(_split3(slopes) + (slopes,), axis=1).reshape(-1)
        dtab = jnp.stack([jnp.zeros_like(tables[:, 1]), tables[:, 1]], axis=1)
        scratch = [pltpu.VMEM((N_MAPS * KV_SUB, LANES, TQ_C), BF16)]
        return _attn_c_call(_attn_c_bounded_kernel, (slope4, lam, kmax.reshape(-1)),
                            (qt, k_aug, vt, dtab, gain_b), 2 * LANES, 0, 2, scratch, batch, seq, "attn_c_bounded")

    kmax = _seg_norm_max(k, batch, seq)
    qmax = _seg_norm_max(q, batch, seq)
    safe = jnp.max(2.0 * BOUND_SLACK * qmax * kmax) < BOUND_LIMIT
    ot = lax.cond(safe, bounded, online)
    return jnp.swapaxes(ot, 1, 2).reshape(batch * seq, WIDTH_C)


def _out_proj(oa_ref, ob_ref, oc_ref, x_ref, wo_ref):
    o = jnp.concatenate([oa_ref[...], ob_ref[...].astype(BF16), oc_ref[...]], axis=-1)
    return x_ref[...] + _dot(o, wo_ref[...])


def _rms(x, g_ref):
    ms = jnp.mean(x * x, axis=-1, keepdims=True)
    return x * lax.rsqrt(ms + EPS) * g_ref[...]


def _swiglu_acc(h, wg_ref, wu_ref, wd_ref, acc, d_ff):
    step = d_ff // FF_CHUNKS
    for c0 in range(0, d_ff, step):
        gate = _dot(h, wg_ref[:, c0:c0 + step])
        up = _dot(h, wu_ref[:, c0:c0 + step])
        act = (gate * _sigmoid(gate) * up).astype(BF16)
        acc = acc + _dot(act, wd_ref[c0:c0 + step, :])
    return acc


def _out_ffn_kernel(oa_ref, ob_ref, oc_ref, x_ref, wo_ref, g_ref, wg_ref, wu_ref, wd_ref, out_ref):
    x1 = _out_proj(oa_ref, ob_ref, oc_ref, x_ref, wo_ref)
    h = _rms(x1, g_ref).astype(BF16)
    out_ref[...] = _swiglu_acc(h, wg_ref, wu_ref, wd_ref, x1, D_FF)


def _mixer_out_specs():
    row = lambda w: pl.BlockSpec((TM, w), lambda i: (i, 0))
    return [row(WIDTH_A), row(D_INNER), row(WIDTH_C), row(D_MODEL), _resident_spec((D_MODEL, D_MODEL)),
            _const_spec((1, D_MODEL))]


def _out_ffn(oa, ob, oc, x, w_out, g, wg, wu, wd):
    t = x.shape[0]
    return pl.pallas_call(
        _out_ffn_kernel,
        grid=(t // TM,),
        in_specs=_mixer_out_specs() + [_resident_spec((D_MODEL, D_FF)), _resident_spec((D_MODEL, D_FF)),
                                       _resident_spec((D_FF, D_MODEL))],
        out_specs=pl.BlockSpec((TM, D_MODEL), lambda i: (i, 0)),
        out_shape=jax.ShapeDtypeStruct((t, D_MODEL), F32),
        compiler_params=_cparams(("parallel",)),
        name="out_ffn",
    )(oa, ob, oc, x, w_out, g, wg, wu, wd)


def _out_router_kernel(oa_ref, ob_ref, oc_ref, x_ref, wo_ref, g_ref, wr_hi_ref, wr_lo_ref,
                       x1_ref, h_ref, sel_ref, gate_ref):
    x1 = _out_proj(oa_ref, ob_ref, oc_ref, x_ref, wo_ref)
    x1_ref[...] = x1
    h = _rms(x1, g_ref)
    h_hi = h.astype(BF16)
    h_ref[...] = h_hi
    h_lo = (h - h_hi.astype(F32)).astype(BF16)
    logits = _dot(h_hi, wr_hi_ref[...]) + (_dot(h_lo, wr_hi_ref[...]) + _dot(h_hi, wr_lo_ref[...]))
    lane_i = lax.broadcasted_iota(jnp.int32, (TM, LANES), 1)
    lane = lane_i.astype(F32)
    logits = jnp.where(lane_i < N_EXPERTS, logits, -jnp.inf)
    v1 = jnp.max(logits, axis=-1, keepdims=True)
    i1 = jnp.min(jnp.where(logits == v1, lane, float(LANES)), axis=-1, keepdims=True)
    rest = jnp.where(lane == i1, -jnp.inf, logits)
    v2 = jnp.max(rest, axis=-1, keepdims=True)
    i2 = jnp.min(jnp.where(rest == v2, lane, float(LANES)), axis=-1, keepdims=True)
    e = jnp.exp(v2 - v1)
    g1 = 1.0 / (1.0 + e)
    sel_ref[...] = jnp.where(lane_i == 0, i1, jnp.where(lane_i == 1, i2, 0.0)).astype(jnp.int32)
    gate_ref[...] = jnp.where(lane_i == 0, g1, jnp.where(lane_i == 1, e * g1, 0.0))


def _out_router(oa, ob, oc, x, w_out, g, wr_hi, wr_lo):
    t = x.shape[0]
    row = lambda w: pl.BlockSpec((TM, w), lambda i: (i, 0))
    return pl.pallas_call(
        _out_router_kernel,
        grid=(t // TM,),
        in_specs=_mixer_out_specs() + [_const_spec((D_MODEL, LANES)), _const_spec((D_MODEL, LANES))],
        out_specs=[row(D_MODEL), row(D_MODEL), row(LANES), row(LANES)],
        out_shape=[jax.ShapeDtypeStruct((t, D_MODEL), F32), jax.ShapeDtypeStruct((t, D_MODEL), BF16),
                   jax.ShapeDtypeStruct((t, LANES), jnp.int32), jax.ShapeDtypeStruct((t, LANES), F32)],
        compiler_params=_cparams(("parallel",)),
        name="out_router",
    )(oa, ob, oc, x, w_out, g, wr_hi, wr_lo)


def _moe_ffn_kernel(blk_exp_ref, n_used_ref, x_ref, wg_ref, wu_ref, wd_ref, out_ref):
    i = pl.program_id(0)

    @pl.when(i < n_used_ref[0])
    def _():
        acc = jnp.zeros((MOE_BLK, D_MODEL), F32)
        out_ref[...] = _swiglu_acc(x_ref[...], wg_ref.at[0], wu_ref.at[0], wd_ref.at[0], acc, D_FF_EXPERT)

    @pl.when(i >= n_used_ref[0])
    def _():
        out_ref[...] = jnp.zeros_like(out_ref)


def _moe_ffn(xg, blk_exp, n_used, wg, wu, wd):
    rows = xg.shape[0]
    wspec = lambda shape: pl.BlockSpec((1,) + shape, lambda i, be, nu: (be[i], 0, 0),
                                       pipeline_mode=pl.Buffered(1))
    grid_spec = pltpu.PrefetchScalarGridSpec(
        num_scalar_prefetch=2,
        grid=(rows // MOE_BLK,),
        in_specs=[pl.BlockSpec((MOE_BLK, D_MODEL), lambda i, be, nu: (i, 0)),
                  wspec((D_MODEL, D_FF_EXPERT)), wspec((D_MODEL, D_FF_EXPERT)),
                  wspec((D_FF_EXPERT, D_MODEL))],
        out_specs=pl.BlockSpec((MOE_BLK, D_MODEL), lambda i, be, nu: (i, 0)),
    )
    return pl.pallas_call(
        _moe_ffn_kernel,
        grid_spec=grid_spec,
        out_shape=jax.ShapeDtypeStruct((rows, D_MODEL), F32),
        compiler_params=_cparams(("arbitrary",)),
        name="moe_ffn",
    )(blk_exp, n_used, xg, wg, wu, wd)


def _moe(x1, h, sel, gates, wg, wu, wd):
    t = x1.shape[0]
    i1, i2 = sel[:, 0], sel[:, 1]
    g1, g2 = gates[:, 0:1], gates[:, 1:2]
    experts = jnp.arange(N_EXPERTS, dtype=jnp.int32)
    onehot = ((i1[:, None] == experts) | (i2[:, None] == experts)).astype(jnp.int32)
    rank = jnp.cumsum(onehot, axis=0) - onehot
    counts = jnp.sum(onehot, axis=0)
    padded = ((counts + MOE_BLK - 1) // MOE_BLK) * MOE_BLK
    pad_end = jnp.cumsum(padded)
    pad_start = pad_end - padded
    dest = pad_start[None, :] + rank
    d1 = jnp.take_along_axis(dest, i1[:, None], axis=1)[:, 0]
    d2 = jnp.take_along_axis(dest, i2[:, None], axis=1)[:, 0]
    n_blocks = (2 * t) // MOE_BLK + N_EXPERTS
    rows = n_blocks * MOE_BLK
    tok = jnp.arange(t, dtype=jnp.int32)
    row_tok = jnp.full((rows,), t, jnp.int32).at[d1].set(tok).at[d2].set(tok)
    blk_start = jnp.arange(n_blocks, dtype=jnp.int32) * MOE_BLK
    blk_exp = jnp.minimum(jnp.sum((pad_end[None, :] <= blk_start[:, None]).astype(jnp.int32), axis=1),
                          N_EXPERTS - 1)
    n_used = (pad_end[-1:] // MOE_BLK).astype(jnp.int32)
    h_pad = jnp.concatenate([h, jnp.zeros((1, D_MODEL), h.dtype)], axis=0)
    yb = _moe_ffn(h_pad[row_tok], blk_exp, n_used, wg, wu, wd)
    return x1 + g1 * yb[d1] + g2 * yb[d2]


def _split_cols(w):
    outs, start = [], 0
    for n in SPLIT_SIZES:
        outs.append(w[:, start:start + n])
        start += n
    return outs


def _layer_params(l, p):
    qa, ka, va, zb, xbc, dtb, qc, kc, vc = _split_cols(p["w_in"][l])
    w_cat = jnp.concatenate(
        [qa, ka, va, xbc, zb, qc, kc, vc, dtb, jnp.zeros((D_MODEL, DT_W - 2 * SSD_HEADS), F32)],
        axis=1).astype(BF16)
    ga = jnp.concatenate([jnp.tile(p["qnorm_a"][l] * (HEAD_DIM_A ** -0.5 * LOG2E), N_HEADS_A),
                          jnp.tile(p["knorm_a"][l], KV_HEADS_A)])[None, :]
    gc = jnp.concatenate([jnp.tile(p["qnorm_c"][l] * (DIFF_QK_DIM ** -0.5 * LOG2E), 2 * DIFF_HEADS),
                          jnp.tile(p["knorm_c"][l], 2 * DIFF_HEADS)])[None, :]
    lam_init = 0.8 - 0.6 * math.exp(-0.3 * l)
    lc = p["lam_c"][l].astype(F32)
    lam = jnp.exp(jnp.sum(lc[0] * lc[1])) - jnp.exp(jnp.sum(lc[2] * lc[3])) + lam_init
    out = {
        "norm_mix_g": p["norm_mix_g"][l][None, :], "w_cat": w_cat, "ga": ga, "gc": gc,
        "sink": p["sink_a"][l],
        "ssd": {
            "conv_w": p["conv_w"][l], "conv_b": p["conv_b"][l][None, :],
            "dtb_c": p["dt_bias"][l].reshape(1, -1), "dtb_r": p["dt_bias"][l].reshape(-1, 1),
            "alog_c": p["a_log"][l].reshape(1, -1), "alog_r": p["a_log"][l].reshape(-1, 1),
            "dskip": jnp.repeat(p["d_skip"][l], SSD_HEAD_DIM)[None, :],
            "norm_g": p["ssd_norm_g"][l][None, :],
        },
        "slopes_c": jnp.exp2(-8.0 * jnp.arange(1, DIFF_HEADS + 1, dtype=F32) / DIFF_HEADS) * LOG2E,
        "lam": jnp.reshape(lam, (1,)).astype(F32),
        "gain_c": (jnp.tile(p["diff_norm_g"][l], 2) * (1.0 - lam_init))[None, :],
        "w_out": p["w_out"][l].astype(BF16),
        "norm_ffn_g": p["norm_ffn_g"][l][None, :],
    }
    i = l // 2
    if l % 2 == 0:
        out["ffn"] = (p["ffn_w_gate"][i].astype(BF16), p["ffn_w_up"][i].astype(BF16),
                      p["ffn_w_down"][i].astype(BF16))
    else:
        wr = jnp.pad(p["router_w"][i], ((0, 0), (0, LANES - N_EXPERTS)))
        wr_hi = wr.astype(BF16)
        out["router"] = (wr_hi, (wr - wr_hi.astype(F32)).astype(BF16))
        out["moe"] = (p["moe_w_gate"][i].astype(BF16), p["moe_w_up"][i].astype(BF16),
                      p["moe_w_down"][i].astype(BF16))
    return out


def _trunk(x3, layers):
    batch, seq, _ = x3.shape
    x = x3.reshape(batch * seq, D_MODEL)
    for l, lp in enumerate(layers):
        pa, pb, pc, dt = _norm_proj(x, lp["norm_mix_g"], lp["w_cat"], lp["ga"], lp["gc"])
        oa = _attn_a(pa, lp["sink"], batch, seq)
        dtt = dt[:, :2 * SSD_HEADS].T
        yf = _ssd(pb, dt, dtt, lp["ssd"], None, batch, seq)
        ob = _ssd(pb, dt, dtt, lp["ssd"], yf, batch, seq)
        oc = _attn_c(pc, lp["slopes_c"], lp["lam"], lp["gain_c"], batch, seq)
        if l % 2 == 0:
            x = _out_ffn(oa, ob, oc, x, lp["w_out"], lp["norm_ffn_g"], *lp["ffn"])
        else:
            x1, h, sel, gates = _out_router(oa, ob, oc, x, lp["w_out"], lp["norm_ffn_g"], *lp["router"])
            x = _moe(x1, h, sel, gates, *lp["moe"])
    return x.reshape(batch, seq, D_MODEL)


def kernel(x_prompt, x_sample, norm_mix_g, w_in, qnorm_a, knorm_a, sink_a, conv_w, conv_b, dt_bias, a_log,
           d_skip, ssd_norm_g, qnorm_c, knorm_c, lam_c, diff_norm_g, w_out, norm_ffn_g, ffn_w_gate, ffn_w_up,
           ffn_w_down, router_w, moe_w_gate, moe_w_up, moe_w_down):
    p = dict(norm_mix_g=norm_mix_g, w_in=w_in, qnorm_a=qnorm_a, knorm_a=knorm_a, sink_a=sink_a,
             conv_w=conv_w, conv_b=conv_b, dt_bias=dt_bias, a_log=a_log, d_skip=d_skip,
             ssd_norm_g=ssd_norm_g, qnorm_c=qnorm_c, knorm_c=knorm_c, lam_c=lam_c, diff_norm_g=diff_norm_g,
             w_out=w_out, norm_ffn_g=norm_ffn_g, ffn_w_gate=ffn_w_gate, ffn_w_up=ffn_w_up,
             ffn_w_down=ffn_w_down, router_w=router_w, moe_w_gate=moe_w_gate, moe_w_up=moe_w_up,
             moe_w_down=moe_w_down)
    layers = [_layer_params(l, p) for l in range(DEPTH)]
    return _trunk(x_prompt, layers), _trunk(x_sample, layers)
```

```python
import functools
import math

import numpy as np
import jax
import jax.numpy as jnp
from jax import lax
from jax.experimental import pallas as pl
from jax.experimental.pallas import tpu as pltpu

F32 = jnp.float32
BF16 = jnp.bfloat16

D_MODEL = 1024
DEPTH = 2
N_HEADS_A = 8
KV_HEADS_A = 2
HEAD_DIM_A = 64
WINDOW = 128
ATT_BLOCK = 128
SSD_HEADS = 4
SSD_HEAD_DIM = 64
D_INNER = SSD_HEADS * SSD_HEAD_DIM
SSD_GROUPS = 2
D_STATE = 64
CONV_K = 3
CHUNK = 128
DIFF_HEADS = 4
DIFF_QK_DIM = 32
DIFF_V_DIM = 64
WIDTH_A = N_HEADS_A * HEAD_DIM_A
WIDTH_C = DIFF_HEADS * DIFF_V_DIM
CONV_DIM = D_INNER + 2 * SSD_GROUPS * D_STATE
SPLIT_SIZES = (WIDTH_A, KV_HEADS_A * HEAD_DIM_A, KV_HEADS_A * HEAD_DIM_A,
               D_INNER, CONV_DIM, 2 * SSD_HEADS,
               DIFF_HEADS * 2 * DIFF_QK_DIM, DIFF_HEADS * 2 * DIFF_QK_DIM, WIDTH_C)
D_FF = 2816
N_EXPERTS = 8
D_FF_EXPERT = 3584
EPS = 1e-6

LOG2E = 1.4426950408889634
NEG_BIG = -1e30

LANES = 128
VMEM_LIMIT = 56 * 1024 * 1024

PA_W = 768
PA_NORM_W = 640
VT_ROWS = 80
PB_W = 768
PC_W = 768
DT_W = 128
PROJ_W = PA_W + PB_W + PC_W + DT_W

TM = 512
MOE_BLK = 512
FF_CHUNKS = 2


def _cparams(sem, vmem=VMEM_LIMIT):
    return pltpu.CompilerParams(dimension_semantics=sem, vmem_limit_bytes=vmem)


def _const_spec(shape):
    nd = len(shape)
    return pl.BlockSpec(shape, lambda *_: (0,) * nd)


def _resident_spec(shape):
    nd = len(shape)
    return pl.BlockSpec(shape, lambda *_: (0,) * nd, pipeline_mode=pl.Buffered(1))


def _block_diag(n, seg):
    idx = np.arange(n) // seg
    return jnp.asarray((idx[:, None] == idx[None, :]).astype(np.float32), dtype=BF16)


def _sigmoid(x):
    return 1.0 / (1.0 + jnp.exp(-x))


def _softplus(x):
    return jnp.maximum(x, 0.0) + jnp.log(1.0 + jnp.exp(-jnp.abs(x)))


def _dot(a, b):
    return jnp.dot(a, b, preferred_element_type=F32)


def _dot_nt(a, b):
    return lax.dot_general(a, b, (((1,), (1,)), ((), ())), preferred_element_type=F32)


def _dot_tn(a, b):
    return lax.dot_general(a, b, (((0,), (0,)), ((), ())), preferred_element_type=F32)


def _dot_hi(a, b):
    return jnp.dot(a, b, preferred_element_type=F32, precision=lax.Precision.HIGHEST)


def _seg_norm(blk, seg, bd_ref, gain):
    w = blk.shape[1]
    ss = _dot((blk * blk).astype(BF16), bd_ref[0:w, 0:w]) * (1.0 / seg)
    return blk * lax.rsqrt(ss + EPS) * gain


def _store_vt(vt_ref, v, heads, head_dim, pad_ref):
    vt = v.T
    for h in range(heads):
        r0 = h * VT_ROWS
        vt_ref[r0:r0 + head_dim, :] = vt[h * head_dim:(h + 1) * head_dim].astype(BF16)
        vt_ref[r0 + head_dim:r0 + VT_ROWS, :] = pad_ref[...]


def _norm_proj_kernel(x_ref, g_ref, w_ref, bd64_ref, bd32_ref, ga_ref, gc_ref, pad_ref, kaug_ref,
                      qat_ref, ka_ref, vat_ref, pb_ref, qct_ref, kc_ref, vct_ref, dt_ref, dtt_ref):
    x = x_ref[...]
    ms = jnp.mean(x * x, axis=-1, keepdims=True)
    h = (x * lax.rsqrt(ms + EPS) * g_ref[...]).astype(BF16)
    ya = _dot(h, w_ref[:, 0:PA_W])
    for c0 in range(0, WIDTH_A, 256):
        qn = _seg_norm(ya[:, c0:c0 + 256], HEAD_DIM_A, bd64_ref, ga_ref[:, c0:c0 + 256])
        qat_ref[c0:c0 + 256, :] = qn.T.astype(BF16)
    ka_ref[...] = _seg_norm(ya[:, WIDTH_A:PA_NORM_W], HEAD_DIM_A, bd64_ref, ga_ref[:, WIDTH_A:PA_NORM_W]).astype(BF16)
    _store_vt(vat_ref, ya[:, PA_NORM_W:PA_W], KV_HEADS_A, HEAD_DIM_A, pad_ref)
    pb_ref[...] = _dot(h, w_ref[:, PA_W:PA_W + PB_W]).astype(BF16)
    yc = _dot(h, w_ref[:, PA_W + PB_W:PA_W + PB_W + PC_W])
    qct_ref[...] = _seg_norm(yc[:, 0:256], DIFF_QK_DIM, bd32_ref, gc_ref[:, 0:256]).T.astype(BF16)
    kc = _seg_norm(yc[:, 256:512], DIFF_QK_DIM, bd32_ref, gc_ref[:, 256:512]).astype(BF16)
    aug = kaug_ref[...]
    pieces = []
    for hd in range(DIFF_HEADS):
        pieces += [kc[:, hd * 2 * DIFF_QK_DIM:(hd + 1) * 2 * DIFF_QK_DIM], aug]
    kc_ref[...] = jnp.concatenate(pieces, axis=1)
    _store_vt(vct_ref, yc[:, 512:PC_W], DIFF_HEADS, DIFF_V_DIM, pad_ref)
    yd = _dot(h, w_ref[:, PA_W + PB_W + PC_W:PROJ_W])
    dt_ref[...] = yd
    dtt_ref[...] = yd.T[0:2 * SSD_HEADS, :]


def _norm_proj(x, g, w_cat, ga, gc, batch, seq):
    t = batch * seq
    nps = seq // TM
    row = lambda w: pl.BlockSpec((TM, w), lambda i: (i, 0))
    tr = lambda r: pl.BlockSpec((None, r, TM), lambda i: (i // nps, 0, i % nps))
    pad = jnp.zeros((VT_ROWS - HEAD_DIM_A, TM), BF16).at[0].set(1.0)
    pos = np.arange(TM) % TK_C
    cols = np.zeros((TM, AUG_LANES), np.float32)
    cols[:, 0:3] = 1.0
    cols[:, 3:6] = (pos // 256 * 256)[:, None]
    cols[:, 6:9] = (pos % 256)[:, None]
    return pl.pallas_call(
        _norm_proj_kernel,
        grid=(t // TM,),
        in_specs=[row(D_MODEL), _const_spec((1, D_MODEL)), _resident_spec((D_MODEL, PROJ_W)),
                  _const_spec((256, 256)), _const_spec((256, 256)),
                  _const_spec((1, PA_NORM_W)), _const_spec((1, 512)),
                  _const_spec((VT_ROWS - HEAD_DIM_A, TM)), _const_spec((TM, AUG_LANES))],
        out_specs=[tr(WIDTH_A), row(LANES), tr(KV_HEADS_A * VT_ROWS), row(PB_W),
                   tr(256), row(DIFF_HEADS * LANES), tr(DIFF_HEADS * VT_ROWS), row(DT_W),
                   pl.BlockSpec((2 * SSD_HEADS, TM), lambda i: (0, i))],
        out_shape=[jax.ShapeDtypeStruct((batch, WIDTH_A, seq), BF16), jax.ShapeDtypeStruct((t, LANES), BF16),
                   jax.ShapeDtypeStruct((batch, KV_HEADS_A * VT_ROWS, seq), BF16),
                   jax.ShapeDtypeStruct((t, PB_W), BF16),
                   jax.ShapeDtypeStruct((batch, 256, seq), BF16),
                   jax.ShapeDtypeStruct((t, DIFF_HEADS * LANES), BF16),
                   jax.ShapeDtypeStruct((batch, DIFF_HEADS * VT_ROWS, seq), BF16),
                   jax.ShapeDtypeStruct((t, DT_W), F32), jax.ShapeDtypeStruct((2 * SSD_HEADS, t), F32)],
        compiler_params=_cparams(("parallel",)),
        name="norm_proj",
    )(x, g, w_cat, _block_diag(256, HEAD_DIM_A), _block_diag(256, DIFF_QK_DIM), ga, gc, pad,
      jnp.asarray(cols, dtype=BF16))


QB_A = 512


GQA = N_HEADS_A // KV_HEADS_A
WIN_A = 3 * ATT_BLOCK
NQ_A = GQA * ATT_BLOCK


def _attn_a_kernel(qt_ref, k_ref, kp_ref, kn_ref, vt_ref, vtp_ref, vtn_ref, bias_ref, sink_ref, o_ref,
                   k_s, vt_s, rhs_s, *, n_blocks):
    i = pl.program_id(1)
    nsub = QB_A // ATT_BLOCK
    b = ATT_BLOCK
    k_s[0:b] = kp_ref[...]
    k_s[b:b + QB_A] = k_ref[...]
    k_s[b + QB_A:] = kn_ref[...]
    vt_s[:, 0:b] = vtp_ref[...]
    vt_s[:, b:b + QB_A] = vt_ref[...]
    vt_s[:, b + QB_A:] = vtn_ref[...]
    units = [(t, j) for t in range(nsub) for j in range(KV_HEADS_A)]

    def scores(u):
        t, j = units[u]
        other = 1 - j
        rhs_s[u, other * HEAD_DIM_A:(other + 1) * HEAD_DIM_A, :] = jnp.zeros((HEAD_DIM_A, NQ_A), BF16)
        for g in range(GQA):
            h = GQA * j + g
            rhs_s[u, j * HEAD_DIM_A:(j + 1) * HEAD_DIM_A, g * b:(g + 1) * b] = (
                qt_ref[h * HEAD_DIM_A:(h + 1) * HEAD_DIM_A, t * b:(t + 1) * b])
        gb = i * nsub + t
        var = jnp.where(gb == 0, 0, jnp.where(gb == n_blocks - 1, 2, 1))
        return _dot(k_s[t * b:t * b + WIN_A, :], rhs_s[u]) + bias_ref[var, j]

    s_next = scores(0)
    for u, (t, j) in enumerate(units):
        s = s_next
        if u + 1 < len(units):
            s_next = scores(u + 1)
        sk = sink_ref[j]
        m = jnp.maximum(jnp.max(s, axis=0, keepdims=True), sk)
        p = jnp.exp2(s - m)
        r = _dot(vt_s[j * VT_ROWS:(j + 1) * VT_ROWS, t * b:t * b + WIN_A], p.astype(BF16))
        den = r[HEAD_DIM_A:HEAD_DIM_A + 1] + jnp.exp2(sk - m)
        o = (r[0:HEAD_DIM_A] * (1.0 / den)).astype(BF16)
        for g in range(GQA):
            h = GQA * j + g
            o_ref[h * HEAD_DIM_A:(h + 1) * HEAD_DIM_A, t * b:(t + 1) * b] = o[:, g * b:(g + 1) * b]


def _attn_a_consts(sink):
    b = ATT_BLOCK
    qi = np.arange(b)[None, :]
    s = np.arange(WIN_A)[:, None]
    dist = np.abs(b + qi - s).astype(np.float32)
    in_win = dist <= WINDOW
    slopes = jnp.exp2(-8.0 * jnp.arange(1, N_HEADS_A + 1, dtype=F32) / N_HEADS_A) * LOG2E
    alibi = -slopes.reshape(KV_HEADS_A, 1, GQA, 1) * jnp.asarray(dist)[None, :, None, :]
    variants = []
    for valid in (s >= b, s >= 0, s < 2 * b):
        mask = jnp.asarray(in_win & valid)[None, :, None, :]
        variants.append(jnp.where(mask, alibi, NEG_BIG).reshape(KV_HEADS_A, WIN_A, NQ_A))
    bias = jnp.stack(variants)
    sink_cols = jnp.repeat(sink.astype(F32).reshape(KV_HEADS_A, GQA) * LOG2E, b, axis=1)[:, None, :]
    return bias, sink_cols


def _attn_a(qt, ka, vt, sink, batch, seq):
    nq = seq // QB_A
    nblk = seq // ATT_BLOCK
    per = QB_A // ATT_BLOCK
    bias, sink_cols = _attn_a_consts(sink)
    kcol = 0
    prev = lambda i: jnp.maximum(i * per - 1, 0)
    nxt = lambda i: jnp.minimum((i + 1) * per, nblk - 1)
    vrows = KV_HEADS_A * VT_ROWS
    return pl.pallas_call(
        functools.partial(_attn_a_kernel, n_blocks=nblk),
        grid=(batch, nq),
        in_specs=[
            pl.BlockSpec((None, WIDTH_A, QB_A), lambda b, i: (b, 0, i)),
            pl.BlockSpec((QB_A, LANES), lambda b, i: (b * nq + i, kcol)),
            pl.BlockSpec((ATT_BLOCK, LANES), lambda b, i: (b * nblk + prev(i), kcol)),
            pl.BlockSpec((ATT_BLOCK, LANES), lambda b, i: (b * nblk + nxt(i), kcol)),
            pl.BlockSpec((None, vrows, QB_A), lambda b, i: (b, 0, i)),
            pl.BlockSpec((None, vrows, ATT_BLOCK), lambda b, i: (b, 0, prev(i))),
            pl.BlockSpec((None, vrows, ATT_BLOCK), lambda b, i: (b, 0, nxt(i))),
            _const_spec((3, KV_HEADS_A, WIN_A, NQ_A)),
            _const_spec((KV_HEADS_A, 1, NQ_A)),
        ],
        out_specs=pl.BlockSpec((None, WIDTH_A, QB_A), lambda b, i: (b, 0, i)),
        out_shape=jax.ShapeDtypeStruct((batch, WIDTH_A, seq), BF16),
        scratch_shapes=[pltpu.VMEM((QB_A + 2 * ATT_BLOCK, LANES), BF16),
                        pltpu.VMEM((vrows, QB_A + 2 * ATT_BLOCK), BF16),
                        pltpu.VMEM((per * KV_HEADS_A, LANES, NQ_A), BF16)],
        compiler_params=_cparams(("parallel", "parallel")),
        name="attn_a",
    )(qt, ka, ka, ka, vt, vt, vt, bias, sink_cols)


RB_S = 512
HALO = 16


def _ssd_kernel(*refs, rev, n_steps):
    if rev:
        (cur_ref, prev_ref, next_ref, dt_ref, dtt_ref, cw_ref, cb_ref, dtb_c_ref, dtb_r_ref,
         alog_c_ref, alog_r_ref, ltri_ref, utri_ref, dskip_ref, g_ref, yf_ref, out_ref, st_ref) = refs
    else:
        (cur_ref, prev_ref, next_ref, dt_ref, dtt_ref, cw_ref, cb_ref, dtb_c_ref, dtb_r_ref,
         alog_c_ref, alog_r_ref, ltri_ref, utri_ref, out_ref, st_ref) = refs
    i = pl.program_id(1)
    ii = (n_steps - 1 - i) if rev else i
    d = 1 if rev else 0

    @pl.when(i == 0)
    def _():
        st_ref[...] = jnp.zeros_like(st_ref)

    xc = cur_ref[:, 0:CONV_DIM].astype(F32)
    prow = jnp.where(ii == 0, 0.0, prev_ref[...].astype(F32)[HALO - 1:HALO, :])
    nrow = jnp.where(ii == n_steps - 1, 0.0, next_ref[...].astype(F32)[0:1, :])
    rid = lax.broadcasted_iota(jnp.int32, (RB_S, CONV_DIM), 0)
    xm1 = jnp.where(rid == 0, prow, pltpu.roll(xc, 1, 0))
    xp1 = jnp.where(rid == RB_S - 1, nrow, pltpu.roll(xc, RB_S - 1, 0))
    u = xm1 * cw_ref[0:1, :] + xc * cw_ref[1:2, :] + xp1 * cw_ref[2:3, :] + cb_ref[...]
    u = u * _sigmoid(u)

    dt_c = _softplus(dt_ref[:, 0:2 * SSD_HEADS] + dtb_c_ref[...])
    dt_r = _softplus(dtt_ref[...] + dtb_r_ref[...])
    da_c = dt_c * (-jnp.exp(alog_c_ref[...]))
    da_r = dt_r * (-jnp.exp(alog_r_ref[...]))

    lane = lax.broadcasted_iota(jnp.int32, (CHUNK, LANES), 1)
    lo = lane < SSD_HEAD_DIM
    lo_row = lax.broadcasted_iota(jnp.int32, (1, LANES), 1) < SSD_HEAD_DIM
    li = lax.broadcasted_iota(jnp.int32, (CHUNK, CHUNK), 0)
    si = lax.broadcasted_iota(jnp.int32, (CHUNK, CHUNK), 1)
    tri = (si >= li) if rev else (si <= li)

    n_chunks = RB_S // CHUNK
    order = range(n_chunks - 1, -1, -1) if rev else range(n_chunks)
    for c in order:
        r0 = c * CHUNK
        dac = da_c[r0:r0 + CHUNK]
        dar = da_r[:, r0:r0 + CHUNK]
        cs_c = _dot_hi(ltri_ref[...], dac)
        cs_r = _dot_hi(dar, utri_ref[...])
        tot = cs_c[CHUNK - 1:CHUNK, :]
        if rev:
            e_c = cs_c - dac
            e_r = cs_r - dar
            w_b = jnp.exp(e_c)
            w_c = jnp.exp(tot - e_c)
        else:
            e_c = cs_c
            e_r = cs_r
            w_b = jnp.exp(tot - cs_c)
            w_c = jnp.exp(cs_c)
        dec = jnp.exp(tot)
        dtc = dt_c[r0:r0 + CHUNK]
        uc = u[r0:r0 + CHUNK]
        bm = uc[:, D_INNER:D_INNER + LANES]
        cm = uc[:, D_INNER + LANES:D_INNER + 2 * LANES]
        for pr in range(SSD_GROUPS):
            gmask = lo if pr == 0 else jnp.logical_not(lo)
            cg = jnp.where(gmask, cm, 0.0).astype(BF16)
            bg = jnp.where(gmask, bm, 0.0).astype(BF16)
            gmat = _dot_nt(cg, bg)
            xp = uc[:, pr * LANES:(pr + 1) * LANES]
            l0 = d * SSD_HEADS + 2 * pr
            l1 = l0 + 1
            pair = lambda a: jnp.where(lo, a[:, l0:l0 + 1], a[:, l1:l1 + 1])
            xdt = xp * pair(dtc)
            y = jnp.zeros((CHUNK, LANES), F32)
            for hh, dl in ((0, l0), (1, l1)):
                ecol = e_c[:, dl:dl + 1]
                erow = e_r[dl:dl + 1, :]
                diff = (erow - ecol) if rev else (ecol - erow)
                dmat = jnp.where(tri, jnp.exp(jnp.minimum(diff, 0.0)), 0.0)
                hmask = lo if hh == 0 else jnp.logical_not(lo)
                y = y + _dot((gmat * dmat).astype(BF16), jnp.where(hmask, xdt, 0.0).astype(BF16))
            st = st_ref[pr]
            y = y + pair(w_c) * _dot(cg, st.astype(BF16))
            s_new = _dot_tn(bg, (xdt * pair(w_b)).astype(BF16))
            dec_pair = jnp.where(lo_row, dec[:, l0:l0 + 1], dec[:, l1:l1 + 1])
            st_ref[pr] = st * dec_pair + s_new
            if rev:
                cols = slice(pr * LANES, (pr + 1) * LANES)
                y = y + yf_ref[r0:r0 + CHUNK, cols] + xp * dskip_ref[:, cols]
            out_ref[r0:r0 + CHUNK, pr * LANES:(pr + 1) * LANES] = y

    if rev:
        z = cur_ref[:, CONV_DIM:PB_W].astype(F32)
        yz = out_ref[...] * (z * _sigmoid(z))
        ms = jnp.mean(yz * yz, axis=-1, keepdims=True)
        out_ref[...] = yz * lax.rsqrt(ms + EPS) * g_ref[...]


def _ssd(pb, dt, dtt, params, yf, batch, seq):
    rev = yf is not None
    t = batch * seq
    ns = seq // RB_S
    per = RB_S // HALO
    nh = seq // HALO
    pos = (lambda i: ns - 1 - i) if rev else (lambda i: i)
    ltri = jnp.asarray(np.tril(np.ones((CHUNK, CHUNK), np.float32)))
    in_specs = [
        pl.BlockSpec((RB_S, PB_W), lambda b, i: (b * ns + pos(i), 0)),
        pl.BlockSpec((HALO, CONV_DIM), lambda b, i: (b * nh + jnp.maximum(pos(i) * per - 1, 0), 0)),
        pl.BlockSpec((HALO, CONV_DIM), lambda b, i: (b * nh + jnp.minimum((pos(i) + 1) * per, nh - 1), 0)),
        pl.BlockSpec((RB_S, DT_W), lambda b, i: (b * ns + pos(i), 0)),
        pl.BlockSpec((2 * SSD_HEADS, RB_S), lambda b, i: (0, b * ns + pos(i))),
        _const_spec((CONV_K, CONV_DIM)), _const_spec((1, CONV_DIM)),
        _const_spec((1, 2 * SSD_HEADS)), _const_spec((2 * SSD_HEADS, 1)),
        _const_spec((1, 2 * SSD_HEADS)), _const_spec((2 * SSD_HEADS, 1)),
        _const_spec((CHUNK, CHUNK)), _const_spec((CHUNK, CHUNK)),
    ]
    args = [pb, pb, pb, dt, dtt, params["conv_w"], params["conv_b"], params["dtb_c"], params["dtb_r"],
            params["alog_c"], params["alog_r"], ltri, ltri.T]
    if rev:
        in_specs += [_const_spec((1, D_INNER)), _const_spec((1, D_INNER)),
                     pl.BlockSpec((RB_S, D_INNER), lambda b, i: (b * ns + pos(i), 0))]
        args += [params["dskip"], params["norm_g"], yf]
    return pl.pallas_call(
        functools.partial(_ssd_kernel, rev=rev, n_steps=ns),
        grid=(batch, ns),
        in_specs=in_specs,
        out_specs=pl.BlockSpec((RB_S, D_INNER), lambda b, i: (b * ns + pos(i), 0)),
        out_shape=jax.ShapeDtypeStruct((t, D_INNER), F32),
        scratch_shapes=[pltpu.VMEM((SSD_GROUPS, LANES, LANES), F32)],
        compiler_params=_cparams(("parallel", "arbitrary")),
        name="ssd_bwd" if rev else "ssd_fwd",
    )(*args)


TQ_C = 512
TK_C = 512
N_MAPS = 4
KV_SUB = 4


def _attn_c_kernel(slope_ref, lam_ref, qt_ref, k_ref, vt_ref, bias_ref, g_ref, o_ref, qz_s, m_s, acc_s):
    hp = pl.program_id(1)
    qi = pl.program_id(2)
    ki = pl.program_id(3)

    @pl.when(ki == 0)
    def _():
        m_s[...] = jnp.full_like(m_s, NEG_BIG)
        acc_s[...] = jnp.zeros_like(acc_s)
        qz_s[...] = jnp.zeros_like(qz_s)
        for idx in range(N_MAPS):
            c = idx % 2
            qz_s[idx, c * DIFF_QK_DIM:(c + 1) * DIFF_QK_DIM, :] = qt_ref[idx * DIFF_QK_DIM:(idx + 1) * DIFF_QK_DIM, :]

    units = [(s, hh, c) for s in range(KV_SUB) for hh in range(2) for c in range(2)]

    def scores(u):
        s, hh, c = units[u]
        kt = ki * KV_SUB + s
        var = jnp.where(kt < qi, 0, jnp.where(kt == qi, 1, 2))
        return (_dot(k_ref[s * TK_C:(s + 1) * TK_C, hh * LANES:(hh + 1) * LANES], qz_s[2 * hh + c])
                + bias_ref[var, hh])

    t_next = scores(0)
    for u, (s, hh, c) in enumerate(units):
        t = t_next
        if u + 1 < len(units):
            t_next = scores(u + 1)
        idx = 2 * hh + c
        kt = ki * KV_SUB + s
        c_tile = -jnp.abs(qi * TQ_C - kt * TK_C).astype(F32) * slope_ref[2 * hp + hh]
        m_old = m_s[idx]
        m_new = jnp.maximum(m_old, jnp.max(t, axis=0, keepdims=True) + c_tile)
        alpha = jnp.exp2(m_old - m_new)
        p = jnp.exp2(t - (m_new - c_tile))
        vt = vt_ref[hh * VT_ROWS:(hh + 1) * VT_ROWS, s * TK_C:(s + 1) * TK_C]
        acc_s[idx] = alpha * acc_s[idx] + _dot(vt, p.astype(BF16))
        m_s[idx] = m_new

    @pl.when(ki == pl.num_programs(3) - 1)
    def _():
        _attn_c_finalize(lam_ref, g_ref, o_ref, acc_s)


def _attn_c_finalize(lam_ref, g_ref, o_ref, acc_s):
    lam = lam_ref[0]
    for hh in range(2):
        a0 = acc_s[2 * hh]
        a1 = acc_s[2 * hh + 1]
        o = (a0[0:DIFF_V_DIM] * (1.0 / a0[DIFF_V_DIM:DIFF_V_DIM + 1])
             - lam * (a1[0:DIFF_V_DIM] * (1.0 / a1[DIFF_V_DIM:DIFF_V_DIM + 1])))
        ms = jnp.mean(o * o, axis=0, keepdims=True)
        rows = slice(hh * DIFF_V_DIM, (hh + 1) * DIFF_V_DIM)
        o_ref[rows, :] = (o * lax.rsqrt(ms + EPS) * g_ref[rows, :]).astype(BF16)


AUG_ROWS = 16
AUG_LANES = 64
BOUND_SLACK = 1.02
BOUND_LIMIT = 100.0


def _split3(x):
    hi = x.astype(BF16).astype(F32)
    r = x - hi
    mid = r.astype(BF16).astype(F32)
    return hi, mid, (r - mid).astype(BF16).astype(F32)


def _attn_c_bounded_kernel(slope_ref, lam_ref, kmax_ref, qt_ref, k_ref, vt_ref, dtab_ref, g_ref, o_ref,
                           rhs_s, mref_s, acc_s):
    hp = pl.program_id(1)
    qi = pl.program_id(2)
    ki = pl.program_id(3)
    units = [(s, hh, c) for s in range(KV_SUB) for hh in range(2) for c in range(2)]

    @pl.when(ki == 0)
    def _():
        acc_s[...] = jnp.zeros_like(acc_s)
        rhs_s[...] = jnp.zeros_like(rhs_s)
        for u, (s, hh, c) in enumerate(units):
            idx = 2 * hh + c
            qseg = qt_ref[idx * DIFF_QK_DIM:(idx + 1) * DIFF_QK_DIM, :]
            rhs_s[u, c * DIFF_QK_DIM:(c + 1) * DIFF_QK_DIM, :] = qseg
            if s == 0:
                qf = qseg.astype(F32)
                kmax = kmax_ref[0] * BOUND_SLACK
                mref_s[idx] = jnp.sqrt(jnp.sum(qf * qf, axis=0, keepdims=True)) * kmax

    qloc = lax.broadcasted_iota(jnp.int32, (1, TQ_C), 1).astype(F32)
    row = lax.broadcasted_iota(jnp.int32, (AUG_ROWS, TQ_C), 0)
    grp = row // 3
    part = row - 3 * grp

    def scores(u):
        s, hh, c = units[u]
        kt = ki * KV_SUB + s
        h = 2 * hp + hh
        sgn = jnp.where(kt < qi, 1.0, jnp.where(kt == qi, 0.0, -1.0)).astype(F32)
        slope = slope_ref[4 * h + 3]
        tile_dist = jnp.abs(qi * TQ_C - kt * TK_C).astype(F32)
        shift = mref_s[2 * hh + c] + slope * tile_dist + (sgn * slope) * qloc
        hi, mid, lo = _split3(shift)
        shift_part = jnp.where(part == 0, hi, jnp.where(part == 1, mid, lo))
        slope_part = sgn * jnp.where(part == 0, slope_ref[4 * h], jnp.where(part == 1, slope_ref[4 * h + 1],
                                                                           slope_ref[4 * h + 2]))
        aug = jnp.where(grp == 0, -shift_part, jnp.where(grp <= 2, slope_part, 0.0))
        rhs_s[u, 2 * DIFF_QK_DIM:2 * DIFF_QK_DIM + AUG_ROWS, :] = aug.astype(BF16)
        on_diag = (kt == qi).astype(jnp.int32)
        return _dot(k_ref[s * TK_C:(s + 1) * TK_C, hh * LANES:(hh + 1) * LANES], rhs_s[u]) + dtab_ref[on_diag, hh]

    t_next = scores(0)
    for u, (s, hh, c) in enumerate(units):
        t = t_next
        if u + 1 < len(units):
            t_next = scores(u + 1)
        idx = 2 * hh + c
        vt = vt_ref[hh * VT_ROWS:(hh + 1) * VT_ROWS, s * TK_C:(s + 1) * TK_C]
        acc_s[idx] = acc_s[idx] + _dot(vt, jnp.exp2(t).astype(BF16))

    @pl.when(ki == pl.num_programs(3) - 1)
    def _():
        _attn_c_finalize(lam_ref, g_ref, o_ref, acc_s)


def _attn_c_bias(slopes):
    rel = (np.arange(TQ_C)[None, :] - np.arange(TK_C)[:, None]).astype(np.float32)
    tables = jnp.asarray(np.stack([-rel, -np.abs(rel), rel]))
    s = slopes.reshape(DIFF_HEADS // 2, 1, 2, 1, 1)
    return s * tables[None, :, None]


def _attn_c_call(body, prefetch, tensors, table_variants, scratch, batch, seq, name):
    nq = seq // TQ_C
    kv_step = TK_C * KV_SUB
    nk = seq // kv_step
    n_pre = len(prefetch)
    grid_spec = pltpu.PrefetchScalarGridSpec(
        num_scalar_prefetch=n_pre,
        grid=(batch, DIFF_HEADS // 2, nq, nk),
        in_specs=[
            pl.BlockSpec((None, LANES, TQ_C), lambda b, h, i, j, *_: (b, h, i)),
            pl.BlockSpec((kv_step, 2 * LANES), lambda b, h, i, j, *_: (b * nk + j, h)),
            pl.BlockSpec((None, 2 * VT_ROWS, kv_step), lambda b, h, i, j, *_: (b, h, j)),
            pl.BlockSpec((None, table_variants, 2, TK_C, TQ_C), lambda b, h, i, j, *_: (h, 0, 0, 0, 0)),
            _const_spec((LANES, TQ_C)),
        ],
        out_specs=pl.BlockSpec((None, LANES, TQ_C), lambda b, h, i, j, *_: (b, h, i)),
        scratch_shapes=scratch + [pltpu.VMEM((N_MAPS, 1, TQ_C), F32), pltpu.VMEM((N_MAPS, VT_ROWS, TQ_C), F32)],
    )
    return pl.pallas_call(
        body,
        grid_spec=grid_spec,
        out_shape=jax.ShapeDtypeStruct((batch, WIDTH_C, seq), BF16),
        compiler_params=_cparams(("parallel", "parallel", "parallel", "arbitrary")),
        name=name,
    )(*prefetch, *tensors)


def _attn_c(qt, k_aug, vt, qk_gains, slopes, lam, gain, batch, seq):
    kv_step = TK_C * KV_SUB
    assert seq % kv_step == 0 and seq % TQ_C == 0
    gain_b = jnp.broadcast_to(gain.reshape(LANES, 1), (LANES, TQ_C))
    tables = _attn_c_bias(slopes)
    seg_bound = lambda g: math.sqrt(DIFF_QK_DIM) * jnp.max(jnp.abs(g))
    qmax = seg_bound(qk_gains[:, 0:256])
    kmax = seg_bound(qk_gains[:, 256:512])

    def online():
        scratch = [pltpu.VMEM((N_MAPS, LANES, TQ_C), BF16)]
        return _attn_c_call(_attn_c_kernel, (slopes, lam), (qt, k_aug, vt, tables, gain_b), 3, scratch,
                            batch, seq, "attn_c_online")

    def bounded():
        slope4 = jnp.stack(_split3(slopes) + (slopes,), axis=1).reshape(-1)
        dtab = jnp.stack([jnp.zeros_like(tables[:, 1]), tables[:, 1]], axis=1)
        scratch = [pltpu.VMEM((N_MAPS * KV_SUB, LANES, TQ_C), BF16)]
        return _attn_c_call(_attn_c_bounded_kernel, (slope4, lam, jnp.reshape(kmax, (1,))),
                            (qt, k_aug, vt, dtab, gain_b), 2, scratch, batch, seq, "attn_c_bounded")

    safe = 2.0 * BOUND_SLACK * BOUND_SLACK * qmax * kmax < BOUND_LIMIT
    return lax.cond(safe, bounded, online)


def _out_proj(oat_ref, ob_ref, oct_ref, x_ref, wo_ref):
    untr = lambda ref: ref[...].astype(F32).T.astype(BF16)
    o = jnp.concatenate([untr(oat_ref), ob_ref[...].astype(BF16), untr(oct_ref)], axis=-1)
    return x_ref[...] + _dot(o, wo_ref[...])


def _rms(x, g_ref):
    ms = jnp.mean(x * x, axis=-1, keepdims=True)
    return x * lax.rsqrt(ms + EPS) * g_ref[...]


def _swiglu_acc(h, wg_ref, wu_ref, wd_ref, acc, d_ff):
    step = d_ff // FF_CHUNKS
    for c0 in range(0, d_ff, step):
        gate = _dot(h, wg_ref[:, c0:c0 + step])
        up = _dot(h, wu_ref[:, c0:c0 + step])
        act = (gate * _sigmoid(gate) * up).astype(BF16)
        acc = acc + _dot(act, wd_ref[c0:c0 + step, :])
    return acc


def _out_ffn_kernel(oa_ref, ob_ref, oc_ref, x_ref, wo_ref, g_ref, wg_ref, wu_ref, wd_ref, out_ref):
    x1 = _out_proj(oa_ref, ob_ref, oc_ref, x_ref, wo_ref)
    h = _rms(x1, g_ref).astype(BF16)
    out_ref[...] = _swiglu_acc(h, wg_ref, wu_ref, wd_ref, x1, D_FF)


def _mixer_out_specs(seq):
    nps = seq // TM
    row = lambda w: pl.BlockSpec((TM, w), lambda i: (i, 0))
    tr = lambda r: pl.BlockSpec((None, r, TM), lambda i: (i // nps, 0, i % nps))
    return [tr(WIDTH_A), row(D_INNER), tr(WIDTH_C), row(D_MODEL), _resident_spec((D_MODEL, D_MODEL)),
            _const_spec((1, D_MODEL))]


def _out_ffn(oa, ob, oc, x, w_out, g, wg, wu, wd):
    t = x.shape[0]
    return pl.pallas_call(
        _out_ffn_kernel,
        grid=(t // TM,),
        in_specs=_mixer_out_specs(oa.shape[2]) + [_resident_spec((D_MODEL, D_FF)), _resident_spec((D_MODEL, D_FF)),
                                       _resident_spec((D_FF, D_MODEL))],
        out_specs=pl.BlockSpec((TM, D_MODEL), lambda i: (i, 0)),
        out_shape=jax.ShapeDtypeStruct((t, D_MODEL), F32),
        compiler_params=_cparams(("parallel",)),
        name="out_ffn",
    )(oa, ob, oc, x, w_out, g, wg, wu, wd)


def _out_router_kernel(oa_ref, ob_ref, oc_ref, x_ref, wo_ref, g_ref, wr_hi_ref, wr_lo_ref,
                       x1_ref, h_ref, sel_ref, gate_ref):
    x1 = _out_proj(oa_ref, ob_ref, oc_ref, x_ref, wo_ref)
    x1_ref[...] = x1
    h = _rms(x1, g_ref)
    h_hi = h.astype(BF16)
    h_ref[...] = h_hi
    h_lo = (h - h_hi.astype(F32)).astype(BF16)
    logits = _dot(h_hi, wr_hi_ref[...]) + (_dot(h_lo, wr_hi_ref[...]) + _dot(h_hi, wr_lo_ref[...]))
    lane_i = lax.broadcasted_iota(jnp.int32, (TM, LANES), 1)
    lane = lane_i.astype(F32)
    logits = jnp.where(lane_i < N_EXPERTS, logits, -jnp.inf)
    v1 = jnp.max(logits, axis=-1, keepdims=True)
    i1 = jnp.min(jnp.where(logits == v1, lane, float(LANES)), axis=-1, keepdims=True)
    rest = jnp.where(lane == i1, -jnp.inf, logits)
    v2 = jnp.max(rest, axis=-1, keepdims=True)
    i2 = jnp.min(jnp.where(rest == v2, lane, float(LANES)), axis=-1, keepdims=True)
    e = jnp.exp(v2 - v1)
    g1 = 1.0 / (1.0 + e)
    sel_ref[...] = jnp.where(lane_i == 0, i1, jnp.where(lane_i == 1, i2, 0.0)).astype(jnp.int32)
    gate_ref[...] = jnp.where(lane_i == 0, g1, jnp.where(lane_i == 1, e * g1, 0.0))


def _out_router(oa, ob, oc, x, w_out, g, wr_hi, wr_lo):
    t = x.shape[0]
    row = lambda w: pl.BlockSpec((TM, w), lambda i: (i, 0))
    return pl.pallas_call(
        _out_router_kernel,
        grid=(t // TM,),
        in_specs=_mixer_out_specs(oa.shape[2]) + [_const_spec((D_MODEL, LANES)), _const_spec((D_MODEL, LANES))],
        out_specs=[row(D_MODEL), row(D_MODEL), row(LANES), row(LANES)],
        out_shape=[jax.ShapeDtypeStruct((t, D_MODEL), F32), jax.ShapeDtypeStruct((t, D_MODEL), BF16),
                   jax.ShapeDtypeStruct((t, LANES), jnp.int32), jax.ShapeDtypeStruct((t, LANES), F32)],
        compiler_params=_cparams(("parallel",)),
        name="out_router",
    )(oa, ob, oc, x, w_out, g, wr_hi, wr_lo)


def _moe_ffn_kernel(blk_exp_ref, n_used_ref, x_ref, wg_ref, wu_ref, wd_ref, out_ref):
    i = pl.program_id(0)

    @pl.when(i < n_used_ref[0])
    def _():
        acc = jnp.zeros((MOE_BLK, D_MODEL), F32)
        out_ref[...] = _swiglu_acc(x_ref[...], wg_ref.at[0], wu_ref.at[0], wd_ref.at[0], acc,
                                   D_FF_EXPERT).astype(BF16)

    @pl.when(i >= n_used_ref[0])
    def _():
        out_ref[...] = jnp.zeros_like(out_ref)


def _moe_ffn(xg, blk_exp, n_used, wg, wu, wd):
    rows = xg.shape[0]
    wspec = lambda shape: pl.BlockSpec((1,) + shape, lambda i, be, nu: (be[i], 0, 0),
                                       pipeline_mode=pl.Buffered(1))
    grid_spec = pltpu.PrefetchScalarGridSpec(
        num_scalar_prefetch=2,
        grid=(rows // MOE_BLK,),
        in_specs=[pl.BlockSpec((MOE_BLK, D_MODEL), lambda i, be, nu: (i, 0)),
                  wspec((D_MODEL, D_FF_EXPERT)), wspec((D_MODEL, D_FF_EXPERT)),
                  wspec((D_FF_EXPERT, D_MODEL))],
        out_specs=pl.BlockSpec((MOE_BLK, D_MODEL), lambda i, be, nu: (i, 0)),
    )
    return pl.pallas_call(
        _moe_ffn_kernel,
        grid_spec=grid_spec,
        out_shape=jax.ShapeDtypeStruct((rows, D_MODEL), BF16),
        compiler_params=_cparams(("arbitrary",)),
        name="moe_ffn",
    )(blk_exp, n_used, xg, wg, wu, wd)


def _moe(x1, h, sel, gates, wg, wu, wd):
    t = x1.shape[0]
    i1, i2 = sel[:, 0], sel[:, 1]
    g1, g2 = gates[:, 0:1], gates[:, 1:2]
    experts = jnp.arange(N_EXPERTS, dtype=jnp.int32)
    onehot = ((i1[:, None] == experts) | (i2[:, None] == experts)).astype(jnp.int32)
    rank = jnp.cumsum(onehot, axis=0) - onehot
    counts = jnp.sum(onehot, axis=0)
    padded = ((counts + MOE_BLK - 1) // MOE_BLK) * MOE_BLK
    pad_end = jnp.cumsum(padded)
    pad_start = pad_end - padded
    dest = pad_start[None, :] + rank
    d1 = jnp.take_along_axis(dest, i1[:, None], axis=1)[:, 0]
    d2 = jnp.take_along_axis(dest, i2[:, None], axis=1)[:, 0]
    n_blocks = (2 * t) // MOE_BLK + N_EXPERTS
    rows = n_blocks * MOE_BLK
    tok = jnp.arange(t, dtype=jnp.int32)
    row_tok = jnp.zeros((rows,), jnp.int32).at[jnp.concatenate([d1, d2])].set(jnp.concatenate([tok, tok]))
    blk_start = jnp.arange(n_blocks, dtype=jnp.int32) * MOE_BLK
    blk_exp = jnp.minimum(jnp.sum((pad_end[None, :] <= blk_start[:, None]).astype(jnp.int32), axis=1),
                          N_EXPERTS - 1)
    n_used = (pad_end[-1:] // MOE_BLK).astype(jnp.int32)
    yb = _moe_ffn(h[row_tok], blk_exp, n_used, wg, wu, wd)
    return x1 + g1 * yb[d1].astype(F32) + g2 * yb[d2].astype(F32)


def _split_cols(w):
    outs, start = [], 0
    for n in SPLIT_SIZES:
        outs.append(w[:, start:start + n])
        start += n
    return outs


def _layer_params(l, p):
    qa, ka, va, zb, xbc, dtb, qc, kc, vc = _split_cols(p["w_in"][l])
    w_cat = jnp.concatenate(
        [qa, ka, va, xbc, zb, qc, kc, vc, dtb, jnp.zeros((D_MODEL, DT_W - 2 * SSD_HEADS), F32)],
        axis=1).astype(BF16)
    ga = jnp.concatenate([jnp.tile(p["qnorm_a"][l] * (HEAD_DIM_A ** -0.5 * LOG2E), N_HEADS_A),
                          jnp.tile(p["knorm_a"][l], KV_HEADS_A)])[None, :]
    gc = jnp.concatenate([jnp.tile(p["qnorm_c"][l] * (DIFF_QK_DIM ** -0.5 * LOG2E), 2 * DIFF_HEADS),
                          jnp.tile(p["knorm_c"][l], 2 * DIFF_HEADS)])[None, :]
    lam_init = 0.8 - 0.6 * math.exp(-0.3 * l)
    lc = p["lam_c"][l].astype(F32)
    lam = jnp.exp(jnp.sum(lc[0] * lc[1])) - jnp.exp(jnp.sum(lc[2] * lc[3])) + lam_init
    out = {
        "norm_mix_g": p["norm_mix_g"][l][None, :], "w_cat": w_cat, "ga": ga, "gc": gc,
        "sink": p["sink_a"][l],
        "ssd": {
            "conv_w": p["conv_w"][l], "conv_b": p["conv_b"][l][None, :],
            "dtb_c": p["dt_bias"][l].reshape(1, -1), "dtb_r": p["dt_bias"][l].reshape(-1, 1),
            "alog_c": p["a_log"][l].reshape(1, -1), "alog_r": p["a_log"][l].reshape(-1, 1),
            "dskip": jnp.repeat(p["d_skip"][l], SSD_HEAD_DIM)[None, :],
            "norm_g": p["ssd_norm_g"][l][None, :],
        },
        "slopes_c": jnp.exp2(-8.0 * jnp.arange(1, DIFF_HEADS + 1, dtype=F32) / DIFF_HEADS) * LOG2E,
        "lam": jnp.reshape(lam, (1,)).astype(F32),
        "gain_c": (jnp.tile(p["diff_norm_g"][l], 2) * (1.0 - lam_init))[None, :],
        "w_out": p["w_out"][l].astype(BF16),
        "norm_ffn_g": p["norm_ffn_g"][l][None, :],
    }
    i = l // 2
    if l % 2 == 0:
        out["ffn"] = (p["ffn_w_gate"][i].astype(BF16), p["ffn_w_up"][i].astype(BF16),
                      p["ffn_w_down"][i].astype(BF16))
    else:
        wr = jnp.pad(p["router_w"][i], ((0, 0), (0, LANES - N_EXPERTS)))
        wr_hi = wr.astype(BF16)
        out["router"] = (wr_hi, (wr - wr_hi.astype(F32)).astype(BF16))
        out["moe"] = (p["moe_w_gate"][i].astype(BF16), p["moe_w_up"][i].astype(BF16),
                      p["moe_w_down"][i].astype(BF16))
    return out


def _trunk(x3, layers):
    batch, seq, _ = x3.shape
    x = x3.reshape(batch * seq, D_MODEL)
    for l, lp in enumerate(layers):
        qat, ka, vat, pb, qct, kc, vct, dt, dtt = _norm_proj(x, lp["norm_mix_g"], lp["w_cat"], lp["ga"], lp["gc"],
                                                             batch, seq)
        oa = _attn_a(qat, ka, vat, lp["sink"], batch, seq)
        yf = _ssd(pb, dt, dtt, lp["ssd"], None, batch, seq)
        ob = _ssd(pb, dt, dtt, lp["ssd"], yf, batch, seq)
        oc = _attn_c(qct, kc, vct, lp["gc"], lp["slopes_c"], lp["lam"], lp["gain_c"], batch, seq)
        if l % 2 == 0:
            x = _out_ffn(oa, ob, oc, x, lp["w_out"], lp["norm_ffn_g"], *lp["ffn"])
        else:
            x1, h, sel, gates = _out_router(oa, ob, oc, x, lp["w_out"], lp["norm_ffn_g"], *lp["router"])
            x = _moe(x1, h, sel, gates, *lp["moe"])
    return x.reshape(batch, seq, D_MODEL)


def kernel(x_prompt, x_sample, norm_mix_g, w_in, qnorm_a, knorm_a, sink_a, conv_w, conv_b, dt_bias, a_log,
           d_skip, ssd_norm_g, qnorm_c, knorm_c, lam_c, diff_norm_g, w_out, norm_ffn_g, ffn_w_gate, ffn_w_up,
           ffn_w_down, router_w, moe_w_gate, moe_w_up, moe_w_down):
    p = dict(norm_mix_g=norm_mix_g, w_in=w_in, qnorm_a=qnorm_a, knorm_a=knorm_a, sink_a=sink_a,
             conv_w=conv_w, conv_b=conv_b, dt_bias=dt_bias, a_log=a_log, d_skip=d_skip,
             ssd_norm_g=ssd_norm_g, qnorm_c=qnorm_c, knorm_c=knorm_c, lam_c=lam_c, diff_norm_g=diff_norm_g,
             w_out=w_out, norm_ffn_g=norm_ffn_g, ffn_w_gate=ffn_w_gate, ffn_w_up=ffn_w_up,
             ffn_w_down=ffn_w_down, router_w=router_w, moe_w_gate=moe_w_gate, moe_w_up=moe_w_up,
             moe_w_down=moe_w_down)
    layers = [_layer_params(l, p) for l in range(DEPTH)]
    return _trunk(x_prompt, layers), _trunk(x_sample, layers)
```

```python
import functools
import math

import numpy as np
import jax
import jax.numpy as jnp
from jax import lax
from jax.experimental import pallas as pl
from jax.experimental.pallas import tpu as pltpu

F32 = jnp.float32
BF16 = jnp.bfloat16

D_MODEL = 1024
DEPTH = 2
N_HEADS_A = 8
KV_HEADS_A = 2
HEAD_DIM_A = 64
WINDOW = 128
ATT_BLOCK = 128
SSD_HEADS = 4
SSD_HEAD_DIM = 64
D_INNER = SSD_HEADS * SSD_HEAD_DIM
SSD_GROUPS = 2
D_STATE = 64
CONV_K = 3
CHUNK = 128
DIFF_HEADS = 4
DIFF_QK_DIM = 32
DIFF_V_DIM = 64
WIDTH_A = N_HEADS_A * HEAD_DIM_A
WIDTH_C = DIFF_HEADS * DIFF_V_DIM
CONV_DIM = D_INNER + 2 * SSD_GROUPS * D_STATE
SPLIT_SIZES = (WIDTH_A, KV_HEADS_A * HEAD_DIM_A, KV_HEADS_A * HEAD_DIM_A,
               D_INNER, CONV_DIM, 2 * SSD_HEADS,
               DIFF_HEADS * 2 * DIFF_QK_DIM, DIFF_HEADS * 2 * DIFF_QK_DIM, WIDTH_C)
D_FF = 2816
N_EXPERTS = 8
D_FF_EXPERT = 3584
EPS = 1e-6

LOG2E = 1.4426950408889634
NEG_BIG = -1e30

LANES = 128
VMEM_LIMIT = 56 * 1024 * 1024

PA_W = 768
PA_NORM_W = 640
VT_ROWS = 80
PB_W = 768
PC_W = 768
DT_W = 128
PROJ_W = PA_W + PB_W + PC_W + DT_W

TM = 512
MOE_BLK = 512
FFN_CHUNKS = 1
MOE_CHUNKS = 2


def _cparams(sem, vmem=VMEM_LIMIT):
    return pltpu.CompilerParams(dimension_semantics=sem, vmem_limit_bytes=vmem)


def _const_spec(shape):
    nd = len(shape)
    return pl.BlockSpec(shape, lambda *_: (0,) * nd)


def _resident_spec(shape):
    nd = len(shape)
    return pl.BlockSpec(shape, lambda *_: (0,) * nd, pipeline_mode=pl.Buffered(1))


def _block_diag(n, seg):
    idx = np.arange(n) // seg
    return jnp.asarray((idx[:, None] == idx[None, :]).astype(np.float32), dtype=BF16)


def _sigmoid(x):
    return 1.0 / (1.0 + jnp.exp(-x))


def _softplus(x):
    return jnp.maximum(x, 0.0) + jnp.log(1.0 + jnp.exp(-jnp.abs(x)))


def _dot(a, b):
    return jnp.dot(a, b, preferred_element_type=F32)


def _dot_nt(a, b):
    return lax.dot_general(a, b, (((1,), (1,)), ((), ())), preferred_element_type=F32)


def _dot_tn(a, b):
    return lax.dot_general(a, b, (((0,), (0,)), ((), ())), preferred_element_type=F32)


def _dot_hi(a, b):
    return jnp.dot(a, b, preferred_element_type=F32, precision=lax.Precision.HIGHEST)


def _seg_norm(blk, seg, bd_ref, gain):
    w = blk.shape[1]
    ss = _dot((blk * blk).astype(BF16), bd_ref[0:w, 0:w]) * (1.0 / seg)
    return blk * lax.rsqrt(ss + EPS) * gain


def _store_vt(vt_ref, v, heads, head_dim, pad_ref):
    vt = v.T
    for h in range(heads):
        r0 = h * VT_ROWS
        vt_ref[r0:r0 + head_dim, :] = vt[h * head_dim:(h + 1) * head_dim].astype(BF16)
        vt_ref[r0 + head_dim:r0 + VT_ROWS, :] = pad_ref[...]


def _norm_proj_kernel(x_ref, g_ref, w_ref, bd64_ref, bd32_ref, ga_ref, gc_ref, pad_ref, kaug_ref,
                      qat_ref, ka_ref, vat_ref, pb_ref, qct_ref, kc_ref, vct_ref, dt_ref, dtt_ref):
    x = x_ref[...]
    ms = jnp.mean(x * x, axis=-1, keepdims=True)
    h = (x * lax.rsqrt(ms + EPS) * g_ref[...]).astype(BF16)
    ya = _dot(h, w_ref[:, 0:PA_W])
    for c0 in range(0, WIDTH_A, 256):
        qn = _seg_norm(ya[:, c0:c0 + 256], HEAD_DIM_A, bd64_ref, ga_ref[:, c0:c0 + 256])
        qat_ref[c0:c0 + 256, :] = qn.T.astype(BF16)
    ka_ref[...] = _seg_norm(ya[:, WIDTH_A:PA_NORM_W], HEAD_DIM_A, bd64_ref, ga_ref[:, WIDTH_A:PA_NORM_W]).astype(BF16)
    _store_vt(vat_ref, ya[:, PA_NORM_W:PA_W], KV_HEADS_A, HEAD_DIM_A, pad_ref)
    pb_ref[...] = _dot(h, w_ref[:, PA_W:PA_W + PB_W]).astype(BF16)
    yc = _dot(h, w_ref[:, PA_W + PB_W:PA_W + PB_W + PC_W])
    qct_ref[...] = _seg_norm(yc[:, 0:256], DIFF_QK_DIM, bd32_ref, gc_ref[:, 0:256]).T.astype(BF16)
    kc = _seg_norm(yc[:, 256:512], DIFF_QK_DIM, bd32_ref, gc_ref[:, 256:512]).astype(BF16)
    aug = kaug_ref[...]
    pieces = []
    for hd in range(DIFF_HEADS):
        pieces += [kc[:, hd * 2 * DIFF_QK_DIM:(hd + 1) * 2 * DIFF_QK_DIM], aug]
    kc_ref[...] = jnp.concatenate(pieces, axis=1)
    _store_vt(vct_ref, yc[:, 512:PC_W], DIFF_HEADS, DIFF_V_DIM, pad_ref)
    yd = _dot(h, w_ref[:, PA_W + PB_W + PC_W:PROJ_W])
    dt_ref[...] = yd
    dtt_ref[...] = yd.T[0:2 * SSD_HEADS, :]


def _norm_proj(x, g, w_cat, ga, gc, batch, seq):
    t = batch * seq
    nps = seq // TM
    row = lambda w: pl.BlockSpec((TM, w), lambda i: (i, 0))
    tr = lambda r: pl.BlockSpec((None, r, TM), lambda i: (i // nps, 0, i % nps))
    pad = jnp.zeros((VT_ROWS - HEAD_DIM_A, TM), BF16).at[0].set(1.0)
    pos = np.arange(TM) % TK_C
    cols = np.zeros((TM, AUG_LANES), np.float32)
    cols[:, 0:3] = 1.0
    cols[:, 3:6] = (pos // 256 * 256)[:, None]
    cols[:, 6:9] = (pos % 256)[:, None]
    return pl.pallas_call(
        _norm_proj_kernel,
        grid=(t // TM,),
        in_specs=[row(D_MODEL), _const_spec((1, D_MODEL)), _resident_spec((D_MODEL, PROJ_W)),
                  _const_spec((256, 256)), _const_spec((256, 256)),
                  _const_spec((1, PA_NORM_W)), _const_spec((1, 512)),
                  _const_spec((VT_ROWS - HEAD_DIM_A, TM)), _const_spec((TM, AUG_LANES))],
        out_specs=[tr(WIDTH_A), row(LANES), tr(KV_HEADS_A * VT_ROWS), row(PB_W),
                   tr(256), row(DIFF_HEADS * LANES), tr(DIFF_HEADS * VT_ROWS), row(DT_W),
                   pl.BlockSpec((2 * SSD_HEADS, TM), lambda i: (0, i))],
        out_shape=[jax.ShapeDtypeStruct((batch, WIDTH_A, seq), BF16), jax.ShapeDtypeStruct((t, LANES), BF16),
                   jax.ShapeDtypeStruct((batch, KV_HEADS_A * VT_ROWS, seq), BF16),
                   jax.ShapeDtypeStruct((t, PB_W), BF16),
                   jax.ShapeDtypeStruct((batch, 256, seq), BF16),
                   jax.ShapeDtypeStruct((t, DIFF_HEADS * LANES), BF16),
                   jax.ShapeDtypeStruct((batch, DIFF_HEADS * VT_ROWS, seq), BF16),
                   jax.ShapeDtypeStruct((t, DT_W), F32), jax.ShapeDtypeStruct((2 * SSD_HEADS, t), F32)],
        compiler_params=_cparams(("parallel",)),
        name="norm_proj",
    )(x, g, w_cat, _block_diag(256, HEAD_DIM_A), _block_diag(256, DIFF_QK_DIM), ga, gc, pad,
      jnp.asarray(cols, dtype=BF16))


QB_A = 512


GQA = N_HEADS_A // KV_HEADS_A
WIN_A = 3 * ATT_BLOCK
NQ_A = GQA * ATT_BLOCK


def _attn_a_kernel(qt_ref, k_ref, kp_ref, kn_ref, vt_ref, vtp_ref, vtn_ref, bias_ref, sink_ref, o_ref,
                   k_s, vt_s, rhs_s, *, n_blocks):
    i = pl.program_id(1)
    nsub = QB_A // ATT_BLOCK
    b = ATT_BLOCK
    k_s[0:b] = kp_ref[...]
    k_s[b:b + QB_A] = k_ref[...]
    k_s[b + QB_A:] = kn_ref[...]
    vt_s[:, 0:b] = vtp_ref[...]
    vt_s[:, b:b + QB_A] = vt_ref[...]
    vt_s[:, b + QB_A:] = vtn_ref[...]
    units = [(t, j) for t in range(nsub) for j in range(KV_HEADS_A)]

    def scores(u):
        t, j = units[u]
        other = 1 - j
        rhs_s[u, other * HEAD_DIM_A:(other + 1) * HEAD_DIM_A, :] = jnp.zeros((HEAD_DIM_A, NQ_A), BF16)
        for g in range(GQA):
            h = GQA * j + g
            rhs_s[u, j * HEAD_DIM_A:(j + 1) * HEAD_DIM_A, g * b:(g + 1) * b] = (
                qt_ref[h * HEAD_DIM_A:(h + 1) * HEAD_DIM_A, t * b:(t + 1) * b])
        gb = i * nsub + t
        var = jnp.where(gb == 0, 0, jnp.where(gb == n_blocks - 1, 2, 1))
        return _dot(k_s[t * b:t * b + WIN_A, :], rhs_s[u]) + bias_ref[var, j]

    s_next = scores(0)
    for u, (t, j) in enumerate(units):
        s = s_next
        if u + 1 < len(units):
            s_next = scores(u + 1)
        sk = sink_ref[j]
        m = jnp.maximum(jnp.max(s, axis=0, keepdims=True), sk)
        p = jnp.exp2(s - m)
        r = _dot(vt_s[j * VT_ROWS:(j + 1) * VT_ROWS, t * b:t * b + WIN_A], p.astype(BF16))
        den = r[HEAD_DIM_A:HEAD_DIM_A + 1] + jnp.exp2(sk - m)
        o = (r[0:HEAD_DIM_A] * (1.0 / den)).astype(BF16)
        for g in range(GQA):
            h = GQA * j + g
            o_ref[h * HEAD_DIM_A:(h + 1) * HEAD_DIM_A, t * b:(t + 1) * b] = o[:, g * b:(g + 1) * b]


def _attn_a_consts(sink):
    b = ATT_BLOCK
    qi = np.arange(b)[None, :]
    s = np.arange(WIN_A)[:, None]
    dist = np.abs(b + qi - s).astype(np.float32)
    in_win = dist <= WINDOW
    slopes = jnp.exp2(-8.0 * jnp.arange(1, N_HEADS_A + 1, dtype=F32) / N_HEADS_A) * LOG2E
    alibi = -slopes.reshape(KV_HEADS_A, 1, GQA, 1) * jnp.asarray(dist)[None, :, None, :]
    variants = []
    for valid in (s >= b, s >= 0, s < 2 * b):
        mask = jnp.asarray(in_win & valid)[None, :, None, :]
        variants.append(jnp.where(mask, alibi, NEG_BIG).reshape(KV_HEADS_A, WIN_A, NQ_A))
    bias = jnp.stack(variants)
    sink_cols = jnp.repeat(sink.astype(F32).reshape(KV_HEADS_A, GQA) * LOG2E, b, axis=1)[:, None, :]
    return bias, sink_cols


def _attn_a(qt, ka, vt, sink, batch, seq):
    nq = seq // QB_A
    nblk = seq // ATT_BLOCK
    per = QB_A // ATT_BLOCK
    bias, sink_cols = _attn_a_consts(sink)
    kcol = 0
    prev = lambda i: jnp.maximum(i * per - 1, 0)
    nxt = lambda i: jnp.minimum((i + 1) * per, nblk - 1)
    vrows = KV_HEADS_A * VT_ROWS
    return pl.pallas_call(
        functools.partial(_attn_a_kernel, n_blocks=nblk),
        grid=(batch, nq),
        in_specs=[
            pl.BlockSpec((None, WIDTH_A, QB_A), lambda b, i: (b, 0, i)),
            pl.BlockSpec((QB_A, LANES), lambda b, i: (b * nq + i, kcol)),
            pl.BlockSpec((ATT_BLOCK, LANES), lambda b, i: (b * nblk + prev(i), kcol)),
            pl.BlockSpec((ATT_BLOCK, LANES), lambda b, i: (b * nblk + nxt(i), kcol)),
            pl.BlockSpec((None, vrows, QB_A), lambda b, i: (b, 0, i)),
            pl.BlockSpec((None, vrows, ATT_BLOCK), lambda b, i: (b, 0, prev(i))),
            pl.BlockSpec((None, vrows, ATT_BLOCK), lambda b, i: (b, 0, nxt(i))),
            _const_spec((3, KV_HEADS_A, WIN_A, NQ_A)),
            _const_spec((KV_HEADS_A, 1, NQ_A)),
        ],
        out_specs=pl.BlockSpec((None, WIDTH_A, QB_A), lambda b, i: (b, 0, i)),
        out_shape=jax.ShapeDtypeStruct((batch, WIDTH_A, seq), BF16),
        scratch_shapes=[pltpu.VMEM((QB_A + 2 * ATT_BLOCK, LANES), BF16),
                        pltpu.VMEM((vrows, QB_A + 2 * ATT_BLOCK), BF16),
                        pltpu.VMEM((per * KV_HEADS_A, LANES, NQ_A), BF16)],
        compiler_params=_cparams(("parallel", "parallel")),
        name="attn_a",
    )(qt, ka, ka, ka, vt, vt, vt, bias, sink_cols)


RB_S = 512
HALO = 16


def _ssd_kernel(*refs, rev, n_steps):
    if rev:
        (cur_ref, prev_ref, next_ref, dt_ref, dtt_ref, cw_ref, cb_ref, dtb_c_ref, dtb_r_ref,
         alog_c_ref, alog_r_ref, ltri_ref, utri_ref, dskip_ref, g_ref, yf_ref, out_ref, st_ref) = refs
    else:
        (cur_ref, prev_ref, next_ref, dt_ref, dtt_ref, cw_ref, cb_ref, dtb_c_ref, dtb_r_ref,
         alog_c_ref, alog_r_ref, ltri_ref, utri_ref, out_ref, st_ref) = refs
    i = pl.program_id(1)
    ii = (n_steps - 1 - i) if rev else i
    d = 1 if rev else 0

    @pl.when(i == 0)
    def _():
        st_ref[...] = jnp.zeros_like(st_ref)

    xc = cur_ref[:, 0:CONV_DIM].astype(F32)
    prow = jnp.where(ii == 0, 0.0, prev_ref[...].astype(F32)[HALO - 1:HALO, :])
    nrow = jnp.where(ii == n_steps - 1, 0.0, next_ref[...].astype(F32)[0:1, :])
    rid = lax.broadcasted_iota(jnp.int32, (RB_S, CONV_DIM), 0)
    xm1 = jnp.where(rid == 0, prow, pltpu.roll(xc, 1, 0))
    xp1 = jnp.where(rid == RB_S - 1, nrow, pltpu.roll(xc, RB_S - 1, 0))
    u = xm1 * cw_ref[0:1, :] + xc * cw_ref[1:2, :] + xp1 * cw_ref[2:3, :] + cb_ref[...]
    u = u * _sigmoid(u)

    dt_c = _softplus(dt_ref[:, 0:2 * SSD_HEADS] + dtb_c_ref[...])
    dt_r = _softplus(dtt_ref[...] + dtb_r_ref[...])
    da_c = dt_c * (-jnp.exp(alog_c_ref[...]))
    da_r = dt_r * (-jnp.exp(alog_r_ref[...]))

    lane = lax.broadcasted_iota(jnp.int32, (CHUNK, LANES), 1)
    lo = lane < SSD_HEAD_DIM
    lo_row = lax.broadcasted_iota(jnp.int32, (1, LANES), 1) < SSD_HEAD_DIM
    li = lax.broadcasted_iota(jnp.int32, (CHUNK, CHUNK), 0)
    si = lax.broadcasted_iota(jnp.int32, (CHUNK, CHUNK), 1)
    tri = (si >= li) if rev else (si <= li)

    n_chunks = RB_S // CHUNK
    order = range(n_chunks - 1, -1, -1) if rev else range(n_chunks)
    for c in order:
        r0 = c * CHUNK
        dac = da_c[r0:r0 + CHUNK]
        dar = da_r[:, r0:r0 + CHUNK]
        cs_c = _dot_hi(ltri_ref[...], dac)
        cs_r = _dot_hi(dar, utri_ref[...])
        tot = cs_c[CHUNK - 1:CHUNK, :]
        if rev:
            e_c = cs_c - dac
            e_r = cs_r - dar
            w_b = jnp.exp(e_c)
            w_c = jnp.exp(tot - e_c)
        else:
            e_c = cs_c
            e_r = cs_r
            w_b = jnp.exp(tot - cs_c)
            w_c = jnp.exp(cs_c)
        dec = jnp.exp(tot)
        dtc = dt_c[r0:r0 + CHUNK]
        uc = u[r0:r0 + CHUNK]
        bm = uc[:, D_INNER:D_INNER + LANES]
        cm = uc[:, D_INNER + LANES:D_INNER + 2 * LANES]
        for pr in range(SSD_GROUPS):
            gmask = lo if pr == 0 else jnp.logical_not(lo)
            cg = jnp.where(gmask, cm, 0.0).astype(BF16)
            bg = jnp.where(gmask, bm, 0.0).astype(BF16)
            gmat = _dot_nt(cg, bg)
            xp = uc[:, pr * LANES:(pr + 1) * LANES]
            l0 = d * SSD_HEADS + 2 * pr
            l1 = l0 + 1
            pair = lambda a: jnp.where(lo, a[:, l0:l0 + 1], a[:, l1:l1 + 1])
            xdt = xp * pair(dtc)
            y = jnp.zeros((CHUNK, LANES), F32)
            for hh, dl in ((0, l0), (1, l1)):
                ecol = e_c[:, dl:dl + 1]
                erow = e_r[dl:dl + 1, :]
                diff = (erow - ecol) if rev else (ecol - erow)
                dmat = jnp.where(tri, jnp.exp(jnp.minimum(diff, 0.0)), 0.0)
                hmask = lo if hh == 0 else jnp.logical_not(lo)
                y = y + _dot((gmat * dmat).astype(BF16), jnp.where(hmask, xdt, 0.0).astype(BF16))
            st = st_ref[pr]
            y = y + pair(w_c) * _dot(cg, st.astype(BF16))
            s_new = _dot_tn(bg, (xdt * pair(w_b)).astype(BF16))
            dec_pair = jnp.where(lo_row, dec[:, l0:l0 + 1], dec[:, l1:l1 + 1])
            st_ref[pr] = st * dec_pair + s_new
            if rev:
                cols = slice(pr * LANES, (pr + 1) * LANES)
                y = y + yf_ref[r0:r0 + CHUNK, cols] + xp * dskip_ref[:, cols]
            out_ref[r0:r0 + CHUNK, pr * LANES:(pr + 1) * LANES] = y

    if rev:
        z = cur_ref[:, CONV_DIM:PB_W].astype(F32)
        yz = out_ref[...] * (z * _sigmoid(z))
        ms = jnp.mean(yz * yz, axis=-1, keepdims=True)
        out_ref[...] = yz * lax.rsqrt(ms + EPS) * g_ref[...]


def _ssd(pb, dt, dtt, params, yf, batch, seq):
    rev = yf is not None
    t = batch * seq
    ns = seq // RB_S
    per = RB_S // HALO
    nh = seq // HALO
    pos = (lambda i: ns - 1 - i) if rev else (lambda i: i)
    ltri = jnp.asarray(np.tril(np.ones((CHUNK, CHUNK), np.float32)))
    in_specs = [
        pl.BlockSpec((RB_S, PB_W), lambda b, i: (b * ns + pos(i), 0)),
        pl.BlockSpec((HALO, CONV_DIM), lambda b, i: (b * nh + jnp.maximum(pos(i) * per - 1, 0), 0)),
        pl.BlockSpec((HALO, CONV_DIM), lambda b, i: (b * nh + jnp.minimum((pos(i) + 1) * per, nh - 1), 0)),
        pl.BlockSpec((RB_S, DT_W), lambda b, i: (b * ns + pos(i), 0)),
        pl.BlockSpec((2 * SSD_HEADS, RB_S), lambda b, i: (0, b * ns + pos(i))),
        _const_spec((CONV_K, CONV_DIM)), _const_spec((1, CONV_DIM)),
        _const_spec((1, 2 * SSD_HEADS)), _const_spec((2 * SSD_HEADS, 1)),
        _const_spec((1, 2 * SSD_HEADS)), _const_spec((2 * SSD_HEADS, 1)),
        _const_spec((CHUNK, CHUNK)), _const_spec((CHUNK, CHUNK)),
    ]
    args = [pb, pb, pb, dt, dtt, params["conv_w"], params["conv_b"], params["dtb_c"], params["dtb_r"],
            params["alog_c"], params["alog_r"], ltri, ltri.T]
    if rev:
        in_specs += [_const_spec((1, D_INNER)), _const_spec((1, D_INNER)),
                     pl.BlockSpec((RB_S, D_INNER), lambda b, i: (b * ns + pos(i), 0))]
        args += [params["dskip"], params["norm_g"], yf]
    return pl.pallas_call(
        functools.partial(_ssd_kernel, rev=rev, n_steps=ns),
        grid=(batch, ns),
        in_specs=in_specs,
        out_specs=pl.BlockSpec((RB_S, D_INNER), lambda b, i: (b * ns + pos(i), 0)),
        out_shape=jax.ShapeDtypeStruct((t, D_INNER), F32),
        scratch_shapes=[pltpu.VMEM((SSD_GROUPS, LANES, LANES), F32)],
        compiler_params=_cparams(("parallel", "arbitrary")),
        name="ssd_bwd" if rev else "ssd_fwd",
    )(*args)


TQ_C = 512
TK_C = 512
N_MAPS = 4
KV_SUB = 8


def _attn_c_kernel(slope_ref, lam_ref, qt_ref, k_ref, vt_ref, bias_ref, g_ref, o_ref, qz_s, m_s, acc_s):
    hp = pl.program_id(1)
    qi = pl.program_id(2)
    ki = pl.program_id(3)

    @pl.when(ki == 0)
    def _():
        m_s[...] = jnp.full_like(m_s, NEG_BIG)
        acc_s[...] = jnp.zeros_like(acc_s)
        qz_s[...] = jnp.zeros_like(qz_s)
        for idx in range(N_MAPS):
            c = idx % 2
            qz_s[idx, c * DIFF_QK_DIM:(c + 1) * DIFF_QK_DIM, :] = qt_ref[idx * DIFF_QK_DIM:(idx + 1) * DIFF_QK_DIM, :]

    units = [(s, hh, c) for s in range(KV_SUB) for hh in range(2) for c in range(2)]

    def scores(u):
        s, hh, c = units[u]
        kt = ki * KV_SUB + s
        var = jnp.where(kt < qi, 0, jnp.where(kt == qi, 1, 2))
        return (_dot(k_ref[s * TK_C:(s + 1) * TK_C, hh * LANES:(hh + 1) * LANES], qz_s[2 * hh + c])
                + bias_ref[var, hh])

    t_next = scores(0)
    for u, (s, hh, c) in enumerate(units):
        t = t_next
        if u + 1 < len(units):
            t_next = scores(u + 1)
        idx = 2 * hh + c
        kt = ki * KV_SUB + s
        c_tile = -jnp.abs(qi * TQ_C - kt * TK_C).astype(F32) * slope_ref[2 * hp + hh]
        m_old = m_s[idx]
        m_new = jnp.maximum(m_old, jnp.max(t, axis=0, keepdims=True) + c_tile)
        alpha = jnp.exp2(m_old - m_new)
        p = jnp.exp2(t - (m_new - c_tile))
        vt = vt_ref[hh * VT_ROWS:(hh + 1) * VT_ROWS, s * TK_C:(s + 1) * TK_C]
        acc_s[idx] = alpha * acc_s[idx] + _dot(vt, p.astype(BF16))
        m_s[idx] = m_new

    @pl.when(ki == pl.num_programs(3) - 1)
    def _():
        _attn_c_finalize(lam_ref, g_ref, o_ref, acc_s)


def _attn_c_finalize(lam_ref, g_ref, o_ref, acc_s):
    lam = lam_ref[0]
    for hh in range(2):
        a0 = acc_s[2 * hh]
        a1 = acc_s[2 * hh + 1]
        o = (a0[0:DIFF_V_DIM] * (1.0 / a0[DIFF_V_DIM:DIFF_V_DIM + 1])
             - lam * (a1[0:DIFF_V_DIM] * (1.0 / a1[DIFF_V_DIM:DIFF_V_DIM + 1])))
        ms = jnp.mean(o * o, axis=0, keepdims=True)
        rows = slice(hh * DIFF_V_DIM, (hh + 1) * DIFF_V_DIM)
        o_ref[rows, :] = (o * lax.rsqrt(ms + EPS) * g_ref[rows, :]).astype(BF16)


AUG_ROWS = 16
AUG_LANES = 64
BOUND_SLACK = 1.02
BOUND_LIMIT = 100.0


def _split3(x):
    hi = x.astype(BF16).astype(F32)
    r = x - hi
    mid = r.astype(BF16).astype(F32)
    return hi, mid, (r - mid).astype(BF16).astype(F32)


def _attn_c_bounded_kernel(slope_ref, lam_ref, kmax_ref, qt_ref, k_ref, vt_ref, dtab_ref, g_ref, o_ref,
                           rhs_s, mref_s, acc_s):
    hp = pl.program_id(1)
    qi = pl.program_id(2)
    ki = pl.program_id(3)
    units = [(s, hh, c) for s in range(KV_SUB) for hh in range(2) for c in range(2)]

    @pl.when((pl.program_id(0) == 0) & (hp == 0) & (qi == 0) & (ki == 0))
    def _():
        rhs_s[...] = jnp.zeros_like(rhs_s)

    @pl.when(ki == 0)
    def _():
        acc_s[...] = jnp.zeros_like(acc_s)
        for u, (s, hh, c) in enumerate(units):
            idx = 2 * hh + c
            qseg = qt_ref[idx * DIFF_QK_DIM:(idx + 1) * DIFF_QK_DIM, :]
            rhs_s[u, c * DIFF_QK_DIM:(c + 1) * DIFF_QK_DIM, :] = qseg
            if s == 0:
                qf = qseg.astype(F32)
                kmax = kmax_ref[0] * BOUND_SLACK
                mref_s[idx] = jnp.sqrt(jnp.sum(qf * qf, axis=0, keepdims=True)) * kmax

    qloc = lax.broadcasted_iota(jnp.int32, (1, TQ_C), 1).astype(F32)
    row = lax.broadcasted_iota(jnp.int32, (AUG_ROWS, TQ_C), 0)
    grp = row // 3
    part = row - 3 * grp

    def scores(u):
        s, hh, c = units[u]
        kt = ki * KV_SUB + s
        h = 2 * hp + hh
        sgn = jnp.where(kt < qi, 1.0, jnp.where(kt == qi, 0.0, -1.0)).astype(F32)
        slope = slope_ref[4 * h + 3]
        tile_dist = jnp.abs(qi * TQ_C - kt * TK_C).astype(F32)
        shift = mref_s[2 * hh + c] + slope * tile_dist + (sgn * slope) * qloc
        hi, mid, lo = _split3(shift)
        shift_part = jnp.where(part == 0, hi, jnp.where(part == 1, mid, lo))
        slope_part = sgn * jnp.where(part == 0, slope_ref[4 * h], jnp.where(part == 1, slope_ref[4 * h + 1],
                                                                           slope_ref[4 * h + 2]))
        aug = jnp.where(grp == 0, -shift_part, jnp.where(grp <= 2, slope_part, 0.0))
        rhs_s[u, 2 * DIFF_QK_DIM:2 * DIFF_QK_DIM + AUG_ROWS, :] = aug.astype(BF16)
        on_diag = (kt == qi).astype(jnp.int32)
        return _dot(k_ref[s * TK_C:(s + 1) * TK_C, hh * LANES:(hh + 1) * LANES], rhs_s[u]) + dtab_ref[on_diag, hh]

    t_next = scores(0)
    for u, (s, hh, c) in enumerate(units):
        t = t_next
        if u + 1 < len(units):
            t_next = scores(u + 1)
        idx = 2 * hh + c
        vt = vt_ref[hh * VT_ROWS:(hh + 1) * VT_ROWS, s * TK_C:(s + 1) * TK_C]
        acc_s[idx] = acc_s[idx] + _dot(vt, jnp.exp2(t).astype(BF16))

    @pl.when(ki == pl.num_programs(3) - 1)
    def _():
        _attn_c_finalize(lam_ref, g_ref, o_ref, acc_s)


def _attn_c_bias(slopes):
    rel = (np.arange(TQ_C)[None, :] - np.arange(TK_C)[:, None]).astype(np.float32)
    tables = jnp.asarray(np.stack([-rel, -np.abs(rel), rel]))
    s = slopes.reshape(DIFF_HEADS // 2, 1, 2, 1, 1)
    return s * tables[None, :, None]


def _attn_c_call(body, prefetch, tensors, table_variants, scratch, batch, seq, name):
    nq = seq // TQ_C
    kv_step = TK_C * KV_SUB
    nk = seq // kv_step
    n_pre = len(prefetch)
    grid_spec = pltpu.PrefetchScalarGridSpec(
        num_scalar_prefetch=n_pre,
        grid=(batch, DIFF_HEADS // 2, nq, nk),
        in_specs=[
            pl.BlockSpec((None, LANES, TQ_C), lambda b, h, i, j, *_: (b, h, i)),
            pl.BlockSpec((kv_step, 2 * LANES), lambda b, h, i, j, *_: (b * nk + j, h)),
            pl.BlockSpec((None, 2 * VT_ROWS, kv_step), lambda b, h, i, j, *_: (b, h, j)),
            pl.BlockSpec((None, table_variants, 2, TK_C, TQ_C), lambda b, h, i, j, *_: (h, 0, 0, 0, 0)),
            _const_spec((LANES, TQ_C)),
        ],
        out_specs=pl.BlockSpec((None, LANES, TQ_C), lambda b, h, i, j, *_: (b, h, i)),
        scratch_shapes=scratch + [pltpu.VMEM((N_MAPS, 1, TQ_C), F32), pltpu.VMEM((N_MAPS, VT_ROWS, TQ_C), F32)],
    )
    return pl.pallas_call(
        body,
        grid_spec=grid_spec,
        out_shape=jax.ShapeDtypeStruct((batch, WIDTH_C, seq), BF16),
        compiler_params=_cparams(("arbitrary", "arbitrary", "arbitrary", "arbitrary")),
        name=name,
    )(*prefetch, *tensors)


def _attn_c(qt, k_aug, vt, qk_gains, slopes, lam, gain, batch, seq):
    kv_step = TK_C * KV_SUB
    assert seq % kv_step == 0 and seq % TQ_C == 0
    gain_b = jnp.broadcast_to(gain.reshape(LANES, 1), (LANES, TQ_C))
    tables = _attn_c_bias(slopes)
    seg_bound = lambda g: math.sqrt(DIFF_QK_DIM) * jnp.max(jnp.abs(g))
    qmax = seg_bound(qk_gains[:, 0:256])
    kmax = seg_bound(qk_gains[:, 256:512])

    def online():
        scratch = [pltpu.VMEM((N_MAPS, LANES, TQ_C), BF16)]
        return _attn_c_call(_attn_c_kernel, (slopes, lam), (qt, k_aug, vt, tables, gain_b), 3, scratch,
                            batch, seq, "attn_c_online")

    def bounded():
        slope4 = jnp.stack(_split3(slopes) + (slopes,), axis=1).reshape(-1)
        dtab = jnp.stack([jnp.zeros_like(tables[:, 1]), tables[:, 1]], axis=1)
        scratch = [pltpu.VMEM((N_MAPS * KV_SUB, LANES, TQ_C), BF16)]
        return _attn_c_call(_attn_c_bounded_kernel, (slope4, lam, jnp.reshape(kmax, (1,))),
                            (qt, k_aug, vt, dtab, gain_b), 2, scratch, batch, seq, "attn_c_bounded")

    safe = 2.0 * BOUND_SLACK * BOUND_SLACK * qmax * kmax < BOUND_LIMIT
    return lax.cond(safe, bounded, online)


def _out_proj(oat_ref, ob_ref, oct_ref, x_ref, wo_ref):
    untr = lambda ref: ref[...].astype(F32).T.astype(BF16)
    o = jnp.concatenate([untr(oat_ref), ob_ref[...].astype(BF16), untr(oct_ref)], axis=-1)
    return x_ref[...] + _dot(o, wo_ref[...])


def _rms(x, g_ref):
    ms = jnp.mean(x * x, axis=-1, keepdims=True)
    return x * lax.rsqrt(ms + EPS) * g_ref[...]


def _swiglu_acc(h, wg_ref, wu_ref, wd_ref, acc, d_ff, chunks):
    step = d_ff // chunks
    for c0 in range(0, d_ff, step):
        gate = _dot(h, wg_ref[:, c0:c0 + step])
        up = _dot(h, wu_ref[:, c0:c0 + step])
        act = (gate * _sigmoid(gate) * up).astype(BF16)
        acc = acc + _dot(act, wd_ref[c0:c0 + step, :])
    return acc


def _out_ffn_kernel(oa_ref, ob_ref, oc_ref, x_ref, wo_ref, g_ref, wg_ref, wu_ref, wd_ref, out_ref):
    x1 = _out_proj(oa_ref, ob_ref, oc_ref, x_ref, wo_ref)
    h = _rms(x1, g_ref).astype(BF16)
    out_ref[...] = _swiglu_acc(h, wg_ref, wu_ref, wd_ref, x1, D_FF, FFN_CHUNKS)


def _mixer_out_specs(seq):
    nps = seq // TM
    row = lambda w: pl.BlockSpec((TM, w), lambda i: (i, 0))
    tr = lambda r: pl.BlockSpec((None, r, TM), lambda i: (i // nps, 0, i % nps))
    return [tr(WIDTH_A), row(D_INNER), tr(WIDTH_C), row(D_MODEL), _resident_spec((D_MODEL, D_MODEL)),
            _const_spec((1, D_MODEL))]


def _out_ffn(oa, ob, oc, x, w_out, g, wg, wu, wd):
    t = x.shape[0]
    return pl.pallas_call(
        _out_ffn_kernel,
        grid=(t // TM,),
        in_specs=_mixer_out_specs(oa.shape[2]) + [_resident_spec((D_MODEL, D_FF)), _resident_spec((D_MODEL, D_FF)),
                                       _resident_spec((D_FF, D_MODEL))],
        out_specs=pl.BlockSpec((TM, D_MODEL), lambda i: (i, 0)),
        out_shape=jax.ShapeDtypeStruct((t, D_MODEL), F32),
        compiler_params=_cparams(("parallel",)),
        name="out_ffn",
    )(oa, ob, oc, x, w_out, g, wg, wu, wd)


def _out_router_kernel(oa_ref, ob_ref, oc_ref, x_ref, wo_ref, g_ref, wr_hi_ref, wr_lo_ref,
                       x1_ref, h_ref, sel_ref, gate_ref):
    x1 = _out_proj(oa_ref, ob_ref, oc_ref, x_ref, wo_ref)
    x1_ref[...] = x1
    h = _rms(x1, g_ref)
    h_hi = h.astype(BF16)
    h_ref[...] = h_hi
    h_lo = (h - h_hi.astype(F32)).astype(BF16)
    logits = _dot(h_hi, wr_hi_ref[...]) + (_dot(h_lo, wr_hi_ref[...]) + _dot(h_hi, wr_lo_ref[...]))
    lt = logits.T[0:N_EXPERTS, :]
    row_i = lax.broadcasted_iota(jnp.int32, (N_EXPERTS, TM), 0)
    row = row_i.astype(F32)
    v1 = jnp.max(lt, axis=0, keepdims=True)
    i1 = jnp.min(jnp.where(lt == v1, row, float(N_EXPERTS)), axis=0, keepdims=True)
    rest = jnp.where(row == i1, -jnp.inf, lt)
    v2 = jnp.max(rest, axis=0, keepdims=True)
    i2 = jnp.min(jnp.where(rest == v2, row, float(N_EXPERTS)), axis=0, keepdims=True)
    e = jnp.exp(v2 - v1)
    g1 = 1.0 / (1.0 + e)
    sel_ref[...] = jnp.where(row_i == 0, i1, jnp.where(row_i == 1, i2, 0.0)).astype(jnp.int32)
    gate_ref[...] = jnp.where(row_i == 0, g1, jnp.where(row_i == 1, e * g1, 0.0))


def _out_router(oa, ob, oc, x, w_out, g, wr_hi, wr_lo):
    t = x.shape[0]
    row = lambda w: pl.BlockSpec((TM, w), lambda i: (i, 0))
    col = pl.BlockSpec((N_EXPERTS, TM), lambda i: (0, i))
    return pl.pallas_call(
        _out_router_kernel,
        grid=(t // TM,),
        in_specs=_mixer_out_specs(oa.shape[2]) + [_const_spec((D_MODEL, LANES)), _const_spec((D_MODEL, LANES))],
        out_specs=[row(D_MODEL), row(D_MODEL), col, col],
        out_shape=[jax.ShapeDtypeStruct((t, D_MODEL), F32), jax.ShapeDtypeStruct((t, D_MODEL), BF16),
                   jax.ShapeDtypeStruct((N_EXPERTS, t), jnp.int32), jax.ShapeDtypeStruct((N_EXPERTS, t), F32)],
        compiler_params=_cparams(("parallel",)),
        name="out_router",
    )(oa, ob, oc, x, w_out, g, wr_hi, wr_lo)


def _moe_ffn_kernel(blk_exp_ref, n_used_ref, x_ref, wg_ref, wu_ref, wd_ref, out_ref):
    i = pl.program_id(0)

    @pl.when(i < n_used_ref[0])
    def _():
        acc = jnp.zeros((MOE_BLK, D_MODEL), F32)
        out_ref[...] = _swiglu_acc(x_ref[...], wg_ref.at[0], wu_ref.at[0], wd_ref.at[0], acc,
                                   D_FF_EXPERT, MOE_CHUNKS).astype(BF16)

    @pl.when(i >= n_used_ref[0])
    def _():
        out_ref[...] = jnp.zeros_like(out_ref)


def _moe_ffn(xg, blk_exp, n_used, wg, wu, wd):
    rows = xg.shape[0]
    wspec = lambda shape: pl.BlockSpec((1,) + shape, lambda i, be, nu: (be[i], 0, 0),
                                       pipeline_mode=pl.Buffered(1))
    grid_spec = pltpu.PrefetchScalarGridSpec(
        num_scalar_prefetch=2,
        grid=(rows // MOE_BLK,),
        in_specs=[pl.BlockSpec((MOE_BLK, D_MODEL), lambda i, be, nu: (i, 0)),
                  wspec((D_MODEL, D_FF_EXPERT)), wspec((D_MODEL, D_FF_EXPERT)),
                  wspec((D_FF_EXPERT, D_MODEL))],
        out_specs=pl.BlockSpec((MOE_BLK, D_MODEL), lambda i, be, nu: (i, 0)),
    )
    return pl.pallas_call(
        _moe_ffn_kernel,
        grid_spec=grid_spec,
        out_shape=jax.ShapeDtypeStruct((rows, D_MODEL), BF16),
        compiler_params=_cparams(("arbitrary",)),
        name="moe_ffn",
    )(blk_exp, n_used, xg, wg, wu, wd)


def _moe(x1, h, sel, gates, wg, wu, wd):
    t = x1.shape[0]
    i1, i2 = sel[0], sel[1]
    g1, g2 = gates[0][:, None], gates[1][:, None]
    experts = jnp.arange(N_EXPERTS, dtype=jnp.int32)
    onehot = ((i1[:, None] == experts) | (i2[:, None] == experts)).astype(jnp.int32)
    rank = jnp.cumsum(onehot, axis=0) - onehot
    counts = jnp.sum(onehot, axis=0)
    padded = ((counts + MOE_BLK - 1) // MOE_BLK) * MOE_BLK
    pad_end = jnp.cumsum(padded)
    pad_start = pad_end - padded
    dest = pad_start[None, :] + rank
    d1 = jnp.take_along_axis(dest, i1[:, None], axis=1)[:, 0]
    d2 = jnp.take_along_axis(dest, i2[:, None], axis=1)[:, 0]
    n_blocks = (2 * t) // MOE_BLK + N_EXPERTS
    rows = n_blocks * MOE_BLK
    tok = jnp.arange(t, dtype=jnp.int32)
    row_tok = jnp.zeros((rows,), jnp.int32).at[jnp.concatenate([d1, d2])].set(jnp.concatenate([tok, tok]))
    blk_start = jnp.arange(n_blocks, dtype=jnp.int32) * MOE_BLK
    blk_exp = jnp.minimum(jnp.sum((pad_end[None, :] <= blk_start[:, None]).astype(jnp.int32), axis=1),
                          N_EXPERTS - 1)
    n_used = (pad_end[-1:] // MOE_BLK).astype(jnp.int32)
    yb = _moe_ffn(h[row_tok], blk_exp, n_used, wg, wu, wd)
    return x1 + g1 * yb[d1].astype(F32) + g2 * yb[d2].astype(F32)


def _split_cols(w):
    outs, start = [], 0
    for n in SPLIT_SIZES:
        outs.append(w[:, start:start + n])
        start += n
    return outs


def _layer_params(l, p):
    qa, ka, va, zb, xbc, dtb, qc, kc, vc = _split_cols(p["w_in"][l])
    w_cat = jnp.concatenate(
        [qa, ka, va, xbc, zb, qc, kc, vc, dtb, jnp.zeros((D_MODEL, DT_W - 2 * SSD_HEADS), F32)],
        axis=1).astype(BF16)
    ga = jnp.concatenate([jnp.tile(p["qnorm_a"][l] * (HEAD_DIM_A ** -0.5 * LOG2E), N_HEADS_A),
                          jnp.tile(p["knorm_a"][l], KV_HEADS_A)])[None, :]
    gc = jnp.concatenate([jnp.tile(p["qnorm_c"][l] * (DIFF_QK_DIM ** -0.5 * LOG2E), 2 * DIFF_HEADS),
                          jnp.tile(p["knorm_c"][l], 2 * DIFF_HEADS)])[None, :]
    lam_init = 0.8 - 0.6 * math.exp(-0.3 * l)
    lc = p["lam_c"][l].astype(F32)
    lam = jnp.exp(jnp.sum(lc[0] * lc[1])) - jnp.exp(jnp.sum(lc[2] * lc[3])) + lam_init
    out = {
        "norm_mix_g": p["norm_mix_g"][l][None, :], "w_cat": w_cat, "ga": ga, "gc": gc,
        "sink": p["sink_a"][l],
        "ssd": {
            "conv_w": p["conv_w"][l], "conv_b": p["conv_b"][l][None, :],
            "dtb_c": p["dt_bias"][l].reshape(1, -1), "dtb_r": p["dt_bias"][l].reshape(-1, 1),
            "alog_c": p["a_log"][l].reshape(1, -1), "alog_r": p["a_log"][l].reshape(-1, 1),
            "dskip": jnp.repeat(p["d_skip"][l], SSD_HEAD_DIM)[None, :],
            "norm_g": p["ssd_norm_g"][l][None, :],
        },
        "slopes_c": jnp.exp2(-8.0 * jnp.arange(1, DIFF_HEADS + 1, dtype=F32) / DIFF_HEADS) * LOG2E,
        "lam": jnp.reshape(lam, (1,)).astype(F32),
        "gain_c": (jnp.tile(p["diff_norm_g"][l], 2) * (1.0 - lam_init))[None, :],
        "w_out": p["w_out"][l].astype(BF16),
        "norm_ffn_g": p["norm_ffn_g"][l][None, :],
    }
    i = l // 2
    if l % 2 == 0:
        out["ffn"] = (p["ffn_w_gate"][i].astype(BF16), p["ffn_w_up"][i].astype(BF16),
                      p["ffn_w_down"][i].astype(BF16))
    else:
        wr = jnp.pad(p["router_w"][i], ((0, 0), (0, LANES - N_EXPERTS)))
        wr_hi = wr.astype(BF16)
        out["router"] = (wr_hi, (wr - wr_hi.astype(F32)).astype(BF16))
        out["moe"] = (p["moe_w_gate"][i].astype(BF16), p["moe_w_up"][i].astype(BF16),
                      p["moe_w_down"][i].astype(BF16))
    return out


def _trunk(x3, layers):
    batch, seq, _ = x3.shape
    x = x3.reshape(batch * seq, D_MODEL)
    for l, lp in enumerate(layers):
        qat, ka, vat, pb, qct, kc, vct, dt, dtt = _norm_proj(x, lp["norm_mix_g"], lp["w_cat"], lp["ga"], lp["gc"],
                                                             batch, seq)
        oa = _attn_a(qat, ka, vat, lp["sink"], batch, seq)
        yf = _ssd(pb, dt, dtt, lp["ssd"], None, batch, seq)
        ob = _ssd(pb, dt, dtt, lp["ssd"], yf, batch, seq)
        oc = _attn_c(qct, kc, vct, lp["gc"], lp["slopes_c"], lp["lam"], lp["gain_c"], batch, seq)
        if l % 2 == 0:
            x = _out_ffn(oa, ob, oc, x, lp["w_out"], lp["norm_ffn_g"], *lp["ffn"])
        else:
            x1, h, sel, gates = _out_router(oa, ob, oc, x, lp["w_out"], lp["norm_ffn_g"], *lp["router"])
            x = _moe(x1, h, sel, gates, *lp["moe"])
    return x.reshape(batch, seq, D_MODEL)


def kernel(x_prompt, x_sample, norm_mix_g, w_in, qnorm_a, knorm_a, sink_a, conv_w, conv_b, dt_bias, a_log,
           d_skip, ssd_norm_g, qnorm_c, knorm_c, lam_c, diff_norm_g, w_out, norm_ffn_g, ffn_w_gate, ffn_w_up,
           ffn_w_down, router_w, moe_w_gate, moe_w_up, moe_w_down):
    p = dict(norm_mix_g=norm_mix_g, w_in=w_in, qnorm_a=qnorm_a, knorm_a=knorm_a, sink_a=sink_a,
             conv_w=conv_w, conv_b=conv_b, dt_bias=dt_bias, a_log=a_log, d_skip=d_skip,
             ssd_norm_g=ssd_norm_g, qnorm_c=qnorm_c, knorm_c=knorm_c, lam_c=lam_c, diff_norm_g=diff_norm_g,
             w_out=w_out, norm_ffn_g=norm_ffn_g, ffn_w_gate=ffn_w_gate, ffn_w_up=ffn_w_up,
             ffn_w_down=ffn_w_down, router_w=router_w, moe_w_gate=moe_w_gate, moe_w_up=moe_w_up,
             moe_w_down=moe_w_down)
    layers = [_layer_params(l, p) for l in range(DEPTH)]
    return _trunk(x_prompt, layers), _trunk(x_sample, layers)
```

```python
import functools
import math

import numpy as np
import jax
import jax.numpy as jnp
from jax import lax
from jax.experimental import pallas as pl
from jax.experimental.pallas import tpu as pltpu

F32 = jnp.float32
BF16 = jnp.bfloat16

D_MODEL = 1024
DEPTH = 2
N_HEADS_A = 8
KV_HEADS_A = 2
HEAD_DIM_A = 64
WINDOW = 128
ATT_BLOCK = 128
SSD_HEADS = 4
SSD_HEAD_DIM = 64
D_INNER = SSD_HEADS * SSD_HEAD_DIM
SSD_GROUPS = 2
D_STATE = 64
CONV_K = 3
CHUNK = 128
DIFF_HEADS = 4
DIFF_QK_DIM = 32
DIFF_V_DIM = 64
WIDTH_A = N_HEADS_A * HEAD_DIM_A
WIDTH_C = DIFF_HEADS * DIFF_V_DIM
CONV_DIM = D_INNER + 2 * SSD_GROUPS * D_STATE
SPLIT_SIZES = (WIDTH_A, KV_HEADS_A * HEAD_DIM_A, KV_HEADS_A * HEAD_DIM_A,
               D_INNER, CONV_DIM, 2 * SSD_HEADS,
               DIFF_HEADS * 2 * DIFF_QK_DIM, DIFF_HEADS * 2 * DIFF_QK_DIM, WIDTH_C)
D_FF = 2816
N_EXPERTS = 8
D_FF_EXPERT = 3584
EPS = 1e-6

LOG2E = 1.4426950408889634
NEG_BIG = -1e30

LANES = 128
VMEM_LIMIT = 56 * 1024 * 1024

PA_W = 768
PA_NORM_W = 640
VT_ROWS = 80
PB_W = 768
PC_W = 768
DT_W = 128
PROJ_W = PA_W + PB_W + PC_W + DT_W

TM = 512
MOE_BLK = 512
FFN_CHUNKS = 1
MOE_CHUNKS = 2


def _cparams(sem, vmem=VMEM_LIMIT):
    return pltpu.CompilerParams(dimension_semantics=sem, vmem_limit_bytes=vmem)


def _const_spec(shape):
    nd = len(shape)
    return pl.BlockSpec(shape, lambda *_: (0,) * nd)


def _resident_spec(shape):
    nd = len(shape)
    return pl.BlockSpec(shape, lambda *_: (0,) * nd, pipeline_mode=pl.Buffered(1))


def _block_diag(n, seg):
    idx = np.arange(n) // seg
    return jnp.asarray((idx[:, None] == idx[None, :]).astype(np.float32), dtype=BF16)


def _sigmoid(x):
    return 1.0 / (1.0 + jnp.exp(-x))


def _softplus(x):
    return jnp.maximum(x, 0.0) + jnp.log(1.0 + jnp.exp(-jnp.abs(x)))


def _dot(a, b):
    return jnp.dot(a, b, preferred_element_type=F32)


def _dot_nt(a, b):
    return lax.dot_general(a, b, (((1,), (1,)), ((), ())), preferred_element_type=F32)


def _dot_tn(a, b):
    return lax.dot_general(a, b, (((0,), (0,)), ((), ())), preferred_element_type=F32)


def _dot_hi(a, b):
    return jnp.dot(a, b, preferred_element_type=F32, precision=lax.Precision.HIGHEST)


def _seg_norm(blk, seg, bd_ref, gain):
    w = blk.shape[1]
    ss = _dot((blk * blk).astype(BF16), bd_ref[0:w, 0:w]) * (1.0 / seg)
    return blk * lax.rsqrt(ss + EPS) * gain


def _store_vt(vt_ref, v, heads, head_dim, pad_ref):
    vt = v.T
    for h in range(heads):
        r0 = h * VT_ROWS
        vt_ref[r0:r0 + head_dim, :] = vt[h * head_dim:(h + 1) * head_dim].astype(BF16)
        vt_ref[r0 + head_dim:r0 + VT_ROWS, :] = pad_ref[...]


def _norm_proj_kernel(x_ref, g_ref, w_ref, bd64_ref, bd32_ref, ga_ref, gc_ref, pad_ref, kaug_ref,
                      qat_ref, ka_ref, vat_ref, pb_ref, qct_ref, kc_ref, vct_ref, dt_ref, dtt_ref):
    x = x_ref[...]
    ms = jnp.mean(x * x, axis=-1, keepdims=True)
    h = (x * lax.rsqrt(ms + EPS) * g_ref[...]).astype(BF16)
    ya = _dot(h, w_ref[:, 0:PA_W])
    for c0 in range(0, WIDTH_A, 256):
        qn = _seg_norm(ya[:, c0:c0 + 256], HEAD_DIM_A, bd64_ref, ga_ref[:, c0:c0 + 256])
        qat_ref[c0:c0 + 256, :] = qn.T.astype(BF16)
    ka_ref[...] = _seg_norm(ya[:, WIDTH_A:PA_NORM_W], HEAD_DIM_A, bd64_ref, ga_ref[:, WIDTH_A:PA_NORM_W]).astype(BF16)
    _store_vt(vat_ref, ya[:, PA_NORM_W:PA_W], KV_HEADS_A, HEAD_DIM_A, pad_ref)
    pb_ref[...] = _dot(h, w_ref[:, PA_W:PA_W + PB_W]).astype(BF16)
    yc = _dot(h, w_ref[:, PA_W + PB_W:PA_W + PB_W + PC_W])
    qct_ref[...] = _seg_norm(yc[:, 0:256], DIFF_QK_DIM, bd32_ref, gc_ref[:, 0:256]).T.astype(BF16)
    kc = _seg_norm(yc[:, 256:512], DIFF_QK_DIM, bd32_ref, gc_ref[:, 256:512]).astype(BF16)
    aug = kaug_ref[...]
    pieces = []
    for hd in range(DIFF_HEADS):
        pieces += [kc[:, hd * 2 * DIFF_QK_DIM:(hd + 1) * 2 * DIFF_QK_DIM], aug]
    kc_ref[...] = jnp.concatenate(pieces, axis=1)
    _store_vt(vct_ref, yc[:, 512:PC_W], DIFF_HEADS, DIFF_V_DIM, pad_ref)
    yd = _dot(h, w_ref[:, PA_W + PB_W + PC_W:PROJ_W])
    dt_ref[...] = yd
    dtt_ref[...] = yd.T[0:2 * SSD_HEADS, :]


def _norm_proj(x, g, w_cat, ga, gc, batch, seq):
    t = batch * seq
    nps = seq // TM
    row = lambda w: pl.BlockSpec((TM, w), lambda i: (i, 0))
    tr = lambda r: pl.BlockSpec((None, r, TM), lambda i: (i // nps, 0, i % nps))
    pad = jnp.zeros((VT_ROWS - HEAD_DIM_A, TM), BF16).at[0].set(1.0)
    pos = np.arange(TM) % TK_C
    cols = np.zeros((TM, AUG_LANES), np.float32)
    cols[:, 0:3] = 1.0
    cols[:, 3:6] = (pos // 256 * 256)[:, None]
    cols[:, 6:9] = (pos % 256)[:, None]
    return pl.pallas_call(
        _norm_proj_kernel,
        grid=(t // TM,),
        in_specs=[row(D_MODEL), _const_spec((1, D_MODEL)), _resident_spec((D_MODEL, PROJ_W)),
                  _const_spec((256, 256)), _const_spec((256, 256)),
                  _const_spec((1, PA_NORM_W)), _const_spec((1, 512)),
                  _const_spec((VT_ROWS - HEAD_DIM_A, TM)), _const_spec((TM, AUG_LANES))],
        out_specs=[tr(WIDTH_A), row(LANES), tr(KV_HEADS_A * VT_ROWS), row(PB_W),
                   tr(256), row(DIFF_HEADS * LANES), tr(DIFF_HEADS * VT_ROWS), row(DT_W),
                   pl.BlockSpec((2 * SSD_HEADS, TM), lambda i: (0, i))],
        out_shape=[jax.ShapeDtypeStruct((batch, WIDTH_A, seq), BF16), jax.ShapeDtypeStruct((t, LANES), BF16),
                   jax.ShapeDtypeStruct((batch, KV_HEADS_A * VT_ROWS, seq), BF16),
                   jax.ShapeDtypeStruct((t, PB_W), BF16),
                   jax.ShapeDtypeStruct((batch, 256, seq), BF16),
                   jax.ShapeDtypeStruct((t, DIFF_HEADS * LANES), BF16),
                   jax.ShapeDtypeStruct((batch, DIFF_HEADS * VT_ROWS, seq), BF16),
                   jax.ShapeDtypeStruct((t, DT_W), F32), jax.ShapeDtypeStruct((2 * SSD_HEADS, t), F32)],
        compiler_params=_cparams(("parallel",)),
        name="norm_proj",
    )(x, g, w_cat, _block_diag(256, HEAD_DIM_A), _block_diag(256, DIFF_QK_DIM), ga, gc, pad,
      jnp.asarray(cols, dtype=BF16))


QB_A = 1024


GQA = N_HEADS_A // KV_HEADS_A
WIN_A = 3 * ATT_BLOCK
NQ_A = GQA * ATT_BLOCK


def _attn_a_kernel(qt_ref, k_ref, kp_ref, kn_ref, vt_ref, vtp_ref, vtn_ref, bias_ref, sink_ref, o_ref,
                   k_s, vt_s, rhs_s, *, n_blocks):
    i = pl.program_id(1)
    nsub = QB_A // ATT_BLOCK
    b = ATT_BLOCK
    k_s[0:b] = kp_ref[...]
    k_s[b:b + QB_A] = k_ref[...]
    k_s[b + QB_A:] = kn_ref[...]
    vt_s[:, 0:b] = vtp_ref[...]
    vt_s[:, b:b + QB_A] = vt_ref[...]
    vt_s[:, b + QB_A:] = vtn_ref[...]
    units = [(t, j) for t in range(nsub) for j in range(KV_HEADS_A)]

    def scores(u):
        t, j = units[u]
        other = 1 - j
        rhs_s[u, other * HEAD_DIM_A:(other + 1) * HEAD_DIM_A, :] = jnp.zeros((HEAD_DIM_A, NQ_A), BF16)
        for g in range(GQA):
            h = GQA * j + g
            rhs_s[u, j * HEAD_DIM_A:(j + 1) * HEAD_DIM_A, g * b:(g + 1) * b] = (
                qt_ref[h * HEAD_DIM_A:(h + 1) * HEAD_DIM_A, t * b:(t + 1) * b])
        gb = i * nsub + t
        var = jnp.where(gb == 0, 0, jnp.where(gb == n_blocks - 1, 2, 1))
        return _dot(k_s[t * b:t * b + WIN_A, :], rhs_s[u]) + bias_ref[var, j]

    s_next = scores(0)
    for u, (t, j) in enumerate(units):
        s = s_next
        if u + 1 < len(units):
            s_next = scores(u + 1)
        sk = sink_ref[j]
        m = jnp.maximum(jnp.max(s, axis=0, keepdims=True), sk)
        p = jnp.exp2(s - m)
        r = _dot(vt_s[j * VT_ROWS:(j + 1) * VT_ROWS, t * b:t * b + WIN_A], p.astype(BF16))
        den = r[HEAD_DIM_A:HEAD_DIM_A + 1] + jnp.exp2(sk - m)
        o = (r[0:HEAD_DIM_A] * (1.0 / den)).astype(BF16)
        for g in range(GQA):
            h = GQA * j + g
            o_ref[h * HEAD_DIM_A:(h + 1) * HEAD_DIM_A, t * b:(t + 1) * b] = o[:, g * b:(g + 1) * b]


def _attn_a_consts(sink):
    b = ATT_BLOCK
    qi = np.arange(b)[None, :]
    s = np.arange(WIN_A)[:, None]
    dist = np.abs(b + qi - s).astype(np.float32)
    in_win = dist <= WINDOW
    slopes = jnp.exp2(-8.0 * jnp.arange(1, N_HEADS_A + 1, dtype=F32) / N_HEADS_A) * LOG2E
    alibi = -slopes.reshape(KV_HEADS_A, 1, GQA, 1) * jnp.asarray(dist)[None, :, None, :]
    variants = []
    for valid in (s >= b, s >= 0, s < 2 * b):
        mask = jnp.asarray(in_win & valid)[None, :, None, :]
        variants.append(jnp.where(mask, alibi, NEG_BIG).reshape(KV_HEADS_A, WIN_A, NQ_A))
    bias = jnp.stack(variants)
    sink_cols = jnp.repeat(sink.astype(F32).reshape(KV_HEADS_A, GQA) * LOG2E, b, axis=1)[:, None, :]
    return bias, sink_cols


def _attn_a(qt, ka, vt, sink, batch, seq):
    nq = seq // QB_A
    nblk = seq // ATT_BLOCK
    per = QB_A // ATT_BLOCK
    bias, sink_cols = _attn_a_consts(sink)
    kcol = 0
    prev = lambda i: jnp.maximum(i * per - 1, 0)
    nxt = lambda i: jnp.minimum((i + 1) * per, nblk - 1)
    vrows = KV_HEADS_A * VT_ROWS
    return pl.pallas_call(
        functools.partial(_attn_a_kernel, n_blocks=nblk),
        grid=(batch, nq),
        in_specs=[
            pl.BlockSpec((None, WIDTH_A, QB_A), lambda b, i: (b, 0, i)),
            pl.BlockSpec((QB_A, LANES), lambda b, i: (b * nq + i, kcol)),
            pl.BlockSpec((ATT_BLOCK, LANES), lambda b, i: (b * nblk + prev(i), kcol)),
            pl.BlockSpec((ATT_BLOCK, LANES), lambda b, i: (b * nblk + nxt(i), kcol)),
            pl.BlockSpec((None, vrows, QB_A), lambda b, i: (b, 0, i)),
            pl.BlockSpec((None, vrows, ATT_BLOCK), lambda b, i: (b, 0, prev(i))),
            pl.BlockSpec((None, vrows, ATT_BLOCK), lambda b, i: (b, 0, nxt(i))),
            _const_spec((3, KV_HEADS_A, WIN_A, NQ_A)),
            _const_spec((KV_HEADS_A, 1, NQ_A)),
        ],
        out_specs=pl.BlockSpec((None, WIDTH_A, QB_A), lambda b, i: (b, 0, i)),
        out_shape=jax.ShapeDtypeStruct((batch, WIDTH_A, seq), BF16),
        scratch_shapes=[pltpu.VMEM((QB_A + 2 * ATT_BLOCK, LANES), BF16),
                        pltpu.VMEM((vrows, QB_A + 2 * ATT_BLOCK), BF16),
                        pltpu.VMEM((per * KV_HEADS_A, LANES, NQ_A), BF16)],
        compiler_params=_cparams(("parallel", "parallel")),
        name="attn_a",
    )(qt, ka, ka, ka, vt, vt, vt, bias, sink_cols)


RB_S = 512
HALO = 16


def _ssd_kernel(*refs, rev, n_steps):
    if rev:
        (cur_ref, prev_ref, next_ref, dt_ref, dtt_ref, cw_ref, cb_ref, dtb_c_ref, dtb_r_ref,
         alog_c_ref, alog_r_ref, ltri_ref, utri_ref, dskip_ref, g_ref, yf_ref, out_ref, st_ref) = refs
    else:
        (cur_ref, prev_ref, next_ref, dt_ref, dtt_ref, cw_ref, cb_ref, dtb_c_ref, dtb_r_ref,
         alog_c_ref, alog_r_ref, ltri_ref, utri_ref, out_ref, st_ref) = refs
    i = pl.program_id(1)
    ii = (n_steps - 1 - i) if rev else i
    d = 1 if rev else 0

    @pl.when(i == 0)
    def _():
        st_ref[...] = jnp.zeros_like(st_ref)

    xc = cur_ref[:, 0:CONV_DIM].astype(F32)
    prow = jnp.where(ii == 0, 0.0, prev_ref[...].astype(F32)[HALO - 1:HALO, :])
    nrow = jnp.where(ii == n_steps - 1, 0.0, next_ref[...].astype(F32)[0:1, :])
    rid = lax.broadcasted_iota(jnp.int32, (RB_S, CONV_DIM), 0)
    xm1 = jnp.where(rid == 0, prow, pltpu.roll(xc, 1, 0))
    xp1 = jnp.where(rid == RB_S - 1, nrow, pltpu.roll(xc, RB_S - 1, 0))
    u = xm1 * cw_ref[0:1, :] + xc * cw_ref[1:2, :] + xp1 * cw_ref[2:3, :] + cb_ref[...]
    u = u * _sigmoid(u)

    dt_c = _softplus(dt_ref[:, 0:2 * SSD_HEADS] + dtb_c_ref[...])
    dt_r = _softplus(dtt_ref[...] + dtb_r_ref[...])
    da_c = dt_c * (-jnp.exp(alog_c_ref[...]))
    da_r = dt_r * (-jnp.exp(alog_r_ref[...]))

    lane = lax.broadcasted_iota(jnp.int32, (CHUNK, LANES), 1)
    lo = lane < SSD_HEAD_DIM
    lo_row = lax.broadcasted_iota(jnp.int32, (1, LANES), 1) < SSD_HEAD_DIM
    li = lax.broadcasted_iota(jnp.int32, (CHUNK, CHUNK), 0)
    si = lax.broadcasted_iota(jnp.int32, (CHUNK, CHUNK), 1)
    tri = (si >= li) if rev else (si <= li)

    n_chunks = RB_S // CHUNK
    order = range(n_chunks - 1, -1, -1) if rev else range(n_chunks)
    for c in order:
        r0 = c * CHUNK
        dac = da_c[r0:r0 + CHUNK]
        dar = da_r[:, r0:r0 + CHUNK]
        cs_c = sum(_dot(ltri_ref[...], part.astype(BF16)) for part in _split3(dac))
        cs_r = sum(_dot(part.astype(BF16), utri_ref[...]) for part in _split3(dar))
        tot = cs_c[CHUNK - 1:CHUNK, :]
        if rev:
            e_c = cs_c - dac
            e_r = cs_r - dar
            w_b = jnp.exp(e_c)
            w_c = jnp.exp(tot - e_c)
        else:
            e_c = cs_c
            e_r = cs_r
            w_b = jnp.exp(tot - cs_c)
            w_c = jnp.exp(cs_c)
        dec = jnp.exp(tot)
        dtc = dt_c[r0:r0 + CHUNK]
        uc = u[r0:r0 + CHUNK]
        bm = uc[:, D_INNER:D_INNER + LANES]
        cm = uc[:, D_INNER + LANES:D_INNER + 2 * LANES]
        for pr in range(SSD_GROUPS):
            gmask = lo if pr == 0 else jnp.logical_not(lo)
            cg = jnp.where(gmask, cm, 0.0).astype(BF16)
            bg = jnp.where(gmask, bm, 0.0).astype(BF16)
            gmat = _dot_nt(cg, bg)
            xp = uc[:, pr * LANES:(pr + 1) * LANES]
            l0 = d * SSD_HEADS + 2 * pr
            l1 = l0 + 1
            pair = lambda a: jnp.where(lo, a[:, l0:l0 + 1], a[:, l1:l1 + 1])
            xdt = xp * pair(dtc)
            y = jnp.zeros((CHUNK, LANES), F32)
            for hh, dl in ((0, l0), (1, l1)):
                ecol = e_c[:, dl:dl + 1]
                erow = e_r[dl:dl + 1, :]
                diff = (erow - ecol) if rev else (ecol - erow)
                dmat = jnp.where(tri, jnp.exp(jnp.minimum(diff, 0.0)), 0.0)
                hmask = lo if hh == 0 else jnp.logical_not(lo)
                y = y + _dot((gmat * dmat).astype(BF16), jnp.where(hmask, xdt, 0.0).astype(BF16))
            st = st_ref[pr]
            y = y + pair(w_c) * _dot(cg, st.astype(BF16))
            s_new = _dot_tn(bg, (xdt * pair(w_b)).astype(BF16))
            dec_pair = jnp.where(lo_row, dec[:, l0:l0 + 1], dec[:, l1:l1 + 1])
            st_ref[pr] = st * dec_pair + s_new
            if rev:
                cols = slice(pr * LANES, (pr + 1) * LANES)
                y = y + yf_ref[r0:r0 + CHUNK, cols] + xp * dskip_ref[:, cols]
            out_ref[r0:r0 + CHUNK, pr * LANES:(pr + 1) * LANES] = y

    if rev:
        z = cur_ref[:, CONV_DIM:PB_W].astype(F32)
        yz = out_ref[...] * (z * _sigmoid(z))
        ms = jnp.mean(yz * yz, axis=-1, keepdims=True)
        out_ref[...] = yz * lax.rsqrt(ms + EPS) * g_ref[...]


def _ssd(pb, dt, dtt, params, yf, batch, seq):
    rev = yf is not None
    t = batch * seq
    ns = seq // RB_S
    per = RB_S // HALO
    nh = seq // HALO
    pos = (lambda i: ns - 1 - i) if rev else (lambda i: i)
    ltri = jnp.asarray(np.tril(np.ones((CHUNK, CHUNK), np.float32)), dtype=BF16)
    in_specs = [
        pl.BlockSpec((RB_S, PB_W), lambda b, i: (b * ns + pos(i), 0)),
        pl.BlockSpec((HALO, CONV_DIM), lambda b, i: (b * nh + jnp.maximum(pos(i) * per - 1, 0), 0)),
        pl.BlockSpec((HALO, CONV_DIM), lambda b, i: (b * nh + jnp.minimum((pos(i) + 1) * per, nh - 1), 0)),
        pl.BlockSpec((RB_S, DT_W), lambda b, i: (b * ns + pos(i), 0)),
        pl.BlockSpec((2 * SSD_HEADS, RB_S), lambda b, i: (0, b * ns + pos(i))),
        _const_spec((CONV_K, CONV_DIM)), _const_spec((1, CONV_DIM)),
        _const_spec((1, 2 * SSD_HEADS)), _const_spec((2 * SSD_HEADS, 1)),
        _const_spec((1, 2 * SSD_HEADS)), _const_spec((2 * SSD_HEADS, 1)),
        _const_spec((CHUNK, CHUNK)), _const_spec((CHUNK, CHUNK)),
    ]
    args = [pb, pb, pb, dt, dtt, params["conv_w"], params["conv_b"], params["dtb_c"], params["dtb_r"],
            params["alog_c"], params["alog_r"], ltri, ltri.T]
    if rev:
        in_specs += [_const_spec((1, D_INNER)), _const_spec((1, D_INNER)),
                     pl.BlockSpec((RB_S, D_INNER), lambda b, i: (b * ns + pos(i), 0))]
        args += [params["dskip"], params["norm_g"], yf]
    return pl.pallas_call(
        functools.partial(_ssd_kernel, rev=rev, n_steps=ns),
        grid=(batch, ns),
        in_specs=in_specs,
        out_specs=pl.BlockSpec((RB_S, D_INNER), lambda b, i: (b * ns + pos(i), 0)),
        out_shape=jax.ShapeDtypeStruct((t, D_INNER), F32),
        scratch_shapes=[pltpu.VMEM((SSD_GROUPS, LANES, LANES), F32)],
        compiler_params=_cparams(("parallel", "arbitrary")),
        name="ssd_bwd" if rev else "ssd_fwd",
    )(*args)


TQ_C = 512
TK_C = 512
N_MAPS = 4
KV_SUB = 8


def _attn_c_kernel(slope_ref, lam_ref, qt_ref, k_ref, vt_ref, bias_ref, g_ref, o_ref, qz_s, m_s, acc_s):
    hp = pl.program_id(1)
    qi = pl.program_id(2)
    ki = pl.program_id(3)

    @pl.when(ki == 0)
    def _():
        m_s[...] = jnp.full_like(m_s, NEG_BIG)
        acc_s[...] = jnp.zeros_like(acc_s)
        qz_s[...] = jnp.zeros_like(qz_s)
        for idx in range(N_MAPS):
            c = idx % 2
            qz_s[idx, c * DIFF_QK_DIM:(c + 1) * DIFF_QK_DIM, :] = qt_ref[idx * DIFF_QK_DIM:(idx + 1) * DIFF_QK_DIM, :]

    units = [(s, hh, c) for s in range(KV_SUB) for hh in range(2) for c in range(2)]

    def scores(u):
        s, hh, c = units[u]
        kt = ki * KV_SUB + s
        var = jnp.where(kt < qi, 0, jnp.where(kt == qi, 1, 2))
        return (_dot(k_ref[s * TK_C:(s + 1) * TK_C, hh * LANES:(hh + 1) * LANES], qz_s[2 * hh + c])
                + bias_ref[var, hh])

    t_next = scores(0)
    for u, (s, hh, c) in enumerate(units):
        t = t_next
        if u + 1 < len(units):
            t_next = scores(u + 1)
        idx = 2 * hh + c
        kt = ki * KV_SUB + s
        c_tile = -jnp.abs(qi * TQ_C - kt * TK_C).astype(F32) * slope_ref[2 * hp + hh]
        m_old = m_s[idx]
        m_new = jnp.maximum(m_old, jnp.max(t, axis=0, keepdims=True) + c_tile)
        alpha = jnp.exp2(m_old - m_new)
        p = jnp.exp2(t - (m_new - c_tile))
        vt = vt_ref[hh * VT_ROWS:(hh + 1) * VT_ROWS, s * TK_C:(s + 1) * TK_C]
        acc_s[idx] = alpha * acc_s[idx] + _dot(vt, p.astype(BF16))
        m_s[idx] = m_new

    @pl.when(ki == pl.num_programs(3) - 1)
    def _():
        _attn_c_finalize(lam_ref, g_ref, o_ref, acc_s)


def _attn_c_finalize(lam_ref, g_ref, o_ref, acc_s):
    lam = lam_ref[0]
    for hh in range(2):
        a0 = acc_s[2 * hh]
        a1 = acc_s[2 * hh + 1]
        o = (a0[0:DIFF_V_DIM] * (1.0 / a0[DIFF_V_DIM:DIFF_V_DIM + 1])
             - lam * (a1[0:DIFF_V_DIM] * (1.0 / a1[DIFF_V_DIM:DIFF_V_DIM + 1])))
        ms = jnp.mean(o * o, axis=0, keepdims=True)
        rows = slice(hh * DIFF_V_DIM, (hh + 1) * DIFF_V_DIM)
        o_ref[rows, :] = (o * lax.rsqrt(ms + EPS) * g_ref[rows, :]).astype(BF16)


AUG_ROWS = 16
AUG_LANES = 64
BOUND_SLACK = 1.02
BOUND_LIMIT = 100.0


def _split3(x):
    hi = x.astype(BF16).astype(F32)
    r = x - hi
    mid = r.astype(BF16).astype(F32)
    return hi, mid, (r - mid).astype(BF16).astype(F32)


def _attn_c_bounded_kernel(slope_ref, lam_ref, kmax_ref, qt_ref, k_ref, vt_ref, dtab_ref, g_ref, o_ref,
                           rhs_s, mref_s, acc_s):
    hp = pl.program_id(1)
    qi = pl.program_id(2)
    ki = pl.program_id(3)
    units = [(s, hh, c) for s in range(KV_SUB) for hh in range(2) for c in range(2)]

    @pl.when((pl.program_id(0) == 0) & (hp == 0) & (qi == 0) & (ki == 0))
    def _():
        rhs_s[...] = jnp.zeros_like(rhs_s)

    @pl.when(ki == 0)
    def _():
        acc_s[...] = jnp.zeros_like(acc_s)
        for u, (s, hh, c) in enumerate(units):
            idx = 2 * hh + c
            qseg = qt_ref[idx * DIFF_QK_DIM:(idx + 1) * DIFF_QK_DIM, :]
            rhs_s[u, c * DIFF_QK_DIM:(c + 1) * DIFF_QK_DIM, :] = qseg
            if s == 0:
                qf = qseg.astype(F32)
                kmax = kmax_ref[0] * BOUND_SLACK
                mref_s[idx] = jnp.sqrt(jnp.sum(qf * qf, axis=0, keepdims=True)) * kmax

    qloc = lax.broadcasted_iota(jnp.int32, (1, TQ_C), 1).astype(F32)
    row = lax.broadcasted_iota(jnp.int32, (AUG_ROWS, TQ_C), 0)
    grp = row // 3
    part = row - 3 * grp

    def scores(u):
        s, hh, c = units[u]
        kt = ki * KV_SUB + s
        h = 2 * hp + hh
        sgn = jnp.where(kt < qi, 1.0, jnp.where(kt == qi, 0.0, -1.0)).astype(F32)
        slope = slope_ref[4 * h + 3]
        tile_dist = jnp.abs(qi * TQ_C - kt * TK_C).astype(F32)
        shift = mref_s[2 * hh + c] + slope * tile_dist + (sgn * slope) * qloc
        hi, mid, lo = _split3(shift)
        shift_part = jnp.where(part == 0, hi, jnp.where(part == 1, mid, lo))
        slope_part = sgn * jnp.where(part == 0, slope_ref[4 * h], jnp.where(part == 1, slope_ref[4 * h + 1],
                                                                           slope_ref[4 * h + 2]))
        aug = jnp.where(grp == 0, -shift_part, jnp.where(grp <= 2, slope_part, 0.0))
        rhs_s[u, 2 * DIFF_QK_DIM:2 * DIFF_QK_DIM + AUG_ROWS, :] = aug.astype(BF16)
        on_diag = (kt == qi).astype(jnp.int32)
        return _dot(k_ref[s * TK_C:(s + 1) * TK_C, hh * LANES:(hh + 1) * LANES], rhs_s[u]) + dtab_ref[on_diag, hh]

    t_next = scores(0)
    for u, (s, hh, c) in enumerate(units):
        t = t_next
        if u + 1 < len(units):
            t_next = scores(u + 1)
        idx = 2 * hh + c
        vt = vt_ref[hh * VT_ROWS:(hh + 1) * VT_ROWS, s * TK_C:(s + 1) * TK_C]
        acc_s[idx] = acc_s[idx] + _dot(vt, jnp.exp2(t).astype(BF16))

    @pl.when(ki == pl.num_programs(3) - 1)
    def _():
        _attn_c_finalize(lam_ref, g_ref, o_ref, acc_s)


def _attn_c_bias(slopes):
    rel = (np.arange(TQ_C)[None, :] - np.arange(TK_C)[:, None]).astype(np.float32)
    tables = jnp.asarray(np.stack([-rel, -np.abs(rel), rel]))
    s = slopes.reshape(DIFF_HEADS // 2, 1, 2, 1, 1)
    return s * tables[None, :, None]


def _attn_c_call(body, prefetch, tensors, table_variants, scratch, batch, seq, name):
    nq = seq // TQ_C
    kv_step = TK_C * KV_SUB
    nk = seq // kv_step
    n_pre = len(prefetch)
    grid_spec = pltpu.PrefetchScalarGridSpec(
        num_scalar_prefetch=n_pre,
        grid=(batch, DIFF_HEADS // 2, nq, nk),
        in_specs=[
            pl.BlockSpec((None, LANES, TQ_C), lambda b, h, i, j, *_: (b, h, i)),
            pl.BlockSpec((kv_step, 2 * LANES), lambda b, h, i, j, *_: (b * nk + j, h)),
            pl.BlockSpec((None, 2 * VT_ROWS, kv_step), lambda b, h, i, j, *_: (b, h, j)),
            pl.BlockSpec((None, table_variants, 2, TK_C, TQ_C), lambda b, h, i, j, *_: (h, 0, 0, 0, 0)),
            _const_spec((LANES, TQ_C)),
        ],
        out_specs=pl.BlockSpec((None, LANES, TQ_C), lambda b, h, i, j, *_: (b, h, i)),
        scratch_shapes=scratch + [pltpu.VMEM((N_MAPS, 1, TQ_C), F32), pltpu.VMEM((N_MAPS, VT_ROWS, TQ_C), F32)],
    )
    return pl.pallas_call(
        body,
        grid_spec=grid_spec,
        out_shape=jax.ShapeDtypeStruct((batch, WIDTH_C, seq), BF16),
        compiler_params=_cparams(("arbitrary", "arbitrary", "arbitrary", "arbitrary")),
        name=name,
    )(*prefetch, *tensors)


def _attn_c(qt, k_aug, vt, qk_gains, slopes, lam, gain, batch, seq):
    kv_step = TK_C * KV_SUB
    assert seq % kv_step == 0 and seq % TQ_C == 0
    gain_b = jnp.broadcast_to(gain.reshape(LANES, 1), (LANES, TQ_C))
    tables = _attn_c_bias(slopes)
    seg_bound = lambda g: math.sqrt(DIFF_QK_DIM) * jnp.max(jnp.abs(g))
    qmax = seg_bound(qk_gains[:, 0:256])
    kmax = seg_bound(qk_gains[:, 256:512])

    def online():
        scratch = [pltpu.VMEM((N_MAPS, LANES, TQ_C), BF16)]
        return _attn_c_call(_attn_c_kernel, (slopes, lam), (qt, k_aug, vt, tables, gain_b), 3, scratch,
                            batch, seq, "attn_c_online")

    def bounded():
        slope4 = jnp.stack(_split3(slopes) + (slopes,), axis=1).reshape(-1)
        dtab = jnp.stack([jnp.zeros_like(tables[:, 1]), tables[:, 1]], axis=1)
        scratch = [pltpu.VMEM((N_MAPS * KV_SUB, LANES, TQ_C), BF16)]
        return _attn_c_call(_attn_c_bounded_kernel, (slope4, lam, jnp.reshape(kmax, (1,))),
                            (qt, k_aug, vt, dtab, gain_b), 2, scratch, batch, seq, "attn_c_bounded")

    safe = 2.0 * BOUND_SLACK * BOUND_SLACK * qmax * kmax < BOUND_LIMIT
    return lax.cond(safe, bounded, online)


def _out_proj(oat_ref, ob_ref, oct_ref, x_ref, wo_ref):
    untr = lambda ref: ref[...].astype(F32).T.astype(BF16)
    o = jnp.concatenate([untr(oat_ref), ob_ref[...].astype(BF16), untr(oct_ref)], axis=-1)
    return x_ref[...] + _dot(o, wo_ref[...])


def _rms(x, g_ref):
    ms = jnp.mean(x * x, axis=-1, keepdims=True)
    return x * lax.rsqrt(ms + EPS) * g_ref[...]


def _swiglu_acc(h, wg_ref, wu_ref, wd_ref, acc, d_ff, chunks):
    step = d_ff // chunks
    for c0 in range(0, d_ff, step):
        gate = _dot(h, wg_ref[:, c0:c0 + step])
        up = _dot(h, wu_ref[:, c0:c0 + step])
        act = (gate * _sigmoid(gate) * up).astype(BF16)
        acc = acc + _dot(act, wd_ref[c0:c0 + step, :])
    return acc


def _out_ffn_kernel(oa_ref, ob_ref, oc_ref, x_ref, wo_ref, g_ref, wg_ref, wu_ref, wd_ref, out_ref):
    x1 = _out_proj(oa_ref, ob_ref, oc_ref, x_ref, wo_ref)
    h = _rms(x1, g_ref).astype(BF16)
    out_ref[...] = _swiglu_acc(h, wg_ref, wu_ref, wd_ref, x1, D_FF, FFN_CHUNKS)


def _mixer_out_specs(seq):
    nps = seq // TM
    row = lambda w: pl.BlockSpec((TM, w), lambda i: (i, 0))
    tr = lambda r: pl.BlockSpec((None, r, TM), lambda i: (i // nps, 0, i % nps))
    return [tr(WIDTH_A), row(D_INNER), tr(WIDTH_C), row(D_MODEL), _resident_spec((D_MODEL, D_MODEL)),
            _const_spec((1, D_MODEL))]


def _out_ffn(oa, ob, oc, x, w_out, g, wg, wu, wd):
    t = x.shape[0]
    return pl.pallas_call(
        _out_ffn_kernel,
        grid=(t // TM,),
        in_specs=_mixer_out_specs(oa.shape[2]) + [_resident_spec((D_MODEL, D_FF)), _resident_spec((D_MODEL, D_FF)),
                                       _resident_spec((D_FF, D_MODEL))],
        out_specs=pl.BlockSpec((TM, D_MODEL), lambda i: (i, 0)),
        out_shape=jax.ShapeDtypeStruct((t, D_MODEL), F32),
        compiler_params=_cparams(("parallel",)),
        name="out_ffn",
    )(oa, ob, oc, x, w_out, g, wg, wu, wd)


def _out_router_kernel(oa_ref, ob_ref, oc_ref, x_ref, wo_ref, g_ref, wr_hi_ref, wr_lo_ref,
                       x1_ref, h_ref, sel_ref, gate_ref):
    x1 = _out_proj(oa_ref, ob_ref, oc_ref, x_ref, wo_ref)
    x1_ref[...] = x1
    h = _rms(x1, g_ref)
    h_hi = h.astype(BF16)
    h_ref[...] = h_hi
    h_lo = (h - h_hi.astype(F32)).astype(BF16)
    logits = _dot(h_hi, wr_hi_ref[...]) + (_dot(h_lo, wr_hi_ref[...]) + _dot(h_hi, wr_lo_ref[...]))
    lt = logits.T[0:N_EXPERTS, :]
    row_i = lax.broadcasted_iota(jnp.int32, (N_EXPERTS, TM), 0)
    row = row_i.astype(F32)
    v1 = jnp.max(lt, axis=0, keepdims=True)
    i1 = jnp.min(jnp.where(lt == v1, row, float(N_EXPERTS)), axis=0, keepdims=True)
    rest = jnp.where(row == i1, -jnp.inf, lt)
    v2 = jnp.max(rest, axis=0, keepdims=True)
    i2 = jnp.min(jnp.where(rest == v2, row, float(N_EXPERTS)), axis=0, keepdims=True)
    e = jnp.exp(v2 - v1)
    g1 = 1.0 / (1.0 + e)
    sel_ref[...] = jnp.where(row_i == 0, i1, jnp.where(row_i == 1, i2, 0.0)).astype(jnp.int32)
    gate_ref[...] = jnp.where(row_i == 0, g1, jnp.where(row_i == 1, e * g1, 0.0))


def _out_router(oa, ob, oc, x, w_out, g, wr_hi, wr_lo):
    t = x.shape[0]
    row = lambda w: pl.BlockSpec((TM, w), lambda i: (i, 0))
    col = pl.BlockSpec((N_EXPERTS, TM), lambda i: (0, i))
    return pl.pallas_call(
        _out_router_kernel,
        grid=(t // TM,),
        in_specs=_mixer_out_specs(oa.shape[2]) + [_const_spec((D_MODEL, LANES)), _const_spec((D_MODEL, LANES))],
        out_specs=[row(D_MODEL), row(D_MODEL), col, col],
        out_shape=[jax.ShapeDtypeStruct((t, D_MODEL), F32), jax.ShapeDtypeStruct((t, D_MODEL), BF16),
                   jax.ShapeDtypeStruct((N_EXPERTS, t), jnp.int32), jax.ShapeDtypeStruct((N_EXPERTS, t), F32)],
        compiler_params=_cparams(("parallel",)),
        name="out_router",
    )(oa, ob, oc, x, w_out, g, wr_hi, wr_lo)


def _moe_ffn_kernel(blk_exp_ref, n_used_ref, x_ref, wg_ref, wu_ref, wd_ref, out_ref):
    i = pl.program_id(0)

    @pl.when(i < n_used_ref[0])
    def _():
        acc = jnp.zeros((MOE_BLK, D_MODEL), F32)
        out_ref[...] = _swiglu_acc(x_ref[...], wg_ref.at[0], wu_ref.at[0], wd_ref.at[0], acc,
                                   D_FF_EXPERT, MOE_CHUNKS).astype(BF16)

    @pl.when(i >= n_used_ref[0])
    def _():
        out_ref[...] = jnp.zeros_like(out_ref)


def _moe_ffn(xg, blk_exp, n_used, wg, wu, wd):
    rows = xg.shape[0]
    wspec = lambda shape: pl.BlockSpec((1,) + shape, lambda i, be, nu: (be[i], 0, 0),
                                       pipeline_mode=pl.Buffered(1))
    grid_spec = pltpu.PrefetchScalarGridSpec(
        num_scalar_prefetch=2,
        grid=(rows // MOE_BLK,),
        in_specs=[pl.BlockSpec((MOE_BLK, D_MODEL), lambda i, be, nu: (i, 0)),
                  wspec((D_MODEL, D_FF_EXPERT)), wspec((D_MODEL, D_FF_EXPERT)),
                  wspec((D_FF_EXPERT, D_MODEL))],
        out_specs=pl.BlockSpec((MOE_BLK, D_MODEL), lambda i, be, nu: (i, 0)),
    )
    return pl.pallas_call(
        _moe_ffn_kernel,
        grid_spec=grid_spec,
        out_shape=jax.ShapeDtypeStruct((rows, D_MODEL), BF16),
        compiler_params=_cparams(("arbitrary",)),
        name="moe_ffn",
    )(blk_exp, n_used, xg, wg, wu, wd)


def _moe(x1, h, sel, gates, wg, wu, wd):
    t = x1.shape[0]
    i1, i2 = sel[0], sel[1]
    g1, g2 = gates[0][:, None], gates[1][:, None]
    experts = jnp.arange(N_EXPERTS, dtype=jnp.int32)
    onehot = ((i1[:, None] == experts) | (i2[:, None] == experts)).astype(jnp.int32)
    rank = jnp.cumsum(onehot, axis=0) - onehot
    counts = jnp.sum(onehot, axis=0)
    padded = ((counts + MOE_BLK - 1) // MOE_BLK) * MOE_BLK
    pad_end = jnp.cumsum(padded)
    pad_start = pad_end - padded
    dest = pad_start[None, :] + rank
    d1 = jnp.take_along_axis(dest, i1[:, None], axis=1)[:, 0]
    d2 = jnp.take_along_axis(dest, i2[:, None], axis=1)[:, 0]
    n_blocks = (2 * t) // MOE_BLK + N_EXPERTS
    rows = n_blocks * MOE_BLK
    tok = jnp.arange(t, dtype=jnp.int32)
    row_tok = jnp.zeros((rows,), jnp.int32).at[jnp.concatenate([d1, d2])].set(jnp.concatenate([tok, tok]))
    blk_start = jnp.arange(n_blocks, dtype=jnp.int32) * MOE_BLK
    blk_exp = jnp.minimum(jnp.sum((pad_end[None, :] <= blk_start[:, None]).astype(jnp.int32), axis=1),
                          N_EXPERTS - 1)
    n_used = (pad_end[-1:] // MOE_BLK).astype(jnp.int32)
    yb = _moe_ffn(h[row_tok], blk_exp, n_used, wg, wu, wd)
    return x1 + g1 * yb[d1].astype(F32) + g2 * yb[d2].astype(F32)


def _split_cols(w):
    outs, start = [], 0
    for n in SPLIT_SIZES:
        outs.append(w[:, start:start + n])
        start += n
    return outs


def _layer_params(l, p):
    qa, ka, va, zb, xbc, dtb, qc, kc, vc = _split_cols(p["w_in"][l])
    w_cat = jnp.concatenate(
        [qa, ka, va, xbc, zb, qc, kc, vc, dtb, jnp.zeros((D_MODEL, DT_W - 2 * SSD_HEADS), F32)],
        axis=1).astype(BF16)
    ga = jnp.concatenate([jnp.tile(p["qnorm_a"][l] * (HEAD_DIM_A ** -0.5 * LOG2E), N_HEADS_A),
                          jnp.tile(p["knorm_a"][l], KV_HEADS_A)])[None, :]
    gc = jnp.concatenate([jnp.tile(p["qnorm_c"][l] * (DIFF_QK_DIM ** -0.5 * LOG2E), 2 * DIFF_HEADS),
                          jnp.tile(p["knorm_c"][l], 2 * DIFF_HEADS)])[None, :]
    lam_init = 0.8 - 0.6 * math.exp(-0.3 * l)
    lc = p["lam_c"][l].astype(F32)
    lam = jnp.exp(jnp.sum(lc[0] * lc[1])) - jnp.exp(jnp.sum(lc[2] * lc[3])) + lam_init
    out = {
        "norm_mix_g": p["norm_mix_g"][l][None, :], "w_cat": w_cat, "ga": ga, "gc": gc,
        "sink": p["sink_a"][l],
        "ssd": {
            "conv_w": p["conv_w"][l], "conv_b": p["conv_b"][l][None, :],
            "dtb_c": p["dt_bias"][l].reshape(1, -1), "dtb_r": p["dt_bias"][l].reshape(-1, 1),
            "alog_c": p["a_log"][l].reshape(1, -1), "alog_r": p["a_log"][l].reshape(-1, 1),
            "dskip": jnp.repeat(p["d_skip"][l], SSD_HEAD_DIM)[None, :],
            "norm_g": p["ssd_norm_g"][l][None, :],
        },
        "slopes_c": jnp.exp2(-8.0 * jnp.arange(1, DIFF_HEADS + 1, dtype=F32) / DIFF_HEADS) * LOG2E,
        "lam": jnp.reshape(lam, (1,)).astype(F32),
        "gain_c": (jnp.tile(p["diff_norm_g"][l], 2) * (1.0 - lam_init))[None, :],
        "w_out": p["w_out"][l].astype(BF16),
        "norm_ffn_g": p["norm_ffn_g"][l][None, :],
    }
    i = l // 2
    if l % 2 == 0:
        out["ffn"] = (p["ffn_w_gate"][i].astype(BF16), p["ffn_w_up"][i].astype(BF16),
                      p["ffn_w_down"][i].astype(BF16))
    else:
        wr = jnp.pad(p["router_w"][i], ((0, 0), (0, LANES - N_EXPERTS)))
        wr_hi = wr.astype(BF16)
        out["router"] = (wr_hi, (wr - wr_hi.astype(F32)).astype(BF16))
        out["moe"] = (p["moe_w_gate"][i].astype(BF16), p["moe_w_up"][i].astype(BF16),
                      p["moe_w_down"][i].astype(BF16))
    return out


def _trunk(x3, layers):
    batch, seq, _ = x3.shape
    x = x3.reshape(batch * seq, D_MODEL)
    for l, lp in enumerate(layers):
        qat, ka, vat, pb, qct, kc, vct, dt, dtt = _norm_proj(x, lp["norm_mix_g"], lp["w_cat"], lp["ga"], lp["gc"],
                                                             batch, seq)
        oa = _attn_a(qat, ka, vat, lp["sink"], batch, seq)
        yf = _ssd(pb, dt, dtt, lp["ssd"], None, batch, seq)
        ob = _ssd(pb, dt, dtt, lp["ssd"], yf, batch, seq)
        oc = _attn_c(qct, kc, vct, lp["gc"], lp["slopes_c"], lp["lam"], lp["gain_c"], batch, seq)
        if l % 2 == 0:
            x = _out_ffn(oa, ob, oc, x, lp["w_out"], lp["norm_ffn_g"], *lp["ffn"])
        else:
            x1, h, sel, gates = _out_router(oa, ob, oc, x, lp["w_out"], lp["norm_ffn_g"], *lp["router"])
            x = _moe(x1, h, sel, gates, *lp["moe"])
    return x.reshape(batch, seq, D_MODEL)


def kernel(x_prompt, x_sample, norm_mix_g, w_in, qnorm_a, knorm_a, sink_a, conv_w, conv_b, dt_bias, a_log,
           d_skip, ssd_norm_g, qnorm_c, knorm_c, lam_c, diff_norm_g, w_out, norm_ffn_g, ffn_w_gate, ffn_w_up,
           ffn_w_down, router_w, moe_w_gate, moe_w_up, moe_w_down):
    p = dict(norm_mix_g=norm_mix_g, w_in=w_in, qnorm_a=qnorm_a, knorm_a=knorm_a, sink_a=sink_a,
             conv_w=conv_w, conv_b=conv_b, dt_bias=dt_bias, a_log=a_log, d_skip=d_skip,
             ssd_norm_g=ssd_norm_g, qnorm_c=qnorm_c, knorm_c=knorm_c, lam_c=lam_c, diff_norm_g=diff_norm_g,
             w_out=w_out, norm_ffn_g=norm_ffn_g, ffn_w_gate=ffn_w_gate, ffn_w_up=ffn_w_up,
             ffn_w_down=ffn_w_down, router_w=router_w, moe_w_gate=moe_w_gate, moe_w_up=moe_w_up,
             moe_w_down=moe_w_down)
    layers = [_layer_params(l, p) for l in range(DEPTH)]
    return _trunk(x_prompt, layers), _trunk(x_sample, layers)
```

```python
import functools
import math

import numpy as np
import jax
import jax.numpy as jnp
from jax import lax
from jax.experimental import pallas as pl
from jax.experimental.pallas import tpu as pltpu

F32 = jnp.float32
BF16 = jnp.bfloat16

D_MODEL = 1024
DEPTH = 2
N_HEADS_A = 8
KV_HEADS_A = 2
HEAD_DIM_A = 64
WINDOW = 128
ATT_BLOCK = 128
SSD_HEADS = 4
SSD_HEAD_DIM = 64
D_INNER = SSD_HEADS * SSD_HEAD_DIM
SSD_GROUPS = 2
D_STATE = 64
CONV_K = 3
CHUNK = 128
DIFF_HEADS = 4
DIFF_QK_DIM = 32
DIFF_V_DIM = 64
WIDTH_A = N_HEADS_A * HEAD_DIM_A
WIDTH_C = DIFF_HEADS * DIFF_V_DIM
CONV_DIM = D_INNER + 2 * SSD_GROUPS * D_STATE
SPLIT_SIZES = (WIDTH_A, KV_HEADS_A * HEAD_DIM_A, KV_HEADS_A * HEAD_DIM_A,
               D_INNER, CONV_DIM, 2 * SSD_HEADS,
               DIFF_HEADS * 2 * DIFF_QK_DIM, DIFF_HEADS * 2 * DIFF_QK_DIM, WIDTH_C)
D_FF = 2816
N_EXPERTS = 8
D_FF_EXPERT = 3584
EPS = 1e-6

LOG2E = 1.4426950408889634
NEG_BIG = -1e30

LANES = 128
VMEM_LIMIT = 56 * 1024 * 1024

PA_W = 768
PA_NORM_W = 640
VT_ROWS = 80
PB_W = 768
PC_W = 768
DT_W = 128
PROJ_W = PA_W + PB_W + PC_W + DT_W

TM = 512
MOE_BLK = 512
FFN_CHUNKS = 1
MOE_CHUNKS = 2


def _cparams(sem, vmem=VMEM_LIMIT):
    return pltpu.CompilerParams(dimension_semantics=sem, vmem_limit_bytes=vmem)


def _const_spec(shape):
    nd = len(shape)
    return pl.BlockSpec(shape, lambda *_: (0,) * nd)


def _resident_spec(shape):
    nd = len(shape)
    return pl.BlockSpec(shape, lambda *_: (0,) * nd, pipeline_mode=pl.Buffered(1))


def _block_diag(n, seg):
    idx = np.arange(n) // seg
    return jnp.asarray((idx[:, None] == idx[None, :]).astype(np.float32), dtype=BF16)


def _sigmoid(x):
    return 1.0 / (1.0 + jnp.exp(-x))


def _softplus(x):
    return jnp.maximum(x, 0.0) + jnp.log(1.0 + jnp.exp(-jnp.abs(x)))


def _dot(a, b):
    return jnp.dot(a, b, preferred_element_type=F32)


def _dot_nt(a, b):
    return lax.dot_general(a, b, (((1,), (1,)), ((), ())), preferred_element_type=F32)


def _dot_tn(a, b):
    return lax.dot_general(a, b, (((0,), (0,)), ((), ())), preferred_element_type=F32)


def _dot_hi(a, b):
    return jnp.dot(a, b, preferred_element_type=F32, precision=lax.Precision.HIGHEST)


def _seg_norm(blk, seg, bd_ref, gain):
    w = blk.shape[1]
    ss = _dot((blk * blk).astype(BF16), bd_ref[0:w, 0:w]) * (1.0 / seg)
    return blk * lax.rsqrt(ss + EPS) * gain


def _store_vt(vt_ref, v, heads, head_dim, pad_ref):
    vt = v.T
    for h in range(heads):
        r0 = h * VT_ROWS
        vt_ref[r0:r0 + head_dim, :] = vt[h * head_dim:(h + 1) * head_dim].astype(BF16)
        vt_ref[r0 + head_dim:r0 + VT_ROWS, :] = pad_ref[...]


def _norm_proj_kernel(x_ref, g_ref, w_ref, bd64_ref, bd32_ref, ga_ref, gc_ref, pad_ref, kaug_ref,
                      qat_ref, ka_ref, vat_ref, pb_ref, qct_ref, kc_ref, vct_ref, dt_ref, dtt_ref):
    x = x_ref[...]
    ms = jnp.mean(x * x, axis=-1, keepdims=True)
    h = (x * lax.rsqrt(ms + EPS) * g_ref[...]).astype(BF16)
    ya = _dot(h, w_ref[:, 0:PA_W])
    for c0 in range(0, WIDTH_A, 256):
        qn = _seg_norm(ya[:, c0:c0 + 256], HEAD_DIM_A, bd64_ref, ga_ref[:, c0:c0 + 256])
        qat_ref[c0:c0 + 256, :] = qn.T.astype(BF16)
    ka_ref[...] = _seg_norm(ya[:, WIDTH_A:PA_NORM_W], HEAD_DIM_A, bd64_ref, ga_ref[:, WIDTH_A:PA_NORM_W]).astype(BF16)
    _store_vt(vat_ref, ya[:, PA_NORM_W:PA_W], KV_HEADS_A, HEAD_DIM_A, pad_ref)
    pb_ref[...] = _dot(h, w_ref[:, PA_W:PA_W + PB_W]).astype(BF16)
    yc = _dot(h, w_ref[:, PA_W + PB_W:PA_W + PB_W + PC_W])
    qct_ref[...] = _seg_norm(yc[:, 0:256], DIFF_QK_DIM, bd32_ref, gc_ref[:, 0:256]).T.astype(BF16)
    kc = _seg_norm(yc[:, 256:512], DIFF_QK_DIM, bd32_ref, gc_ref[:, 256:512]).astype(BF16)
    aug = kaug_ref[...]
    pieces = []
    for hd in range(DIFF_HEADS):
        pieces += [kc[:, hd * 2 * DIFF_QK_DIM:(hd + 1) * 2 * DIFF_QK_DIM], aug]
    kc_ref[...] = jnp.concatenate(pieces, axis=1)
    _store_vt(vct_ref, yc[:, 512:PC_W], DIFF_HEADS, DIFF_V_DIM, pad_ref)
    yd = _dot(h, w_ref[:, PA_W + PB_W + PC_W:PROJ_W])
    dt_ref[...] = yd
    dtt_ref[...] = yd.T[0:2 * SSD_HEADS, :]


def _norm_proj(x, g, w_cat, ga, gc, batch, seq):
    t = batch * seq
    nps = seq // TM
    row = lambda w: pl.BlockSpec((TM, w), lambda i: (i, 0))
    tr = lambda r: pl.BlockSpec((None, r, TM), lambda i: (i // nps, 0, i % nps))
    pad = jnp.zeros((VT_ROWS - HEAD_DIM_A, TM), BF16).at[0].set(1.0)
    pos = np.arange(TM) % TK_C
    cols = np.zeros((TM, AUG_LANES), np.float32)
    cols[:, 0:3] = 1.0
    cols[:, 3:6] = (pos // 256 * 256)[:, None]
    cols[:, 6:9] = (pos % 256)[:, None]
    return pl.pallas_call(
        _norm_proj_kernel,
        grid=(t // TM,),
        in_specs=[row(D_MODEL), _const_spec((1, D_MODEL)), _resident_spec((D_MODEL, PROJ_W)),
                  _const_spec((256, 256)), _const_spec((256, 256)),
                  _const_spec((1, PA_NORM_W)), _const_spec((1, 512)),
                  _const_spec((VT_ROWS - HEAD_DIM_A, TM)), _const_spec((TM, AUG_LANES))],
        out_specs=[tr(WIDTH_A), row(LANES), tr(KV_HEADS_A * VT_ROWS), row(PB_W),
                   tr(256), row(DIFF_HEADS * LANES), tr(DIFF_HEADS * VT_ROWS), row(DT_W),
                   pl.BlockSpec((2 * SSD_HEADS, TM), lambda i: (0, i))],
        out_shape=[jax.ShapeDtypeStruct((batch, WIDTH_A, seq), BF16), jax.ShapeDtypeStruct((t, LANES), BF16),
                   jax.ShapeDtypeStruct((batch, KV_HEADS_A * VT_ROWS, seq), BF16),
                   jax.ShapeDtypeStruct((t, PB_W), BF16),
                   jax.ShapeDtypeStruct((batch, 256, seq), BF16),
                   jax.ShapeDtypeStruct((t, DIFF_HEADS * LANES), BF16),
                   jax.ShapeDtypeStruct((batch, DIFF_HEADS * VT_ROWS, seq), BF16),
                   jax.ShapeDtypeStruct((t, DT_W), F32), jax.ShapeDtypeStruct((2 * SSD_HEADS, t), F32)],
        compiler_params=_cparams(("parallel",)),
        name="norm_proj",
    )(x, g, w_cat, _block_diag(256, HEAD_DIM_A), _block_diag(256, DIFF_QK_DIM), ga, gc, pad,
      jnp.asarray(cols, dtype=BF16))


QB_A = 1024


GQA = N_HEADS_A // KV_HEADS_A
WIN_A = 3 * ATT_BLOCK
NQ_A = GQA * ATT_BLOCK


def _attn_a_kernel(qt_ref, k_ref, kp_ref, kn_ref, vt_ref, vtp_ref, vtn_ref, bias_ref, sink_ref, o_ref,
                   k_s, vt_s, rhs_s, *, n_blocks):
    i = pl.program_id(1)
    nsub = QB_A // ATT_BLOCK
    b = ATT_BLOCK
    k_s[0:b] = kp_ref[...]
    k_s[b:b + QB_A] = k_ref[...]
    k_s[b + QB_A:] = kn_ref[...]
    vt_s[:, 0:b] = vtp_ref[...]
    vt_s[:, b:b + QB_A] = vt_ref[...]
    vt_s[:, b + QB_A:] = vtn_ref[...]
    units = [(t, j) for t in range(nsub) for j in range(KV_HEADS_A)]

    def scores(u):
        t, j = units[u]
        other = 1 - j
        rhs_s[u, other * HEAD_DIM_A:(other + 1) * HEAD_DIM_A, :] = jnp.zeros((HEAD_DIM_A, NQ_A), BF16)
        for g in range(GQA):
            h = GQA * j + g
            rhs_s[u, j * HEAD_DIM_A:(j + 1) * HEAD_DIM_A, g * b:(g + 1) * b] = (
                qt_ref[h * HEAD_DIM_A:(h + 1) * HEAD_DIM_A, t * b:(t + 1) * b])
        gb = i * nsub + t
        var = jnp.where(gb == 0, 0, jnp.where(gb == n_blocks - 1, 2, 1))
        return _dot(k_s[t * b:t * b + WIN_A, :], rhs_s[u]) + bias_ref[var, j]

    s_next = scores(0)
    for u, (t, j) in enumerate(units):
        s = s_next
        if u + 1 < len(units):
            s_next = scores(u + 1)
        sk = sink_ref[j]
        m = jnp.maximum(jnp.max(s, axis=0, keepdims=True), sk)
        p = jnp.exp2(s - m)
        r = _dot(vt_s[j * VT_ROWS:(j + 1) * VT_ROWS, t * b:t * b + WIN_A], p.astype(BF16))
        den = r[HEAD_DIM_A:HEAD_DIM_A + 1] + jnp.exp2(sk - m)
        o = (r[0:HEAD_DIM_A] * (1.0 / den)).astype(BF16)
        for g in range(GQA):
            h = GQA * j + g
            o_ref[h * HEAD_DIM_A:(h + 1) * HEAD_DIM_A, t * b:(t + 1) * b] = o[:, g * b:(g + 1) * b]


def _attn_a_consts(sink):
    b = ATT_BLOCK
    qi = np.arange(b)[None, :]
    s = np.arange(WIN_A)[:, None]
    dist = np.abs(b + qi - s).astype(np.float32)
    in_win = dist <= WINDOW
    slopes = jnp.exp2(-8.0 * jnp.arange(1, N_HEADS_A + 1, dtype=F32) / N_HEADS_A) * LOG2E
    alibi = -slopes.reshape(KV_HEADS_A, 1, GQA, 1) * jnp.asarray(dist)[None, :, None, :]
    variants = []
    for valid in (s >= b, s >= 0, s < 2 * b):
        mask = jnp.asarray(in_win & valid)[None, :, None, :]
        variants.append(jnp.where(mask, alibi, NEG_BIG).reshape(KV_HEADS_A, WIN_A, NQ_A))
    bias = jnp.stack(variants)
    sink_cols = jnp.repeat(sink.astype(F32).reshape(KV_HEADS_A, GQA) * LOG2E, b, axis=1)[:, None, :]
    return bias, sink_cols


def _attn_a(qt, ka, vt, sink, batch, seq):
    nq = seq // QB_A
    nblk = seq // ATT_BLOCK
    per = QB_A // ATT_BLOCK
    bias, sink_cols = _attn_a_consts(sink)
    kcol = 0
    prev = lambda i: jnp.maximum(i * per - 1, 0)
    nxt = lambda i: jnp.minimum((i + 1) * per, nblk - 1)
    vrows = KV_HEADS_A * VT_ROWS
    return pl.pallas_call(
        functools.partial(_attn_a_kernel, n_blocks=nblk),
        grid=(batch, nq),
        in_specs=[
            pl.BlockSpec((None, WIDTH_A, QB_A), lambda b, i: (b, 0, i)),
            pl.BlockSpec((QB_A, LANES), lambda b, i: (b * nq + i, kcol)),
            pl.BlockSpec((ATT_BLOCK, LANES), lambda b, i: (b * nblk + prev(i), kcol)),
            pl.BlockSpec((ATT_BLOCK, LANES), lambda b, i: (b * nblk + nxt(i), kcol)),
            pl.BlockSpec((None, vrows, QB_A), lambda b, i: (b, 0, i)),
            pl.BlockSpec((None, vrows, ATT_BLOCK), lambda b, i: (b, 0, prev(i))),
            pl.BlockSpec((None, vrows, ATT_BLOCK), lambda b, i: (b, 0, nxt(i))),
            _const_spec((3, KV_HEADS_A, WIN_A, NQ_A)),
            _const_spec((KV_HEADS_A, 1, NQ_A)),
        ],
        out_specs=pl.BlockSpec((None, WIDTH_A, QB_A), lambda b, i: (b, 0, i)),
        out_shape=jax.ShapeDtypeStruct((batch, WIDTH_A, seq), BF16),
        scratch_shapes=[pltpu.VMEM((QB_A + 2 * ATT_BLOCK, LANES), BF16),
                        pltpu.VMEM((vrows, QB_A + 2 * ATT_BLOCK), BF16),
                        pltpu.VMEM((per * KV_HEADS_A, LANES, NQ_A), BF16)],
        compiler_params=_cparams(("parallel", "parallel")),
        name="attn_a",
    )(qt, ka, ka, ka, vt, vt, vt, bias, sink_cols)


RB_S = 512
HALO = 16


def _ssd_kernel(*refs, rev, n_steps):
    if rev:
        (cur_ref, prev_ref, next_ref, dt_ref, dtt_ref, cw_ref, cb_ref, dtb_c_ref, dtb_r_ref,
         alog_c_ref, alog_r_ref, ltri_ref, utri_ref, dskip_ref, g_ref, yf_ref, out_ref, st_ref) = refs
    else:
        (cur_ref, prev_ref, next_ref, dt_ref, dtt_ref, cw_ref, cb_ref, dtb_c_ref, dtb_r_ref,
         alog_c_ref, alog_r_ref, ltri_ref, utri_ref, out_ref, st_ref) = refs
    i = pl.program_id(1)
    ii = (n_steps - 1 - i) if rev else i
    d = 1 if rev else 0

    @pl.when(i == 0)
    def _():
        st_ref[...] = jnp.zeros_like(st_ref)

    xc = cur_ref[:, 0:CONV_DIM].astype(F32)
    prow = jnp.where(ii == 0, 0.0, prev_ref[...].astype(F32)[HALO - 1:HALO, :])
    nrow = jnp.where(ii == n_steps - 1, 0.0, next_ref[...].astype(F32)[0:1, :])
    rid = lax.broadcasted_iota(jnp.int32, (RB_S, CONV_DIM), 0)
    xm1 = jnp.where(rid == 0, prow, pltpu.roll(xc, 1, 0))
    xp1 = jnp.where(rid == RB_S - 1, nrow, pltpu.roll(xc, RB_S - 1, 0))
    u = xm1 * cw_ref[0:1, :] + xc * cw_ref[1:2, :] + xp1 * cw_ref[2:3, :] + cb_ref[...]
    u = u * _sigmoid(u)

    dt_c = _softplus(dt_ref[:, 0:2 * SSD_HEADS] + dtb_c_ref[...])
    dt_r = _softplus(dtt_ref[...] + dtb_r_ref[...])
    da_c = dt_c * (-jnp.exp(alog_c_ref[...]))
    da_r = dt_r * (-jnp.exp(alog_r_ref[...]))

    lane = lax.broadcasted_iota(jnp.int32, (CHUNK, LANES), 1)
    lo = lane < SSD_HEAD_DIM
    lo_row = lax.broadcasted_iota(jnp.int32, (1, LANES), 1) < SSD_HEAD_DIM
    li = lax.broadcasted_iota(jnp.int32, (CHUNK, CHUNK), 0)
    si = lax.broadcasted_iota(jnp.int32, (CHUNK, CHUNK), 1)
    tri = (si >= li) if rev else (si <= li)

    n_chunks = RB_S // CHUNK
    order = range(n_chunks - 1, -1, -1) if rev else range(n_chunks)
    for c in order:
        r0 = c * CHUNK
        dac = da_c[r0:r0 + CHUNK]
        dar = da_r[:, r0:r0 + CHUNK]
        cs_c = sum(_dot(ltri_ref[...], part.astype(BF16)) for part in _split3(dac))
        cs_r = sum(_dot(part.astype(BF16), utri_ref[...]) for part in _split3(dar))
        tot = cs_c[CHUNK - 1:CHUNK, :]
        if rev:
            e_c = cs_c - dac
            e_r = cs_r - dar
            w_b = jnp.exp(e_c)
            w_c = jnp.exp(tot - e_c)
        else:
            e_c = cs_c
            e_r = cs_r
            w_b = jnp.exp(tot - cs_c)
            w_c = jnp.exp(cs_c)
        dec = jnp.exp(tot)
        dtc = dt_c[r0:r0 + CHUNK]
        uc = u[r0:r0 + CHUNK]
        bm = uc[:, D_INNER:D_INNER + LANES]
        cm = uc[:, D_INNER + LANES:D_INNER + 2 * LANES]
        for pr in range(SSD_GROUPS):
            gmask = lo if pr == 0 else jnp.logical_not(lo)
            cg = jnp.where(gmask, cm, 0.0).astype(BF16)
            bg = jnp.where(gmask, bm, 0.0).astype(BF16)
            gmat = _dot_nt(cg, bg)
            xp = uc[:, pr * LANES:(pr + 1) * LANES]
            l0 = d * SSD_HEADS + 2 * pr
            l1 = l0 + 1
            pair = lambda a: jnp.where(lo, a[:, l0:l0 + 1], a[:, l1:l1 + 1])
            xdt = xp * pair(dtc)
            y = jnp.zeros((CHUNK, LANES), F32)
            for hh, dl in ((0, l0), (1, l1)):
                ecol = e_c[:, dl:dl + 1]
                erow = e_r[dl:dl + 1, :]
                diff = (erow - ecol) if rev else (ecol - erow)
                dmat = jnp.where(tri, jnp.exp(jnp.minimum(diff, 0.0)), 0.0)
                hmask = lo if hh == 0 else jnp.logical_not(lo)
                y = y + _dot((gmat * dmat).astype(BF16), jnp.where(hmask, xdt, 0.0).astype(BF16))
            st = st_ref[pr]
            y = y + pair(w_c) * _dot(cg, st.astype(BF16))
            s_new = _dot_tn(bg, (xdt * pair(w_b)).astype(BF16))
            dec_pair = jnp.where(lo_row, dec[:, l0:l0 + 1], dec[:, l1:l1 + 1])
            st_ref[pr] = st * dec_pair + s_new
            if rev:
                cols = slice(pr * LANES, (pr + 1) * LANES)
                y = y + yf_ref[r0:r0 + CHUNK, cols] + xp * dskip_ref[:, cols]
            out_ref[r0:r0 + CHUNK, pr * LANES:(pr + 1) * LANES] = y

    if rev:
        z = cur_ref[:, CONV_DIM:PB_W].astype(F32)
        yz = out_ref[...] * (z * _sigmoid(z))
        ms = jnp.mean(yz * yz, axis=-1, keepdims=True)
        out_ref[...] = yz * lax.rsqrt(ms + EPS) * g_ref[...]


def _ssd(pb, dt, dtt, params, yf, batch, seq):
    rev = yf is not None
    t = batch * seq
    ns = seq // RB_S
    per = RB_S // HALO
    nh = seq // HALO
    pos = (lambda i: ns - 1 - i) if rev else (lambda i: i)
    ltri = jnp.asarray(np.tril(np.ones((CHUNK, CHUNK), np.float32)), dtype=BF16)
    in_specs = [
        pl.BlockSpec((RB_S, PB_W), lambda b, i: (b * ns + pos(i), 0)),
        pl.BlockSpec((HALO, CONV_DIM), lambda b, i: (b * nh + jnp.maximum(pos(i) * per - 1, 0), 0)),
        pl.BlockSpec((HALO, CONV_DIM), lambda b, i: (b * nh + jnp.minimum((pos(i) + 1) * per, nh - 1), 0)),
        pl.BlockSpec((RB_S, DT_W), lambda b, i: (b * ns + pos(i), 0)),
        pl.BlockSpec((2 * SSD_HEADS, RB_S), lambda b, i: (0, b * ns + pos(i))),
        _const_spec((CONV_K, CONV_DIM)), _const_spec((1, CONV_DIM)),
        _const_spec((1, 2 * SSD_HEADS)), _const_spec((2 * SSD_HEADS, 1)),
        _const_spec((1, 2 * SSD_HEADS)), _const_spec((2 * SSD_HEADS, 1)),
        _const_spec((CHUNK, CHUNK)), _const_spec((CHUNK, CHUNK)),
    ]
    args = [pb, pb, pb, dt, dtt, params["conv_w"], params["conv_b"], params["dtb_c"], params["dtb_r"],
            params["alog_c"], params["alog_r"], ltri, ltri.T]
    if rev:
        in_specs += [_const_spec((1, D_INNER)), _const_spec((1, D_INNER)),
                     pl.BlockSpec((RB_S, D_INNER), lambda b, i: (b * ns + pos(i), 0))]
        args += [params["dskip"], params["norm_g"], yf]
    return pl.pallas_call(
        functools.partial(_ssd_kernel, rev=rev, n_steps=ns),
        grid=(batch, ns),
        in_specs=in_specs,
        out_specs=pl.BlockSpec((RB_S, D_INNER), lambda b, i: (b * ns + pos(i), 0)),
        out_shape=jax.ShapeDtypeStruct((t, D_INNER), F32),
        scratch_shapes=[pltpu.VMEM((SSD_GROUPS, LANES, LANES), F32)],
        compiler_params=_cparams(("parallel", "arbitrary")),
        name="ssd_bwd" if rev else "ssd_fwd",
    )(*args)


TQ_C = 512
TK_C = 512
N_MAPS = 4
KV_SUB = 8


def _attn_c_kernel(slope_ref, lam_ref, qt_ref, k_ref, vt_ref, bias_ref, g_ref, o_ref, qz_s, m_s, acc_s):
    hp = pl.program_id(1)
    qi = pl.program_id(2)
    ki = pl.program_id(3)

    @pl.when(ki == 0)
    def _():
        m_s[...] = jnp.full_like(m_s, NEG_BIG)
        acc_s[...] = jnp.zeros_like(acc_s)
        qz_s[...] = jnp.zeros_like(qz_s)
        for idx in range(N_MAPS):
            c = idx % 2
            qz_s[idx, c * DIFF_QK_DIM:(c + 1) * DIFF_QK_DIM, :] = qt_ref[idx * DIFF_QK_DIM:(idx + 1) * DIFF_QK_DIM, :]

    units = [(s, hh, c) for s in range(KV_SUB) for hh in range(2) for c in range(2)]

    def scores(u):
        s, hh, c = units[u]
        kt = ki * KV_SUB + s
        var = jnp.where(kt < qi, 0, jnp.where(kt == qi, 1, 2))
        return (_dot(k_ref[s * TK_C:(s + 1) * TK_C, hh * LANES:(hh + 1) * LANES], qz_s[2 * hh + c])
                + bias_ref[var, hh])

    t_next = scores(0)
    for u, (s, hh, c) in enumerate(units):
        t = t_next
        if u + 1 < len(units):
            t_next = scores(u + 1)
        idx = 2 * hh + c
        kt = ki * KV_SUB + s
        c_tile = -jnp.abs(qi * TQ_C - kt * TK_C).astype(F32) * slope_ref[2 * hp + hh]
        m_old = m_s[idx]
        m_new = jnp.maximum(m_old, jnp.max(t, axis=0, keepdims=True) + c_tile)
        alpha = jnp.exp2(m_old - m_new)
        p = jnp.exp2(t - (m_new - c_tile))
        vt = vt_ref[hh * VT_ROWS:(hh + 1) * VT_ROWS, s * TK_C:(s + 1) * TK_C]
        acc_s[idx] = alpha * acc_s[idx] + _dot(vt, p.astype(BF16))
        m_s[idx] = m_new

    @pl.when(ki == pl.num_programs(3) - 1)
    def _():
        _attn_c_finalize(lam_ref, g_ref, o_ref, acc_s)


def _attn_c_finalize(lam_ref, g_ref, o_ref, acc_s):
    lam = lam_ref[0]
    for hh in range(2):
        a0 = acc_s[2 * hh]
        a1 = acc_s[2 * hh + 1]
        o = (a0[0:DIFF_V_DIM] * (1.0 / a0[DIFF_V_DIM:DIFF_V_DIM + 1])
             - lam * (a1[0:DIFF_V_DIM] * (1.0 / a1[DIFF_V_DIM:DIFF_V_DIM + 1])))
        ms = jnp.mean(o * o, axis=0, keepdims=True)
        rows = slice(hh * DIFF_V_DIM, (hh + 1) * DIFF_V_DIM)
        o_ref[rows, :] = (o * lax.rsqrt(ms + EPS) * g_ref[rows, :]).astype(BF16)


AUG_ROWS = 16
AUG_LANES = 64
BOUND_SLACK = 1.02
BOUND_LIMIT = 100.0


def _split3(x):
    hi = x.astype(BF16).astype(F32)
    r = x - hi
    mid = r.astype(BF16).astype(F32)
    return hi, mid, (r - mid).astype(BF16).astype(F32)


def _attn_c_bounded_kernel(slope_ref, lam_ref, kmax_ref, qt_ref, k_ref, vt_ref, dtab_ref, g_ref, o_ref,
                           rhs_s, mref_s, acc_s):
    hp = pl.program_id(1)
    qi = pl.program_id(2)
    ki = pl.program_id(3)
    units = [(s, hh, c) for s in range(KV_SUB) for hh in range(2) for c in range(2)]

    @pl.when((pl.program_id(0) == 0) & (hp == 0) & (qi == 0) & (ki == 0))
    def _():
        rhs_s[...] = jnp.zeros_like(rhs_s)

    @pl.when(ki == 0)
    def _():
        acc_s[...] = jnp.zeros_like(acc_s)
        for u, (s, hh, c) in enumerate(units):
            idx = 2 * hh + c
            qseg = qt_ref[idx * DIFF_QK_DIM:(idx + 1) * DIFF_QK_DIM, :]
            rhs_s[u, c * DIFF_QK_DIM:(c + 1) * DIFF_QK_DIM, :] = qseg
            if s == 0:
                qf = qseg.astype(F32)
                kmax = kmax_ref[0] * BOUND_SLACK
                mref_s[idx] = jnp.sqrt(jnp.sum(qf * qf, axis=0, keepdims=True)) * kmax

    qloc = lax.broadcasted_iota(jnp.int32, (1, TQ_C), 1).astype(F32)
    row = lax.broadcasted_iota(jnp.int32, (AUG_ROWS, TQ_C), 0)
    grp = row // 3
    part = row - 3 * grp

    def scores(u):
        s, hh, c = units[u]
        kt = ki * KV_SUB + s
        h = 2 * hp + hh
        sgn = jnp.where(kt < qi, 1.0, jnp.where(kt == qi, 0.0, -1.0)).astype(F32)
        slope = slope_ref[4 * h + 3]
        tile_dist = jnp.abs(qi * TQ_C - kt * TK_C).astype(F32)
        shift = mref_s[2 * hh + c] + slope * tile_dist + (sgn * slope) * qloc
        hi, mid, lo = _split3(shift)
        shift_part = jnp.where(part == 0, hi, jnp.where(part == 1, mid, lo))
        slope_part = sgn * jnp.where(part == 0, slope_ref[4 * h], jnp.where(part == 1, slope_ref[4 * h + 1],
                                                                           slope_ref[4 * h + 2]))
        aug = jnp.where(grp == 0, -shift_part, jnp.where(grp <= 2, slope_part, 0.0))
        rhs_s[u, 2 * DIFF_QK_DIM:2 * DIFF_QK_DIM + AUG_ROWS, :] = aug.astype(BF16)
        on_diag = (kt == qi).astype(jnp.int32)
        return _dot(k_ref[s * TK_C:(s + 1) * TK_C, hh * LANES:(hh + 1) * LANES], rhs_s[u]) + dtab_ref[on_diag, hh]

    t_next = scores(0)
    for u, (s, hh, c) in enumerate(units):
        t = t_next
        if u + 1 < len(units):
            t_next = scores(u + 1)
        idx = 2 * hh + c
        vt = vt_ref[hh * VT_ROWS:(hh + 1) * VT_ROWS, s * TK_C:(s + 1) * TK_C]
        acc_s[idx] = acc_s[idx] + _dot(vt, jnp.exp2(t).astype(BF16))

    @pl.when(ki == pl.num_programs(3) - 1)
    def _():
        _attn_c_finalize(lam_ref, g_ref, o_ref, acc_s)


def _attn_c_bias(slopes):
    rel = (np.arange(TQ_C)[None, :] - np.arange(TK_C)[:, None]).astype(np.float32)
    tables = jnp.asarray(np.stack([-rel, -np.abs(rel), rel]))
    s = slopes.reshape(DIFF_HEADS // 2, 1, 2, 1, 1)
    return s * tables[None, :, None]


def _attn_c_call(body, prefetch, tensors, table_variants, scratch, batch, seq, name):
    nq = seq // TQ_C
    kv_step = TK_C * KV_SUB
    nk = seq // kv_step
    n_pre = len(prefetch)
    grid_spec = pltpu.PrefetchScalarGridSpec(
        num_scalar_prefetch=n_pre,
        grid=(batch, DIFF_HEADS // 2, nq, nk),
        in_specs=[
            pl.BlockSpec((None, LANES, TQ_C), lambda b, h, i, j, *_: (b, h, i)),
            pl.BlockSpec((kv_step, 2 * LANES), lambda b, h, i, j, *_: (b * nk + j, h)),
            pl.BlockSpec((None, 2 * VT_ROWS, kv_step), lambda b, h, i, j, *_: (b, h, j)),
            pl.BlockSpec((None, table_variants, 2, TK_C, TQ_C), lambda b, h, i, j, *_: (h, 0, 0, 0, 0)),
            _const_spec((LANES, TQ_C)),
        ],
        out_specs=pl.BlockSpec((None, LANES, TQ_C), lambda b, h, i, j, *_: (b, h, i)),
        scratch_shapes=scratch + [pltpu.VMEM((N_MAPS, 1, TQ_C), F32), pltpu.VMEM((N_MAPS, VT_ROWS, TQ_C), F32)],
    )
    return pl.pallas_call(
        body,
        grid_spec=grid_spec,
        out_shape=jax.ShapeDtypeStruct((batch, WIDTH_C, seq), BF16),
        compiler_params=_cparams(("arbitrary", "arbitrary", "arbitrary", "arbitrary")),
        name=name,
    )(*prefetch, *tensors)


def _attn_c(qt, k_aug, vt, qk_gains, slopes, lam, gain, batch, seq):
    kv_step = TK_C * KV_SUB
    assert seq % kv_step == 0 and seq % TQ_C == 0
    gain_b = jnp.broadcast_to(gain.reshape(LANES, 1), (LANES, TQ_C))
    tables = _attn_c_bias(slopes)
    seg_bound = lambda g: math.sqrt(DIFF_QK_DIM) * jnp.max(jnp.abs(g))
    qmax = seg_bound(qk_gains[:, 0:256])
    kmax = seg_bound(qk_gains[:, 256:512])

    def online():
        scratch = [pltpu.VMEM((N_MAPS, LANES, TQ_C), BF16)]
        return _attn_c_call(_attn_c_kernel, (slopes, lam), (qt, k_aug, vt, tables, gain_b), 3, scratch,
                            batch, seq, "attn_c_online")

    def bounded():
        slope4 = jnp.stack(_split3(slopes) + (slopes,), axis=1).reshape(-1)
        dtab = jnp.stack([jnp.zeros_like(tables[:, 1]), tables[:, 1]], axis=1)
        scratch = [pltpu.VMEM((N_MAPS * KV_SUB, LANES, TQ_C), BF16)]
        return _attn_c_call(_attn_c_bounded_kernel, (slope4, lam, jnp.reshape(kmax, (1,))),
                            (qt, k_aug, vt, dtab, gain_b), 2, scratch, batch, seq, "attn_c_bounded")

    safe = 2.0 * BOUND_SLACK * BOUND_SLACK * qmax * kmax < BOUND_LIMIT
    return lax.cond(safe, bounded, online)


def _out_proj(oat_ref, ob_ref, oct_ref, x_ref, wo_ref):
    untr = lambda ref: ref[...].astype(F32).T.astype(BF16)
    o = jnp.concatenate([untr(oat_ref), ob_ref[...].astype(BF16), untr(oct_ref)], axis=-1)
    return x_ref[...] + _dot(o, wo_ref[...])


def _rms(x, g_ref):
    ms = jnp.mean(x * x, axis=-1, keepdims=True)
    return x * lax.rsqrt(ms + EPS) * g_ref[...]


def _swiglu_acc(h, wg_ref, wu_ref, wd_ref, acc, d_ff, chunks):
    step = d_ff // chunks
    for c0 in range(0, d_ff, step):
        gate = _dot(h, wg_ref[:, c0:c0 + step])
        up = _dot(h, wu_ref[:, c0:c0 + step])
        act = (gate * _sigmoid(gate) * up).astype(BF16)
        acc = acc + _dot(act, wd_ref[c0:c0 + step, :])
    return acc


def _out_ffn_kernel(oa_ref, ob_ref, oc_ref, x_ref, wo_ref, g_ref, wg_ref, wu_ref, wd_ref, out_ref):
    x1 = _out_proj(oa_ref, ob_ref, oc_ref, x_ref, wo_ref)
    h = _rms(x1, g_ref).astype(BF16)
    out_ref[...] = _swiglu_acc(h, wg_ref, wu_ref, wd_ref, x1, D_FF, FFN_CHUNKS)


def _mixer_out_specs(seq):
    nps = seq // TM
    row = lambda w: pl.BlockSpec((TM, w), lambda i: (i, 0))
    tr = lambda r: pl.BlockSpec((None, r, TM), lambda i: (i // nps, 0, i % nps))
    return [tr(WIDTH_A), row(D_INNER), tr(WIDTH_C), row(D_MODEL), _resident_spec((D_MODEL, D_MODEL)),
            _const_spec((1, D_MODEL))]


def _out_ffn(oa, ob, oc, x, w_out, g, wg, wu, wd):
    t = x.shape[0]
    return pl.pallas_call(
        _out_ffn_kernel,
        grid=(t // TM,),
        in_specs=_mixer_out_specs(oa.shape[2]) + [_resident_spec((D_MODEL, D_FF)), _resident_spec((D_MODEL, D_FF)),
                                       _resident_spec((D_FF, D_MODEL))],
        out_specs=pl.BlockSpec((TM, D_MODEL), lambda i: (i, 0)),
        out_shape=jax.ShapeDtypeStruct((t, D_MODEL), F32),
        compiler_params=_cparams(("parallel",)),
        name="out_ffn",
    )(oa, ob, oc, x, w_out, g, wg, wu, wd)


def _out_router_kernel(oa_ref, ob_ref, oc_ref, x_ref, wo_ref, g_ref, wr_hi_ref, wr_lo_ref,
                       x1_ref, h_ref, sel_ref, gate_ref):
    x1 = _out_proj(oa_ref, ob_ref, oc_ref, x_ref, wo_ref)
    x1_ref[...] = x1
    h = _rms(x1, g_ref)
    h_hi = h.astype(BF16)
    h_ref[...] = h_hi
    h_lo = (h - h_hi.astype(F32)).astype(BF16)
    logits = _dot(h_hi, wr_hi_ref[...]) + (_dot(h_lo, wr_hi_ref[...]) + _dot(h_hi, wr_lo_ref[...]))
    lt = logits.T[0:N_EXPERTS, :]
    row_i = lax.broadcasted_iota(jnp.int32, (N_EXPERTS, TM), 0)
    row = row_i.astype(F32)
    v1 = jnp.max(lt, axis=0, keepdims=True)
    i1 = jnp.min(jnp.where(lt == v1, row, float(N_EXPERTS)), axis=0, keepdims=True)
    rest = jnp.where(row == i1, -jnp.inf, lt)
    v2 = jnp.max(rest, axis=0, keepdims=True)
    i2 = jnp.min(jnp.where(rest == v2, row, float(N_EXPERTS)), axis=0, keepdims=True)
    e = jnp.exp(v2 - v1)
    g1 = 1.0 / (1.0 + e)
    sel_ref[...] = jnp.where(row_i == 0, i1, jnp.where(row_i == 1, i2, 0.0)).astype(jnp.int32)
    gate_ref[...] = jnp.where(row_i == 0, g1, jnp.where(row_i == 1, e * g1, 0.0))


def _out_router(oa, ob, oc, x, w_out, g, wr_hi, wr_lo):
    t = x.shape[0]
    row = lambda w: pl.BlockSpec((TM, w), lambda i: (i, 0))
    col = pl.BlockSpec((N_EXPERTS, TM), lambda i: (0, i))
    return pl.pallas_call(
        _out_router_kernel,
        grid=(t // TM,),
        in_specs=_mixer_out_specs(oa.shape[2]) + [_const_spec((D_MODEL, LANES)), _const_spec((D_MODEL, LANES))],
        out_specs=[row(D_MODEL), row(D_MODEL), col, col],
        out_shape=[jax.ShapeDtypeStruct((t, D_MODEL), F32), jax.ShapeDtypeStruct((t, D_MODEL), BF16),
                   jax.ShapeDtypeStruct((N_EXPERTS, t), jnp.int32), jax.ShapeDtypeStruct((N_EXPERTS, t), F32)],
        compiler_params=_cparams(("parallel",)),
        name="out_router",
    )(oa, ob, oc, x, w_out, g, wr_hi, wr_lo)


def _moe_ffn_kernel(blk_exp_ref, n_used_ref, x_ref, wg_ref, wu_ref, wd_ref, out_ref):
    i = pl.program_id(0)

    @pl.when(i < n_used_ref[0])
    def _():
        acc = jnp.zeros((MOE_BLK, D_MODEL), F32)
        out_ref[...] = _swiglu_acc(x_ref[...], wg_ref.at[0], wu_ref.at[0], wd_ref.at[0], acc,
                                   D_FF_EXPERT, MOE_CHUNKS).astype(BF16)

    @pl.when(i >= n_used_ref[0])
    def _():
        out_ref[...] = jnp.zeros_like(out_ref)


def _moe_ffn(xg, blk_exp, n_used, wg, wu, wd):
    rows = xg.shape[0]
    wspec = lambda shape: pl.BlockSpec((1,) + shape, lambda i, be, nu: (be[i], 0, 0),
                                       pipeline_mode=pl.Buffered(1))
    grid_spec = pltpu.PrefetchScalarGridSpec(
        num_scalar_prefetch=2,
        grid=(rows // MOE_BLK,),
        in_specs=[pl.BlockSpec((MOE_BLK, D_MODEL), lambda i, be, nu: (i, 0)),
                  wspec((D_MODEL, D_FF_EXPERT)), wspec((D_MODEL, D_FF_EXPERT)),
                  wspec((D_FF_EXPERT, D_MODEL))],
        out_specs=pl.BlockSpec((MOE_BLK, D_MODEL), lambda i, be, nu: (i, 0)),
    )
    return pl.pallas_call(
        _moe_ffn_kernel,
        grid_spec=grid_spec,
        out_shape=jax.ShapeDtypeStruct((rows, D_MODEL), BF16),
        compiler_params=_cparams(("arbitrary",)),
        name="moe_ffn",
    )(blk_exp, n_used, xg, wg, wu, wd)


def _moe(x1, h, sel, gates, wg, wu, wd):
    t = x1.shape[0]
    i1, i2 = sel[0], sel[1]
    g1, g2 = gates[0][:, None], gates[1][:, None]
    experts = jnp.arange(N_EXPERTS, dtype=jnp.int32)
    onehot = ((i1[:, None] == experts) | (i2[:, None] == experts)).astype(jnp.int32)
    rank = jnp.cumsum(onehot, axis=0) - onehot
    counts = jnp.sum(onehot, axis=0)
    padded = ((counts + MOE_BLK - 1) // MOE_BLK) * MOE_BLK
    pad_end = jnp.cumsum(padded)
    pad_start = pad_end - padded
    dest = pad_start[None, :] + rank
    d1 = jnp.take_along_axis(dest, i1[:, None], axis=1)[:, 0]
    d2 = jnp.take_along_axis(dest, i2[:, None], axis=1)[:, 0]
    n_blocks = (2 * t) // MOE_BLK + N_EXPERTS
    rows = n_blocks * MOE_BLK
    blk_start = jnp.arange(n_blocks, dtype=jnp.int32) * MOE_BLK
    blk_exp = jnp.minimum(jnp.sum((pad_end[None, :] <= blk_start[:, None]).astype(jnp.int32), axis=1),
                          N_EXPERTS - 1)
    order = jnp.argsort(jnp.concatenate([d1, d2])).astype(jnp.int32)
    row_exp = jnp.repeat(blk_exp, MOE_BLK)
    in_exp = jnp.arange(rows, dtype=jnp.int32) - pad_start[row_exp]
    starts = jnp.cumsum(counts) - counts
    packed = jnp.minimum(starts[row_exp] + in_exp, 2 * t - 1)
    row_tok = jnp.where(in_exp < counts[row_exp], order[packed] % t, 0)
    n_used = (pad_end[-1:] // MOE_BLK).astype(jnp.int32)
    yb = _moe_ffn(h[row_tok], blk_exp, n_used, wg, wu, wd)
    return x1 + g1 * yb[d1].astype(F32) + g2 * yb[d2].astype(F32)


def _split_cols(w):
    outs, start = [], 0
    for n in SPLIT_SIZES:
        outs.append(w[:, start:start + n])
        start += n
    return outs


def _layer_params(l, p):
    qa, ka, va, zb, xbc, dtb, qc, kc, vc = _split_cols(p["w_in"][l])
    w_cat = jnp.concatenate(
        [qa, ka, va, xbc, zb, qc, kc, vc, dtb, jnp.zeros((D_MODEL, DT_W - 2 * SSD_HEADS), F32)],
        axis=1).astype(BF16)
    ga = jnp.concatenate([jnp.tile(p["qnorm_a"][l] * (HEAD_DIM_A ** -0.5 * LOG2E), N_HEADS_A),
                          jnp.tile(p["knorm_a"][l], KV_HEADS_A)])[None, :]
    gc = jnp.concatenate([jnp.tile(p["qnorm_c"][l] * (DIFF_QK_DIM ** -0.5 * LOG2E), 2 * DIFF_HEADS),
                          jnp.tile(p["knorm_c"][l], 2 * DIFF_HEADS)])[None, :]
    lam_init = 0.8 - 0.6 * math.exp(-0.3 * l)
    lc = p["lam_c"][l].astype(F32)
    lam = jnp.exp(jnp.sum(lc[0] * lc[1])) - jnp.exp(jnp.sum(lc[2] * lc[3])) + lam_init
    out = {
        "norm_mix_g": p["norm_mix_g"][l][None, :], "w_cat": w_cat, "ga": ga, "gc": gc,
        "sink": p["sink_a"][l],
        "ssd": {
            "conv_w": p["conv_w"][l], "conv_b": p["conv_b"][l][None, :],
            "dtb_c": p["dt_bias"][l].reshape(1, -1), "dtb_r": p["dt_bias"][l].reshape(-1, 1),
            "alog_c": p["a_log"][l].reshape(1, -1), "alog_r": p["a_log"][l].reshape(-1, 1),
            "dskip": jnp.repeat(p["d_skip"][l], SSD_HEAD_DIM)[None, :],
            "norm_g": p["ssd_norm_g"][l][None, :],
        },
        "slopes_c": jnp.exp2(-8.0 * jnp.arange(1, DIFF_HEADS + 1, dtype=F32) / DIFF_HEADS) * LOG2E,
        "lam": jnp.reshape(lam, (1,)).astype(F32),
        "gain_c": (jnp.tile(p["diff_norm_g"][l], 2) * (1.0 - lam_init))[None, :],
        "w_out": p["w_out"][l].astype(BF16),
        "norm_ffn_g": p["norm_ffn_g"][l][None, :],
    }
    i = l // 2
    if l % 2 == 0:
        out["ffn"] = (p["ffn_w_gate"][i].astype(BF16), p["ffn_w_up"][i].astype(BF16),
                      p["ffn_w_down"][i].astype(BF16))
    else:
        wr = jnp.pad(p["router_w"][i], ((0, 0), (0, LANES - N_EXPERTS)))
        wr_hi = wr.astype(BF16)
        out["router"] = (wr_hi, (wr - wr_hi.astype(F32)).astype(BF16))
        out["moe"] = (p["moe_w_gate"][i].astype(BF16), p["moe_w_up"][i].astype(BF16),
                      p["moe_w_down"][i].astype(BF16))
    return out


def _trunk(x3, layers):
    batch, seq, _ = x3.shape
    x = x3.reshape(batch * seq, D_MODEL)
    for l, lp in enumerate(layers):
        qat, ka, vat, pb, qct, kc, vct, dt, dtt = _norm_proj(x, lp["norm_mix_g"], lp["w_cat"], lp["ga"], lp["gc"],
                                                             batch, seq)
        oa = _attn_a(qat, ka, vat, lp["sink"], batch, seq)
        yf = _ssd(pb, dt, dtt, lp["ssd"], None, batch, seq)
        ob = _ssd(pb, dt, dtt, lp["ssd"], yf, batch, seq)
        oc = _attn_c(qct, kc, vct, lp["gc"], lp["slopes_c"], lp["lam"], lp["gain_c"], batch, seq)
        if l % 2 == 0:
            x = _out_ffn(oa, ob, oc, x, lp["w_out"], lp["norm_ffn_g"], *lp["ffn"])
        else:
            x1, h, sel, gates = _out_router(oa, ob, oc, x, lp["w_out"], lp["norm_ffn_g"], *lp["router"])
            x = _moe(x1, h, sel, gates, *lp["moe"])
    return x.reshape(batch, seq, D_MODEL)


def kernel(x_prompt, x_sample, norm_mix_g, w_in, qnorm_a, knorm_a, sink_a, conv_w, conv_b, dt_bias, a_log,
           d_skip, ssd_norm_g, qnorm_c, knorm_c, lam_c, diff_norm_g, w_out, norm_ffn_g, ffn_w_gate, ffn_w_up,
           ffn_w_down, router_w, moe_w_gate, moe_w_up, moe_w_down):
    p = dict(norm_mix_g=norm_mix_g, w_in=w_in, qnorm_a=qnorm_a, knorm_a=knorm_a, sink_a=sink_a,
             conv_w=conv_w, conv_b=conv_b, dt_bias=dt_bias, a_log=a_log, d_skip=d_skip,
             ssd_norm_g=ssd_norm_g, qnorm_c=qnorm_c, knorm_c=knorm_c, lam_c=lam_c, diff_norm_g=diff_norm_g,
             w_out=w_out, norm_ffn_g=norm_ffn_g, ffn_w_gate=ffn_w_gate, ffn_w_up=ffn_w_up,
             ffn_w_down=ffn_w_down, router_w=router_w, moe_w_gate=moe_w_gate, moe_w_up=moe_w_up,
             moe_w_down=moe_w_down)
    layers = [_layer_params(l, p) for l in range(DEPTH)]
    return _trunk(x_prompt, layers), _trunk(x_sample, layers)
```

```python
import functools
import math

import numpy as np
import jax
import jax.numpy as jnp
from jax import lax
from jax.experimental import pallas as pl
from jax.experimental.pallas import tpu as pltpu

F32 = jnp.float32
BF16 = jnp.bfloat16

D_MODEL = 1024
DEPTH = 2
N_HEADS_A = 8
KV_HEADS_A = 2
HEAD_DIM_A = 64
WINDOW = 128
ATT_BLOCK = 128
SSD_HEADS = 4
SSD_HEAD_DIM = 64
D_INNER = SSD_HEADS * SSD_HEAD_DIM
SSD_GROUPS = 2
D_STATE = 64
CONV_K = 3
CHUNK = 128
DIFF_HEADS = 4
DIFF_QK_DIM = 32
DIFF_V_DIM = 64
WIDTH_A = N_HEADS_A * HEAD_DIM_A
WIDTH_C = DIFF_HEADS * DIFF_V_DIM
CONV_DIM = D_INNER + 2 * SSD_GROUPS * D_STATE
SPLIT_SIZES = (WIDTH_A, KV_HEADS_A * HEAD_DIM_A, KV_HEADS_A * HEAD_DIM_A,
               D_INNER, CONV_DIM, 2 * SSD_HEADS,
               DIFF_HEADS * 2 * DIFF_QK_DIM, DIFF_HEADS * 2 * DIFF_QK_DIM, WIDTH_C)
D_FF = 2816
N_EXPERTS = 8
D_FF_EXPERT = 3584
EPS = 1e-6

LOG2E = 1.4426950408889634
NEG_BIG = -1e30

LANES = 128
VMEM_LIMIT = 56 * 1024 * 1024

PA_W = 768
PA_NORM_W = 640
VT_ROWS = 80
PB_W = 768
PC_W = 768
DT_W = 128
PROJ_W = PA_W + PB_W + PC_W + DT_W

TM = 512
MOE_BLK = 512
FFN_CHUNKS = 1
MOE_CHUNKS = 2


def _cparams(sem, vmem=VMEM_LIMIT):
    return pltpu.CompilerParams(dimension_semantics=sem, vmem_limit_bytes=vmem)


def _const_spec(shape):
    nd = len(shape)
    return pl.BlockSpec(shape, lambda *_: (0,) * nd)


def _resident_spec(shape):
    nd = len(shape)
    return pl.BlockSpec(shape, lambda *_: (0,) * nd, pipeline_mode=pl.Buffered(1))


def _block_diag(n, seg):
    idx = np.arange(n) // seg
    return jnp.asarray((idx[:, None] == idx[None, :]).astype(np.float32), dtype=BF16)


def _sigmoid(x):
    return 1.0 / (1.0 + jnp.exp(-x))


def _softplus(x):
    return jnp.maximum(x, 0.0) + jnp.log(1.0 + jnp.exp(-jnp.abs(x)))


def _dot(a, b):
    return jnp.dot(a, b, preferred_element_type=F32)


def _dot_nt(a, b):
    return lax.dot_general(a, b, (((1,), (1,)), ((), ())), preferred_element_type=F32)


def _dot_tn(a, b):
    return lax.dot_general(a, b, (((0,), (0,)), ((), ())), preferred_element_type=F32)


def _dot_hi(a, b):
    return jnp.dot(a, b, preferred_element_type=F32, precision=lax.Precision.HIGHEST)


def _seg_norm(blk, seg, bd_ref, gain):
    w = blk.shape[1]
    ss = _dot((blk * blk).astype(BF16), bd_ref[0:w, 0:w]) * (1.0 / seg)
    return blk * lax.rsqrt(ss + EPS) * gain


def _store_vt(vt_ref, v, heads, head_dim, pad_ref):
    vt = v.T
    for h in range(heads):
        r0 = h * VT_ROWS
        vt_ref[r0:r0 + head_dim, :] = vt[h * head_dim:(h + 1) * head_dim].astype(BF16)
        vt_ref[r0 + head_dim:r0 + VT_ROWS, :] = pad_ref[...]


def _norm_proj_kernel(x_ref, g_ref, w_ref, bd64_ref, bd32_ref, ga_ref, gc_ref, pad_ref, kaug_ref,
                      qat_ref, ka_ref, vat_ref, pb_ref, qct_ref, kc_ref, vct_ref, dt_ref, dtt_ref):
    x = x_ref[...]
    ms = jnp.mean(x * x, axis=-1, keepdims=True)
    h = (x * lax.rsqrt(ms + EPS) * g_ref[...]).astype(BF16)
    ya = _dot(h, w_ref[:, 0:PA_W])
    for c0 in range(0, WIDTH_A, 256):
        qn = _seg_norm(ya[:, c0:c0 + 256], HEAD_DIM_A, bd64_ref, ga_ref[:, c0:c0 + 256])
        qat_ref[c0:c0 + 256, :] = qn.T.astype(BF16)
    ka_ref[...] = _seg_norm(ya[:, WIDTH_A:PA_NORM_W], HEAD_DIM_A, bd64_ref, ga_ref[:, WIDTH_A:PA_NORM_W]).astype(BF16)
    _store_vt(vat_ref, ya[:, PA_NORM_W:PA_W], KV_HEADS_A, HEAD_DIM_A, pad_ref)
    pb_ref[...] = _dot(h, w_ref[:, PA_W:PA_W + PB_W]).astype(BF16)
    yc = _dot(h, w_ref[:, PA_W + PB_W:PA_W + PB_W + PC_W])
    qct_ref[...] = _seg_norm(yc[:, 0:256], DIFF_QK_DIM, bd32_ref, gc_ref[:, 0:256]).T.astype(BF16)
    kc = _seg_norm(yc[:, 256:512], DIFF_QK_DIM, bd32_ref, gc_ref[:, 256:512]).astype(BF16)
    aug = kaug_ref[...]
    pieces = []
    for hd in range(DIFF_HEADS):
        pieces += [kc[:, hd * 2 * DIFF_QK_DIM:(hd + 1) * 2 * DIFF_QK_DIM], aug]
    kc_ref[...] = jnp.concatenate(pieces, axis=1)
    _store_vt(vct_ref, yc[:, 512:PC_W], DIFF_HEADS, DIFF_V_DIM, pad_ref)
    yd = _dot(h, w_ref[:, PA_W + PB_W + PC_W:PROJ_W])
    dt_ref[...] = yd
    dtt_ref[...] = yd.T[0:2 * SSD_HEADS, :]


def _norm_proj(x, g, w_cat, ga, gc, batch, seq):
    t = batch * seq
    nps = seq // TM
    row = lambda w: pl.BlockSpec((TM, w), lambda i: (i, 0))
    tr = lambda r: pl.BlockSpec((None, r, TM), lambda i: (i // nps, 0, i % nps))
    pad = jnp.zeros((VT_ROWS - HEAD_DIM_A, TM), BF16).at[0].set(1.0)
    pos = np.arange(TM) % TK_C
    cols = np.zeros((TM, AUG_LANES), np.float32)
    cols[:, 0:3] = 1.0
    cols[:, 3:6] = (pos // 256 * 256)[:, None]
    cols[:, 6:9] = (pos % 256)[:, None]
    return pl.pallas_call(
        _norm_proj_kernel,
        grid=(t // TM,),
        in_specs=[row(D_MODEL), _const_spec((1, D_MODEL)), _resident_spec((D_MODEL, PROJ_W)),
                  _const_spec((256, 256)), _const_spec((256, 256)),
                  _const_spec((1, PA_NORM_W)), _const_spec((1, 512)),
                  _const_spec((VT_ROWS - HEAD_DIM_A, TM)), _const_spec((TM, AUG_LANES))],
        out_specs=[tr(WIDTH_A), row(LANES), tr(KV_HEADS_A * VT_ROWS), row(PB_W),
                   tr(256), row(DIFF_HEADS * LANES), tr(DIFF_HEADS * VT_ROWS), row(DT_W),
                   pl.BlockSpec((2 * SSD_HEADS, TM), lambda i: (0, i))],
        out_shape=[jax.ShapeDtypeStruct((batch, WIDTH_A, seq), BF16), jax.ShapeDtypeStruct((t, LANES), BF16),
                   jax.ShapeDtypeStruct((batch, KV_HEADS_A * VT_ROWS, seq), BF16),
                   jax.ShapeDtypeStruct((t, PB_W), BF16),
                   jax.ShapeDtypeStruct((batch, 256, seq), BF16),
                   jax.ShapeDtypeStruct((t, DIFF_HEADS * LANES), BF16),
                   jax.ShapeDtypeStruct((batch, DIFF_HEADS * VT_ROWS, seq), BF16),
                   jax.ShapeDtypeStruct((t, DT_W), F32), jax.ShapeDtypeStruct((2 * SSD_HEADS, t), F32)],
        compiler_params=_cparams(("parallel",)),
        name="norm_proj",
    )(x, g, w_cat, _block_diag(256, HEAD_DIM_A), _block_diag(256, DIFF_QK_DIM), ga, gc, pad,
      jnp.asarray(cols, dtype=BF16))


QB_A = 2048


GQA = N_HEADS_A // KV_HEADS_A
WIN_A = 3 * ATT_BLOCK
NQ_A = GQA * ATT_BLOCK


def _attn_a_kernel(qt_ref, k_ref, kp_ref, kn_ref, vt_ref, vtp_ref, vtn_ref, bias_ref, sink_ref, o_ref,
                   k_s, vt_s, rhs_s, *, n_blocks):
    i = pl.program_id(1)
    nsub = QB_A // ATT_BLOCK
    b = ATT_BLOCK
    k_s[0:b] = kp_ref[...]
    k_s[b:b + QB_A] = k_ref[...]
    k_s[b + QB_A:] = kn_ref[...]
    vt_s[:, 0:b] = vtp_ref[...]
    vt_s[:, b:b + QB_A] = vt_ref[...]
    vt_s[:, b + QB_A:] = vtn_ref[...]
    units = [(t, j) for t in range(nsub) for j in range(KV_HEADS_A)]

    def scores(u):
        t, j = units[u]
        other = 1 - j
        rhs_s[u, other * HEAD_DIM_A:(other + 1) * HEAD_DIM_A, :] = jnp.zeros((HEAD_DIM_A, NQ_A), BF16)
        for g in range(GQA):
            h = GQA * j + g
            rhs_s[u, j * HEAD_DIM_A:(j + 1) * HEAD_DIM_A, g * b:(g + 1) * b] = (
                qt_ref[h * HEAD_DIM_A:(h + 1) * HEAD_DIM_A, t * b:(t + 1) * b])
        gb = i * nsub + t
        var = jnp.where(gb == 0, 0, jnp.where(gb == n_blocks - 1, 2, 1))
        return _dot(k_s[t * b:t * b + WIN_A, :], rhs_s[u]) + bias_ref[var, j]

    s_next = scores(0)
    for u, (t, j) in enumerate(units):
        s = s_next
        if u + 1 < len(units):
            s_next = scores(u + 1)
        sk = sink_ref[j]
        m = jnp.maximum(jnp.max(s, axis=0, keepdims=True), sk)
        p = jnp.exp2(s - m)
        r = _dot(vt_s[j * VT_ROWS:(j + 1) * VT_ROWS, t * b:t * b + WIN_A], p.astype(BF16))
        den = r[HEAD_DIM_A:HEAD_DIM_A + 1] + jnp.exp2(sk - m)
        o = (r[0:HEAD_DIM_A] * (1.0 / den)).astype(BF16)
        for g in range(GQA):
            h = GQA * j + g
            o_ref[h * HEAD_DIM_A:(h + 1) * HEAD_DIM_A, t * b:(t + 1) * b] = o[:, g * b:(g + 1) * b]


def _attn_a_consts(sink):
    b = ATT_BLOCK
    qi = np.arange(b)[None, :]
    s = np.arange(WIN_A)[:, None]
    dist = np.abs(b + qi - s).astype(np.float32)
    in_win = dist <= WINDOW
    slopes = jnp.exp2(-8.0 * jnp.arange(1, N_HEADS_A + 1, dtype=F32) / N_HEADS_A) * LOG2E
    alibi = -slopes.reshape(KV_HEADS_A, 1, GQA, 1) * jnp.asarray(dist)[None, :, None, :]
    variants = []
    for valid in (s >= b, s >= 0, s < 2 * b):
        mask = jnp.asarray(in_win & valid)[None, :, None, :]
        variants.append(jnp.where(mask, alibi, NEG_BIG).reshape(KV_HEADS_A, WIN_A, NQ_A))
    bias = jnp.stack(variants)
    sink_cols = jnp.repeat(sink.astype(F32).reshape(KV_HEADS_A, GQA) * LOG2E, b, axis=1)[:, None, :]
    return bias, sink_cols


def _attn_a(qt, ka, vt, sink, batch, seq):
    nq = seq // QB_A
    nblk = seq // ATT_BLOCK
    per = QB_A // ATT_BLOCK
    bias, sink_cols = _attn_a_consts(sink)
    kcol = 0
    prev = lambda i: jnp.maximum(i * per - 1, 0)
    nxt = lambda i: jnp.minimum((i + 1) * per, nblk - 1)
    vrows = KV_HEADS_A * VT_ROWS
    return pl.pallas_call(
        functools.partial(_attn_a_kernel, n_blocks=nblk),
        grid=(batch, nq),
        in_specs=[
            pl.BlockSpec((None, WIDTH_A, QB_A), lambda b, i: (b, 0, i)),
            pl.BlockSpec((QB_A, LANES), lambda b, i: (b * nq + i, kcol)),
            pl.BlockSpec((ATT_BLOCK, LANES), lambda b, i: (b * nblk + prev(i), kcol)),
            pl.BlockSpec((ATT_BLOCK, LANES), lambda b, i: (b * nblk + nxt(i), kcol)),
            pl.BlockSpec((None, vrows, QB_A), lambda b, i: (b, 0, i)),
            pl.BlockSpec((None, vrows, ATT_BLOCK), lambda b, i: (b, 0, prev(i))),
            pl.BlockSpec((None, vrows, ATT_BLOCK), lambda b, i: (b, 0, nxt(i))),
            _const_spec((3, KV_HEADS_A, WIN_A, NQ_A)),
            _const_spec((KV_HEADS_A, 1, NQ_A)),
        ],
        out_specs=pl.BlockSpec((None, WIDTH_A, QB_A), lambda b, i: (b, 0, i)),
        out_shape=jax.ShapeDtypeStruct((batch, WIDTH_A, seq), BF16),
        scratch_shapes=[pltpu.VMEM((QB_A + 2 * ATT_BLOCK, LANES), BF16),
                        pltpu.VMEM((vrows, QB_A + 2 * ATT_BLOCK), BF16),
                        pltpu.VMEM((per * KV_HEADS_A, LANES, NQ_A), BF16)],
        compiler_params=_cparams(("parallel", "parallel")),
        name="attn_a",
    )(qt, ka, ka, ka, vt, vt, vt, bias, sink_cols)


RB_S = 512
HALO = 16


def _ssd_kernel(*refs, rev, n_steps):
    if rev:
        (cur_ref, prev_ref, next_ref, dt_ref, dtt_ref, cw_ref, cb_ref, dtb_c_ref, dtb_r_ref,
         alog_c_ref, alog_r_ref, ltri_ref, utri_ref, dskip_ref, g_ref, yf_ref, out_ref, st_ref) = refs
    else:
        (cur_ref, prev_ref, next_ref, dt_ref, dtt_ref, cw_ref, cb_ref, dtb_c_ref, dtb_r_ref,
         alog_c_ref, alog_r_ref, ltri_ref, utri_ref, out_ref, st_ref) = refs
    i = pl.program_id(1)
    ii = (n_steps - 1 - i) if rev else i
    d = 1 if rev else 0

    @pl.when(i == 0)
    def _():
        st_ref[...] = jnp.zeros_like(st_ref)

    xc = cur_ref[:, 0:CONV_DIM].astype(F32)
    prow = jnp.where(ii == 0, 0.0, prev_ref[...].astype(F32)[HALO - 1:HALO, :])
    nrow = jnp.where(ii == n_steps - 1, 0.0, next_ref[...].astype(F32)[0:1, :])
    rid = lax.broadcasted_iota(jnp.int32, (RB_S, CONV_DIM), 0)
    xm1 = jnp.where(rid == 0, prow, pltpu.roll(xc, 1, 0))
    xp1 = jnp.where(rid == RB_S - 1, nrow, pltpu.roll(xc, RB_S - 1, 0))
    u = xm1 * cw_ref[0:1, :] + xc * cw_ref[1:2, :] + xp1 * cw_ref[2:3, :] + cb_ref[...]
    u = u * _sigmoid(u)

    dt_c = _softplus(dt_ref[:, 0:2 * SSD_HEADS] + dtb_c_ref[...])
    dt_r = _softplus(dtt_ref[...] + dtb_r_ref[...])
    da_c = dt_c * (-jnp.exp(alog_c_ref[...]))
    da_r = dt_r * (-jnp.exp(alog_r_ref[...]))

    lane = lax.broadcasted_iota(jnp.int32, (CHUNK, LANES), 1)
    lo = lane < SSD_HEAD_DIM
    lo_row = lax.broadcasted_iota(jnp.int32, (1, LANES), 1) < SSD_HEAD_DIM
    li = lax.broadcasted_iota(jnp.int32, (CHUNK, CHUNK), 0)
    si = lax.broadcasted_iota(jnp.int32, (CHUNK, CHUNK), 1)
    tri = (si >= li) if rev else (si <= li)

    n_chunks = RB_S // CHUNK
    order = range(n_chunks - 1, -1, -1) if rev else range(n_chunks)
    for c in order:
        r0 = c * CHUNK
        dac = da_c[r0:r0 + CHUNK]
        dar = da_r[:, r0:r0 + CHUNK]
        cs_c = sum(_dot(ltri_ref[...], part.astype(BF16)) for part in _split3(dac))
        cs_r = sum(_dot(part.astype(BF16), utri_ref[...]) for part in _split3(dar))
        tot = cs_c[CHUNK - 1:CHUNK, :]
        if rev:
            e_c = cs_c - dac
            e_r = cs_r - dar
            w_b = jnp.exp(e_c)
            w_c = jnp.exp(tot - e_c)
        else:
            e_c = cs_c
            e_r = cs_r
            w_b = jnp.exp(tot - cs_c)
            w_c = jnp.exp(cs_c)
        dec = jnp.exp(tot)
        dtc = dt_c[r0:r0 + CHUNK]
        uc = u[r0:r0 + CHUNK]
        bm = uc[:, D_INNER:D_INNER + LANES]
        cm = uc[:, D_INNER + LANES:D_INNER + 2 * LANES]
        for pr in range(SSD_GROUPS):
            gmask = lo if pr == 0 else jnp.logical_not(lo)
            cg = jnp.where(gmask, cm, 0.0).astype(BF16)
            bg = jnp.where(gmask, bm, 0.0).astype(BF16)
            gmat = _dot_nt(cg, bg)
            xp = uc[:, pr * LANES:(pr + 1) * LANES]
            l0 = d * SSD_HEADS + 2 * pr
            l1 = l0 + 1
            pair = lambda a: jnp.where(lo, a[:, l0:l0 + 1], a[:, l1:l1 + 1])
            xdt = xp * pair(dtc)
            y = jnp.zeros((CHUNK, LANES), F32)
            for hh, dl in ((0, l0), (1, l1)):
                ecol = e_c[:, dl:dl + 1]
                erow = e_r[dl:dl + 1, :]
                diff = (erow - ecol) if rev else (ecol - erow)
                dmat = jnp.where(tri, jnp.exp(jnp.minimum(diff, 0.0)), 0.0)
                hmask = lo if hh == 0 else jnp.logical_not(lo)
                y = y + _dot((gmat * dmat).astype(BF16), jnp.where(hmask, xdt, 0.0).astype(BF16))
            st = st_ref[pr]
            y = y + pair(w_c) * _dot(cg, st.astype(BF16))
            s_new = _dot_tn(bg, (xdt * pair(w_b)).astype(BF16))
            dec_pair = jnp.where(lo_row, dec[:, l0:l0 + 1], dec[:, l1:l1 + 1])
            st_ref[pr] = st * dec_pair + s_new
            if rev:
                cols = slice(pr * LANES, (pr + 1) * LANES)
                y = y + yf_ref[r0:r0 + CHUNK, cols] + xp * dskip_ref[:, cols]
            out_ref[r0:r0 + CHUNK, pr * LANES:(pr + 1) * LANES] = y

    if rev:
        z = cur_ref[:, CONV_DIM:PB_W].astype(F32)
        yz = out_ref[...] * (z * _sigmoid(z))
        ms = jnp.mean(yz * yz, axis=-1, keepdims=True)
        out_ref[...] = yz * lax.rsqrt(ms + EPS) * g_ref[...]


def _ssd(pb, dt, dtt, params, yf, batch, seq):
    rev = yf is not None
    t = batch * seq
    ns = seq // RB_S
    per = RB_S // HALO
    nh = seq // HALO
    pos = (lambda i: ns - 1 - i) if rev else (lambda i: i)
    ltri = jnp.asarray(np.tril(np.ones((CHUNK, CHUNK), np.float32)), dtype=BF16)
    in_specs = [
        pl.BlockSpec((RB_S, PB_W), lambda b, i: (b * ns + pos(i), 0)),
        pl.BlockSpec((HALO, CONV_DIM), lambda b, i: (b * nh + jnp.maximum(pos(i) * per - 1, 0), 0)),
        pl.BlockSpec((HALO, CONV_DIM), lambda b, i: (b * nh + jnp.minimum((pos(i) + 1) * per, nh - 1), 0)),
        pl.BlockSpec((RB_S, DT_W), lambda b, i: (b * ns + pos(i), 0)),
        pl.BlockSpec((2 * SSD_HEADS, RB_S), lambda b, i: (0, b * ns + pos(i))),
        _const_spec((CONV_K, CONV_DIM)), _const_spec((1, CONV_DIM)),
        _const_spec((1, 2 * SSD_HEADS)), _const_spec((2 * SSD_HEADS, 1)),
        _const_spec((1, 2 * SSD_HEADS)), _const_spec((2 * SSD_HEADS, 1)),
        _const_spec((CHUNK, CHUNK)), _const_spec((CHUNK, CHUNK)),
    ]
    args = [pb, pb, pb, dt, dtt, params["conv_w"], params["conv_b"], params["dtb_c"], params["dtb_r"],
            params["alog_c"], params["alog_r"], ltri, ltri.T]
    if rev:
        in_specs += [_const_spec((1, D_INNER)), _const_spec((1, D_INNER)),
                     pl.BlockSpec((RB_S, D_INNER), lambda b, i: (b * ns + pos(i), 0))]
        args += [params["dskip"], params["norm_g"], yf]
    return pl.pallas_call(
        functools.partial(_ssd_kernel, rev=rev, n_steps=ns),
        grid=(batch, ns),
        in_specs=in_specs,
        out_specs=pl.BlockSpec((RB_S, D_INNER), lambda b, i: (b * ns + pos(i), 0)),
        out_shape=jax.ShapeDtypeStruct((t, D_INNER), F32),
        scratch_shapes=[pltpu.VMEM((SSD_GROUPS, LANES, LANES), F32)],
        compiler_params=_cparams(("parallel", "arbitrary")),
        name="ssd_bwd" if rev else "ssd_fwd",
    )(*args)


TQ_C = 512
TK_C = 512
N_MAPS = 4
KV_SUB = 8


def _attn_c_kernel(slope_ref, lam_ref, qt_ref, k_ref, vt_ref, bias_ref, g_ref, o_ref, qz_s, m_s, acc_s):
    hp = pl.program_id(1)
    qi = pl.program_id(2)
    ki = pl.program_id(3)

    @pl.when(ki == 0)
    def _():
        m_s[...] = jnp.full_like(m_s, NEG_BIG)
        acc_s[...] = jnp.zeros_like(acc_s)
        qz_s[...] = jnp.zeros_like(qz_s)
        for idx in range(N_MAPS):
            c = idx % 2
            qz_s[idx, c * DIFF_QK_DIM:(c + 1) * DIFF_QK_DIM, :] = qt_ref[idx * DIFF_QK_DIM:(idx + 1) * DIFF_QK_DIM, :]

    units = [(s, hh, c) for s in range(KV_SUB) for hh in range(2) for c in range(2)]

    def scores(u):
        s, hh, c = units[u]
        kt = ki * KV_SUB + s
        var = jnp.where(kt < qi, 0, jnp.where(kt == qi, 1, 2))
        return (_dot(k_ref[s * TK_C:(s + 1) * TK_C, hh * LANES:(hh + 1) * LANES], qz_s[2 * hh + c])
                + bias_ref[var, hh])

    t_next = scores(0)
    for u, (s, hh, c) in enumerate(units):
        t = t_next
        if u + 1 < len(units):
            t_next = scores(u + 1)
        idx = 2 * hh + c
        kt = ki * KV_SUB + s
        c_tile = -jnp.abs(qi * TQ_C - kt * TK_C).astype(F32) * slope_ref[2 * hp + hh]
        m_old = m_s[idx]
        m_new = jnp.maximum(m_old, jnp.max(t, axis=0, keepdims=True) + c_tile)
        alpha = jnp.exp2(m_old - m_new)
        p = jnp.exp2(t - (m_new - c_tile))
        vt = vt_ref[hh * VT_ROWS:(hh + 1) * VT_ROWS, s * TK_C:(s + 1) * TK_C]
        acc_s[idx] = alpha * acc_s[idx] + _dot(vt, p.astype(BF16))
        m_s[idx] = m_new

    @pl.when(ki == pl.num_programs(3) - 1)
    def _():
        _attn_c_finalize(lam_ref, g_ref, o_ref, acc_s)


def _attn_c_finalize(lam_ref, g_ref, o_ref, acc_s):
    lam = lam_ref[0]
    for hh in range(2):
        a0 = acc_s[2 * hh]
        a1 = acc_s[2 * hh + 1]
        o = (a0[0:DIFF_V_DIM] * (1.0 / a0[DIFF_V_DIM:DIFF_V_DIM + 1])
             - lam * (a1[0:DIFF_V_DIM] * (1.0 / a1[DIFF_V_DIM:DIFF_V_DIM + 1])))
        ms = jnp.mean(o * o, axis=0, keepdims=True)
        rows = slice(hh * DIFF_V_DIM, (hh + 1) * DIFF_V_DIM)
        o_ref[rows, :] = (o * lax.rsqrt(ms + EPS) * g_ref[rows, :]).astype(BF16)


AUG_ROWS = 16
AUG_LANES = 64
BOUND_SLACK = 1.02
BOUND_LIMIT = 100.0


def _split3(x):
    hi = x.astype(BF16).astype(F32)
    r = x - hi
    mid = r.astype(BF16).astype(F32)
    return hi, mid, (r - mid).astype(BF16).astype(F32)


def _attn_c_bounded_kernel(slope_ref, lam_ref, kmax_ref, qt_ref, k_ref, vt_ref, dtab_ref, g_ref, o_ref,
                           rhs_s, mref_s, acc_s):
    hp = pl.program_id(1)
    qi = pl.program_id(2)
    ki = pl.program_id(3)
    units = [(s, hh, c) for s in range(KV_SUB) for hh in range(2) for c in range(2)]

    @pl.when((pl.program_id(0) == 0) & (hp == 0) & (qi == 0) & (ki == 0))
    def _():
        rhs_s[...] = jnp.zeros_like(rhs_s)

    @pl.when(ki == 0)
    def _():
        acc_s[...] = jnp.zeros_like(acc_s)
        for u, (s, hh, c) in enumerate(units):
            idx = 2 * hh + c
            qseg = qt_ref[idx * DIFF_QK_DIM:(idx + 1) * DIFF_QK_DIM, :]
            rhs_s[u, c * DIFF_QK_DIM:(c + 1) * DIFF_QK_DIM, :] = qseg
            if s == 0:
                qf = qseg.astype(F32)
                kmax = kmax_ref[0] * BOUND_SLACK
                mref_s[idx] = jnp.sqrt(jnp.sum(qf * qf, axis=0, keepdims=True)) * kmax

    qloc = lax.broadcasted_iota(jnp.int32, (1, TQ_C), 1).astype(F32)
    row = lax.broadcasted_iota(jnp.int32, (AUG_ROWS, TQ_C), 0)
    grp = row // 3
    part = row - 3 * grp

    def scores(u):
        s, hh, c = units[u]
        kt = ki * KV_SUB + s
        h = 2 * hp + hh
        sgn = jnp.where(kt < qi, 1.0, jnp.where(kt == qi, 0.0, -1.0)).astype(F32)
        slope = slope_ref[4 * h + 3]
        tile_dist = jnp.abs(qi * TQ_C - kt * TK_C).astype(F32)
        shift = mref_s[2 * hh + c] + slope * tile_dist + (sgn * slope) * qloc
        hi, mid, lo = _split3(shift)
        shift_part = jnp.where(part == 0, hi, jnp.where(part == 1, mid, lo))
        slope_part = sgn * jnp.where(part == 0, slope_ref[4 * h], jnp.where(part == 1, slope_ref[4 * h + 1],
                                                                           slope_ref[4 * h + 2]))
        aug = jnp.where(grp == 0, -shift_part, jnp.where(grp <= 2, slope_part, 0.0))
        rhs_s[u, 2 * DIFF_QK_DIM:2 * DIFF_QK_DIM + AUG_ROWS, :] = aug.astype(BF16)
        on_diag = (kt == qi).astype(jnp.int32)
        return _dot(k_ref[s * TK_C:(s + 1) * TK_C, hh * LANES:(hh + 1) * LANES], rhs_s[u]) + dtab_ref[on_diag, hh]

    t_next = scores(0)
    for u, (s, hh, c) in enumerate(units):
        t = t_next
        if u + 1 < len(units):
            t_next = scores(u + 1)
        idx = 2 * hh + c
        vt = vt_ref[hh * VT_ROWS:(hh + 1) * VT_ROWS, s * TK_C:(s + 1) * TK_C]
        acc_s[idx] = acc_s[idx] + _dot(vt, jnp.exp2(t).astype(BF16))

    @pl.when(ki == pl.num_programs(3) - 1)
    def _():
        _attn_c_finalize(lam_ref, g_ref, o_ref, acc_s)


def _attn_c_bias(slopes):
    rel = (np.arange(TQ_C)[None, :] - np.arange(TK_C)[:, None]).astype(np.float32)
    tables = jnp.asarray(np.stack([-rel, -np.abs(rel), rel]))
    s = slopes.reshape(DIFF_HEADS // 2, 1, 2, 1, 1)
    return s * tables[None, :, None]


def _attn_c_call(body, prefetch, tensors, table_variants, scratch, batch, seq, name):
    nq = seq // TQ_C
    kv_step = TK_C * KV_SUB
    nk = seq // kv_step
    n_pre = len(prefetch)
    grid_spec = pltpu.PrefetchScalarGridSpec(
        num_scalar_prefetch=n_pre,
        grid=(batch, DIFF_HEADS // 2, nq, nk),
        in_specs=[
            pl.BlockSpec((None, LANES, TQ_C), lambda b, h, i, j, *_: (b, h, i)),
            pl.BlockSpec((kv_step, 2 * LANES), lambda b, h, i, j, *_: (b * nk + j, h)),
            pl.BlockSpec((None, 2 * VT_ROWS, kv_step), lambda b, h, i, j, *_: (b, h, j)),
            pl.BlockSpec((None, table_variants, 2, TK_C, TQ_C), lambda b, h, i, j, *_: (h, 0, 0, 0, 0)),
            _const_spec((LANES, TQ_C)),
        ],
        out_specs=pl.BlockSpec((None, LANES, TQ_C), lambda b, h, i, j, *_: (b, h, i)),
        scratch_shapes=scratch + [pltpu.VMEM((N_MAPS, 1, TQ_C), F32), pltpu.VMEM((N_MAPS, VT_ROWS, TQ_C), F32)],
    )
    return pl.pallas_call(
        body,
        grid_spec=grid_spec,
        out_shape=jax.ShapeDtypeStruct((batch, WIDTH_C, seq), BF16),
        compiler_params=_cparams(("arbitrary", "arbitrary", "arbitrary", "arbitrary")),
        name=name,
    )(*prefetch, *tensors)


def _attn_c(qt, k_aug, vt, qk_gains, slopes, lam, gain, batch, seq):
    kv_step = TK_C * KV_SUB
    assert seq % kv_step == 0 and seq % TQ_C == 0
    gain_b = jnp.broadcast_to(gain.reshape(LANES, 1), (LANES, TQ_C))
    tables = _attn_c_bias(slopes)
    seg_bound = lambda g: math.sqrt(DIFF_QK_DIM) * jnp.max(jnp.abs(g))
    qmax = seg_bound(qk_gains[:, 0:256])
    kmax = seg_bound(qk_gains[:, 256:512])

    def online():
        scratch = [pltpu.VMEM((N_MAPS, LANES, TQ_C), BF16)]
        return _attn_c_call(_attn_c_kernel, (slopes, lam), (qt, k_aug, vt, tables, gain_b), 3, scratch,
                            batch, seq, "attn_c_online")

    def bounded():
        slope4 = jnp.stack(_split3(slopes) + (slopes,), axis=1).reshape(-1)
        dtab = jnp.stack([jnp.zeros_like(tables[:, 1]), tables[:, 1]], axis=1)
        scratch = [pltpu.VMEM((N_MAPS * KV_SUB, LANES, TQ_C), BF16)]
        return _attn_c_call(_attn_c_bounded_kernel, (slope4, lam, jnp.reshape(kmax, (1,))),
                            (qt, k_aug, vt, dtab, gain_b), 2, scratch, batch, seq, "attn_c_bounded")

    safe = 2.0 * BOUND_SLACK * BOUND_SLACK * qmax * kmax < BOUND_LIMIT
    return lax.cond(safe, bounded, online)


def _out_proj(oat_ref, ob_ref, oct_ref, x_ref, wo_ref):
    untr = lambda ref: ref[...].astype(F32).T.astype(BF16)
    o = jnp.concatenate([untr(oat_ref), ob_ref[...].astype(BF16), untr(oct_ref)], axis=-1)
    return x_ref[...] + _dot(o, wo_ref[...])


def _rms(x, g_ref):
    ms = jnp.mean(x * x, axis=-1, keepdims=True)
    return x * lax.rsqrt(ms + EPS) * g_ref[...]


def _swiglu_acc(h, wg_ref, wu_ref, wd_ref, acc, d_ff, chunks):
    step = d_ff // chunks
    for c0 in range(0, d_ff, step):
        gate = _dot(h, wg_ref[:, c0:c0 + step])
        up = _dot(h, wu_ref[:, c0:c0 + step])
        act = (gate * _sigmoid(gate) * up).astype(BF16)
        acc = acc + _dot(act, wd_ref[c0:c0 + step, :])
    return acc


def _out_ffn_kernel(oa_ref, ob_ref, oc_ref, x_ref, wo_ref, g_ref, wg_ref, wu_ref, wd_ref, out_ref):
    x1 = _out_proj(oa_ref, ob_ref, oc_ref, x_ref, wo_ref)
    h = _rms(x1, g_ref).astype(BF16)
    out_ref[...] = _swiglu_acc(h, wg_ref, wu_ref, wd_ref, x1, D_FF, FFN_CHUNKS)


def _mixer_out_specs(seq):
    nps = seq // TM
    row = lambda w: pl.BlockSpec((TM, w), lambda i: (i, 0))
    tr = lambda r: pl.BlockSpec((None, r, TM), lambda i: (i // nps, 0, i % nps))
    return [tr(WIDTH_A), row(D_INNER), tr(WIDTH_C), row(D_MODEL), _resident_spec((D_MODEL, D_MODEL)),
            _const_spec((1, D_MODEL))]


def _out_ffn(oa, ob, oc, x, w_out, g, wg, wu, wd):
    t = x.shape[0]
    return pl.pallas_call(
        _out_ffn_kernel,
        grid=(t // TM,),
        in_specs=_mixer_out_specs(oa.shape[2]) + [_resident_spec((D_MODEL, D_FF)), _resident_spec((D_MODEL, D_FF)),
                                       _resident_spec((D_FF, D_MODEL))],
        out_specs=pl.BlockSpec((TM, D_MODEL), lambda i: (i, 0)),
        out_shape=jax.ShapeDtypeStruct((t, D_MODEL), F32),
        compiler_params=_cparams(("parallel",)),
        name="out_ffn",
    )(oa, ob, oc, x, w_out, g, wg, wu, wd)


def _out_router_kernel(oa_ref, ob_ref, oc_ref, x_ref, wo_ref, g_ref, wr_hi_ref, wr_lo_ref,
                       x1_ref, h_ref, sel_ref, gate_ref):
    x1 = _out_proj(oa_ref, ob_ref, oc_ref, x_ref, wo_ref)
    x1_ref[...] = x1
    h = _rms(x1, g_ref)
    h_hi = h.astype(BF16)
    h_ref[...] = h_hi
    h_lo = (h - h_hi.astype(F32)).astype(BF16)
    logits = _dot(h_hi, wr_hi_ref[...]) + (_dot(h_lo, wr_hi_ref[...]) + _dot(h_hi, wr_lo_ref[...]))
    lt = logits.T[0:N_EXPERTS, :]
    row_i = lax.broadcasted_iota(jnp.int32, (N_EXPERTS, TM), 0)
    row = row_i.astype(F32)
    v1 = jnp.max(lt, axis=0, keepdims=True)
    i1 = jnp.min(jnp.where(lt == v1, row, float(N_EXPERTS)), axis=0, keepdims=True)
    rest = jnp.where(row == i1, -jnp.inf, lt)
    v2 = jnp.max(rest, axis=0, keepdims=True)
    i2 = jnp.min(jnp.where(rest == v2, row, float(N_EXPERTS)), axis=0, keepdims=True)
    e = jnp.exp(v2 - v1)
    g1 = 1.0 / (1.0 + e)
    sel_ref[...] = jnp.where(row_i == 0, i1, jnp.where(row_i == 1, i2, 0.0)).astype(jnp.int32)
    gate_ref[...] = jnp.where(row_i == 0, g1, jnp.where(row_i == 1, e * g1, 0.0))


def _out_router(oa, ob, oc, x, w_out, g, wr_hi, wr_lo):
    t = x.shape[0]
    row = lambda w: pl.BlockSpec((TM, w), lambda i: (i, 0))
    col = pl.BlockSpec((N_EXPERTS, TM), lambda i: (0, i))
    return pl.pallas_call(
        _out_router_kernel,
        grid=(t // TM,),
        in_specs=_mixer_out_specs(oa.shape[2]) + [_const_spec((D_MODEL, LANES)), _const_spec((D_MODEL, LANES))],
        out_specs=[row(D_MODEL), row(D_MODEL), col, col],
        out_shape=[jax.ShapeDtypeStruct((t, D_MODEL), F32), jax.ShapeDtypeStruct((t, D_MODEL), BF16),
                   jax.ShapeDtypeStruct((N_EXPERTS, t), jnp.int32), jax.ShapeDtypeStruct((N_EXPERTS, t), F32)],
        compiler_params=_cparams(("parallel",)),
        name="out_router",
    )(oa, ob, oc, x, w_out, g, wr_hi, wr_lo)


def _moe_ffn_kernel(blk_exp_ref, n_used_ref, x_ref, wg_ref, wu_ref, wd_ref, out_ref):
    i = pl.program_id(0)

    @pl.when(i < n_used_ref[0])
    def _():
        acc = jnp.zeros((MOE_BLK, D_MODEL), F32)
        out_ref[...] = _swiglu_acc(x_ref[...], wg_ref.at[0], wu_ref.at[0], wd_ref.at[0], acc,
                                   D_FF_EXPERT, MOE_CHUNKS).astype(BF16)

    @pl.when(i >= n_used_ref[0])
    def _():
        out_ref[...] = jnp.zeros_like(out_ref)


def _moe_ffn(xg, blk_exp, n_used, wg, wu, wd):
    rows = xg.shape[0]
    wspec = lambda shape: pl.BlockSpec((1,) + shape, lambda i, be, nu: (be[i], 0, 0),
                                       pipeline_mode=pl.Buffered(1))
    grid_spec = pltpu.PrefetchScalarGridSpec(
        num_scalar_prefetch=2,
        grid=(rows // MOE_BLK,),
        in_specs=[pl.BlockSpec((MOE_BLK, D_MODEL), lambda i, be, nu: (i, 0)),
                  wspec((D_MODEL, D_FF_EXPERT)), wspec((D_MODEL, D_FF_EXPERT)),
                  wspec((D_FF_EXPERT, D_MODEL))],
        out_specs=pl.BlockSpec((MOE_BLK, D_MODEL), lambda i, be, nu: (i, 0)),
    )
    return pl.pallas_call(
        _moe_ffn_kernel,
        grid_spec=grid_spec,
        out_shape=jax.ShapeDtypeStruct((rows, D_MODEL), BF16),
        compiler_params=_cparams(("arbitrary",)),
        name="moe_ffn",
    )(blk_exp, n_used, xg, wg, wu, wd)


def _moe(x1, h, sel, gates, wg, wu, wd):
    t = x1.shape[0]
    i1, i2 = sel[0], sel[1]
    g1, g2 = gates[0][:, None], gates[1][:, None]
    experts = jnp.arange(N_EXPERTS, dtype=jnp.int32)
    onehot = ((i1[:, None] == experts) | (i2[:, None] == experts)).astype(jnp.int32)
    rank = jnp.cumsum(onehot, axis=0) - onehot
    counts = jnp.sum(onehot, axis=0)
    padded = ((counts + MOE_BLK - 1) // MOE_BLK) * MOE_BLK
    pad_end = jnp.cumsum(padded)
    pad_start = pad_end - padded
    dest = pad_start[None, :] + rank
    d1 = jnp.take_along_axis(dest, i1[:, None], axis=1)[:, 0]
    d2 = jnp.take_along_axis(dest, i2[:, None], axis=1)[:, 0]
    n_blocks = (2 * t) // MOE_BLK + N_EXPERTS
    rows = n_blocks * MOE_BLK
    blk_start = jnp.arange(n_blocks, dtype=jnp.int32) * MOE_BLK
    blk_exp = jnp.minimum(jnp.sum((pad_end[None, :] <= blk_start[:, None]).astype(jnp.int32), axis=1),
                          N_EXPERTS - 1)
    order = jnp.argsort(jnp.concatenate([d1, d2])).astype(jnp.int32)
    row_exp = jnp.repeat(blk_exp, MOE_BLK)
    in_exp = jnp.arange(rows, dtype=jnp.int32) - pad_start[row_exp]
    starts = jnp.cumsum(counts) - counts
    packed = jnp.minimum(starts[row_exp] + in_exp, 2 * t - 1)
    row_tok = jnp.where(in_exp < counts[row_exp], order[packed] % t, 0)
    n_used = (pad_end[-1:] // MOE_BLK).astype(jnp.int32)
    yb = _moe_ffn(h[row_tok], blk_exp, n_used, wg, wu, wd)
    return x1 + g1 * yb[d1].astype(F32) + g2 * yb[d2].astype(F32)


def _split_cols(w):
    outs, start = [], 0
    for n in SPLIT_SIZES:
        outs.append(w[:, start:start + n])
        start += n
    return outs


def _layer_params(l, p):
    qa, ka, va, zb, xbc, dtb, qc, kc, vc = _split_cols(p["w_in"][l])
    w_cat = jnp.concatenate(
        [qa, ka, va, xbc, zb, qc, kc, vc, dtb, jnp.zeros((D_MODEL, DT_W - 2 * SSD_HEADS), F32)],
        axis=1).astype(BF16)
    ga = jnp.concatenate([jnp.tile(p["qnorm_a"][l] * (HEAD_DIM_A ** -0.5 * LOG2E), N_HEADS_A),
                          jnp.tile(p["knorm_a"][l], KV_HEADS_A)])[None, :]
    gc = jnp.concatenate([jnp.tile(p["qnorm_c"][l] * (DIFF_QK_DIM ** -0.5 * LOG2E), 2 * DIFF_HEADS),
                          jnp.tile(p["knorm_c"][l], 2 * DIFF_HEADS)])[None, :]
    lam_init = 0.8 - 0.6 * math.exp(-0.3 * l)
    lc = p["lam_c"][l].astype(F32)
    lam = jnp.exp(jnp.sum(lc[0] * lc[1])) - jnp.exp(jnp.sum(lc[2] * lc[3])) + lam_init
    out = {
        "norm_mix_g": p["norm_mix_g"][l][None, :], "w_cat": w_cat, "ga": ga, "gc": gc,
        "sink": p["sink_a"][l],
        "ssd": {
            "conv_w": p["conv_w"][l], "conv_b": p["conv_b"][l][None, :],
            "dtb_c": p["dt_bias"][l].reshape(1, -1), "dtb_r": p["dt_bias"][l].reshape(-1, 1),
            "alog_c": p["a_log"][l].reshape(1, -1), "alog_r": p["a_log"][l].reshape(-1, 1),
            "dskip": jnp.repeat(p["d_skip"][l], SSD_HEAD_DIM)[None, :],
            "norm_g": p["ssd_norm_g"][l][None, :],
        },
        "slopes_c": jnp.exp2(-8.0 * jnp.arange(1, DIFF_HEADS + 1, dtype=F32) / DIFF_HEADS) * LOG2E,
        "lam": jnp.reshape(lam, (1,)).astype(F32),
        "gain_c": (jnp.tile(p["diff_norm_g"][l], 2) * (1.0 - lam_init))[None, :],
        "w_out": p["w_out"][l].astype(BF16),
        "norm_ffn_g": p["norm_ffn_g"][l][None, :],
    }
    i = l // 2
    if l % 2 == 0:
        out["ffn"] = (p["ffn_w_gate"][i].astype(BF16), p["ffn_w_up"][i].astype(BF16),
                      p["ffn_w_down"][i].astype(BF16))
    else:
        wr = jnp.pad(p["router_w"][i], ((0, 0), (0, LANES - N_EXPERTS)))
        wr_hi = wr.astype(BF16)
        out["router"] = (wr_hi, (wr - wr_hi.astype(F32)).astype(BF16))
        out["moe"] = (p["moe_w_gate"][i].astype(BF16), p["moe_w_up"][i].astype(BF16),
                      p["moe_w_down"][i].astype(BF16))
    return out


def _trunk(x3, layers):
    batch, seq, _ = x3.shape
    x = x3.reshape(batch * seq, D_MODEL)
    for l, lp in enumerate(layers):
        qat, ka, vat, pb, qct, kc, vct, dt, dtt = _norm_proj(x, lp["norm_mix_g"], lp["w_cat"], lp["ga"], lp["gc"],
                                                             batch, seq)
        oa = _attn_a(qat, ka, vat, lp["sink"], batch, seq)
        yf = _ssd(pb, dt, dtt, lp["ssd"], None, batch, seq)
        ob = _ssd(pb, dt, dtt, lp["ssd"], yf, batch, seq)
        oc = _attn_c(qct, kc, vct, lp["gc"], lp["slopes_c"], lp["lam"], lp["gain_c"], batch, seq)
        if l % 2 == 0:
            x = _out_ffn(oa, ob, oc, x, lp["w_out"], lp["norm_ffn_g"], *lp["ffn"])
        else:
            x1, h, sel, gates = _out_router(oa, ob, oc, x, lp["w_out"], lp["norm_ffn_g"], *lp["router"])
            x = _moe(x1, h, sel, gates, *lp["moe"])
    return x.reshape(batch, seq, D_MODEL)


def kernel(x_prompt, x_sample, norm_mix_g, w_in, qnorm_a, knorm_a, sink_a, conv_w, conv_b, dt_bias, a_log,
           d_skip, ssd_norm_g, qnorm_c, knorm_c, lam_c, diff_norm_g, w_out, norm_ffn_g, ffn_w_gate, ffn_w_up,
           ffn_w_down, router_w, moe_w_gate, moe_w_up, moe_w_down):
    p = dict(norm_mix_g=norm_mix_g, w_in=w_in, qnorm_a=qnorm_a, knorm_a=knorm_a, sink_a=sink_a,
             conv_w=conv_w, conv_b=conv_b, dt_bias=dt_bias, a_log=a_log, d_skip=d_skip,
             ssd_norm_g=ssd_norm_g, qnorm_c=qnorm_c, knorm_c=knorm_c, lam_c=lam_c, diff_norm_g=diff_norm_g,
             w_out=w_out, norm_ffn_g=norm_ffn_g, ffn_w_gate=ffn_w_gate, ffn_w_up=ffn_w_up,
             ffn_w_down=ffn_w_down, router_w=router_w, moe_w_gate=moe_w_gate, moe_w_up=moe_w_up,
             moe_w_down=moe_w_down)
    layers = [_layer_params(l, p) for l in range(DEPTH)]
    return _trunk(x_prompt, layers), _trunk(x_sample, layers)
```

```python
import functools
import math

import numpy as np
import jax
import jax.numpy as jnp
from jax import lax
from jax.experimental import pallas as pl
from jax.experimental.pallas import tpu as pltpu

F32 = jnp.float32
BF16 = jnp.bfloat16

D_MODEL = 1024
DEPTH = 2
N_HEADS_A = 8
KV_HEADS_A = 2
HEAD_DIM_A = 64
WINDOW = 128
ATT_BLOCK = 128
SSD_HEADS = 4
SSD_HEAD_DIM = 64
D_INNER = SSD_HEADS * SSD_HEAD_DIM
SSD_GROUPS = 2
D_STATE = 64
CONV_K = 3
CHUNK = 128
DIFF_HEADS = 4
DIFF_QK_DIM = 32
DIFF_V_DIM = 64
WIDTH_A = N_HEADS_A * HEAD_DIM_A
WIDTH_C = DIFF_HEADS * DIFF_V_DIM
CONV_DIM = D_INNER + 2 * SSD_GROUPS * D_STATE
SPLIT_SIZES = (WIDTH_A, KV_HEADS_A * HEAD_DIM_A, KV_HEADS_A * HEAD_DIM_A,
               D_INNER, CONV_DIM, 2 * SSD_HEADS,
               DIFF_HEADS * 2 * DIFF_QK_DIM, DIFF_HEADS * 2 * DIFF_QK_DIM, WIDTH_C)
D_FF = 2816
N_EXPERTS = 8
D_FF_EXPERT = 3584
EPS = 1e-6

LOG2E = 1.4426950408889634
NEG_BIG = -1e30

LANES = 128
VMEM_LIMIT = 56 * 1024 * 1024

PA_W = 768
PA_NORM_W = 640
VT_ROWS = 80
PB_W = 768
PC_W = 768
DT_W = 128
PROJ_W = PA_W + PB_W + PC_W + DT_W

TM = 512
MOE_BLK = 512
FFN_CHUNKS = 1
MOE_CHUNKS = 2


def _cparams(sem, vmem=VMEM_LIMIT):
    return pltpu.CompilerParams(dimension_semantics=sem, vmem_limit_bytes=vmem)


def _const_spec(shape):
    nd = len(shape)
    return pl.BlockSpec(shape, lambda *_: (0,) * nd)


def _resident_spec(shape):
    nd = len(shape)
    return pl.BlockSpec(shape, lambda *_: (0,) * nd, pipeline_mode=pl.Buffered(1))


def _block_diag(n, seg):
    idx = np.arange(n) // seg
    return jnp.asarray((idx[:, None] == idx[None, :]).astype(np.float32), dtype=BF16)


def _sigmoid(x):
    return 1.0 / (1.0 + jnp.exp(-x))


def _softplus(x):
    return jnp.maximum(x, 0.0) + jnp.log(1.0 + jnp.exp(-jnp.abs(x)))


def _dot(a, b):
    return jnp.dot(a, b, preferred_element_type=F32)


def _dot_nt(a, b):
    return lax.dot_general(a, b, (((1,), (1,)), ((), ())), preferred_element_type=F32)


def _dot_tn(a, b):
    return lax.dot_general(a, b, (((0,), (0,)), ((), ())), preferred_element_type=F32)


def _dot_hi(a, b):
    return jnp.dot(a, b, preferred_element_type=F32, precision=lax.Precision.HIGHEST)


def _seg_norm(blk, seg, bd_ref, gain):
    w = blk.shape[1]
    ss = _dot((blk * blk).astype(BF16), bd_ref[0:w, 0:w]) * (1.0 / seg)
    return blk * lax.rsqrt(ss + EPS) * gain


def _store_vt(vt_ref, v, heads, head_dim, pad_ref):
    vt = v.T
    for h in range(heads):
        r0 = h * VT_ROWS
        vt_ref[r0:r0 + head_dim, :] = vt[h * head_dim:(h + 1) * head_dim].astype(BF16)
        vt_ref[r0 + head_dim:r0 + VT_ROWS, :] = pad_ref[...]


def _norm_proj_kernel(x_ref, g_ref, w_ref, bd64_ref, bd32_ref, ga_ref, gc_ref, pad_ref, kaug_ref,
                      qat_ref, ka_ref, vat_ref, pb_ref, qct_ref, kc_ref, vct_ref, dt_ref, dtt_ref):
    x = x_ref[...]
    ms = jnp.mean(x * x, axis=-1, keepdims=True)
    h = (x * lax.rsqrt(ms + EPS) * g_ref[...]).astype(BF16)
    ya = _dot(h, w_ref[:, 0:PA_W])
    for c0 in range(0, WIDTH_A, 256):
        qn = _seg_norm(ya[:, c0:c0 + 256], HEAD_DIM_A, bd64_ref, ga_ref[:, c0:c0 + 256])
        qat_ref[c0:c0 + 256, :] = qn.T.astype(BF16)
    ka_ref[...] = _seg_norm(ya[:, WIDTH_A:PA_NORM_W], HEAD_DIM_A, bd64_ref, ga_ref[:, WIDTH_A:PA_NORM_W]).astype(BF16)
    _store_vt(vat_ref, ya[:, PA_NORM_W:PA_W], KV_HEADS_A, HEAD_DIM_A, pad_ref)
    pb_ref[...] = _dot(h, w_ref[:, PA_W:PA_W + PB_W]).astype(BF16)
    yc = _dot(h, w_ref[:, PA_W + PB_W:PA_W + PB_W + PC_W])
    qct_ref[...] = _seg_norm(yc[:, 0:256], DIFF_QK_DIM, bd32_ref, gc_ref[:, 0:256]).T.astype(BF16)
    kc = _seg_norm(yc[:, 256:512], DIFF_QK_DIM, bd32_ref, gc_ref[:, 256:512]).astype(BF16)
    aug = kaug_ref[...]
    pieces = []
    for hd in range(DIFF_HEADS):
        pieces += [kc[:, hd * 2 * DIFF_QK_DIM:(hd + 1) * 2 * DIFF_QK_DIM], aug]
    kc_ref[...] = jnp.concatenate(pieces, axis=1)
    _store_vt(vct_ref, yc[:, 512:PC_W], DIFF_HEADS, DIFF_V_DIM, pad_ref)
    yd = _dot(h, w_ref[:, PA_W + PB_W + PC_W:PROJ_W])
    dt_ref[...] = yd
    dtt_ref[...] = yd.T[0:2 * SSD_HEADS, :]


def _norm_proj(x, g, w_cat, ga, gc, batch, seq):
    t = batch * seq
    nps = seq // TM
    row = lambda w: pl.BlockSpec((TM, w), lambda i: (i, 0))
    tr = lambda r: pl.BlockSpec((None, r, TM), lambda i: (i // nps, 0, i % nps))
    pad = jnp.zeros((VT_ROWS - HEAD_DIM_A, TM), BF16).at[0].set(1.0)
    pos = np.arange(TM) % TK_C
    cols = np.zeros((TM, AUG_LANES), np.float32)
    cols[:, 0:3] = 1.0
    cols[:, 3:6] = (pos // 256 * 256)[:, None]
    cols[:, 6:9] = (pos % 256)[:, None]
    return pl.pallas_call(
        _norm_proj_kernel,
        grid=(t // TM,),
        in_specs=[row(D_MODEL), _const_spec((1, D_MODEL)), _resident_spec((D_MODEL, PROJ_W)),
                  _const_spec((256, 256)), _const_spec((256, 256)),
                  _const_spec((1, PA_NORM_W)), _const_spec((1, 512)),
                  _const_spec((VT_ROWS - HEAD_DIM_A, TM)), _const_spec((TM, AUG_LANES))],
        out_specs=[tr(WIDTH_A), row(LANES), tr(KV_HEADS_A * VT_ROWS), row(PB_W),
                   tr(256), row(DIFF_HEADS * LANES), tr(DIFF_HEADS * VT_ROWS), row(DT_W),
                   pl.BlockSpec((2 * SSD_HEADS, TM), lambda i: (0, i))],
        out_shape=[jax.ShapeDtypeStruct((batch, WIDTH_A, seq), BF16), jax.ShapeDtypeStruct((t, LANES), BF16),
                   jax.ShapeDtypeStruct((batch, KV_HEADS_A * VT_ROWS, seq), BF16),
                   jax.ShapeDtypeStruct((t, PB_W), BF16),
                   jax.ShapeDtypeStruct((batch, 256, seq), BF16),
                   jax.ShapeDtypeStruct((t, DIFF_HEADS * LANES), BF16),
                   jax.ShapeDtypeStruct((batch, DIFF_HEADS * VT_ROWS, seq), BF16),
                   jax.ShapeDtypeStruct((t, DT_W), F32), jax.ShapeDtypeStruct((2 * SSD_HEADS, t), F32)],
        compiler_params=_cparams(("parallel",)),
        name="norm_proj",
    )(x, g, w_cat, _block_diag(256, HEAD_DIM_A), _block_diag(256, DIFF_QK_DIM), ga, gc, pad,
      jnp.asarray(cols, dtype=BF16))


QB_A = 2048


GQA = N_HEADS_A // KV_HEADS_A
WIN_A = 3 * ATT_BLOCK
NQ_A = GQA * ATT_BLOCK


def _attn_a_kernel(qt_ref, k_ref, kp_ref, kn_ref, vt_ref, vtp_ref, vtn_ref, bias_ref, sink_ref, o_ref,
                   k_s, vt_s, rhs_s, *, n_blocks):
    i = pl.program_id(1)
    nsub = QB_A // ATT_BLOCK
    b = ATT_BLOCK
    k_s[0:b] = kp_ref[...]
    k_s[b:b + QB_A] = k_ref[...]
    k_s[b + QB_A:] = kn_ref[...]
    vt_s[:, 0:b] = vtp_ref[...]
    vt_s[:, b:b + QB_A] = vt_ref[...]
    vt_s[:, b + QB_A:] = vtn_ref[...]
    units = [(t, j) for t in range(nsub) for j in range(KV_HEADS_A)]

    def scores(u):
        t, j = units[u]
        other = 1 - j
        rhs_s[u, other * HEAD_DIM_A:(other + 1) * HEAD_DIM_A, :] = jnp.zeros((HEAD_DIM_A, NQ_A), BF16)
        for g in range(GQA):
            h = GQA * j + g
            rhs_s[u, j * HEAD_DIM_A:(j + 1) * HEAD_DIM_A, g * b:(g + 1) * b] = (
                qt_ref[h * HEAD_DIM_A:(h + 1) * HEAD_DIM_A, t * b:(t + 1) * b])
        gb = i * nsub + t
        var = jnp.where(gb == 0, 0, jnp.where(gb == n_blocks - 1, 2, 1))
        return _dot(k_s[t * b:t * b + WIN_A, :], rhs_s[u]) + bias_ref[var, j]

    s_next = scores(0)
    for u, (t, j) in enumerate(units):
        s = s_next
        if u + 1 < len(units):
            s_next = scores(u + 1)
        sk = sink_ref[j]
        m = jnp.maximum(jnp.max(s, axis=0, keepdims=True), sk)
        p = jnp.exp2(s - m)
        r = _dot(vt_s[j * VT_ROWS:(j + 1) * VT_ROWS, t * b:t * b + WIN_A], p.astype(BF16))
        den = r[HEAD_DIM_A:HEAD_DIM_A + 1] + jnp.exp2(sk - m)
        o = (r[0:HEAD_DIM_A] * (1.0 / den)).astype(BF16)
        for g in range(GQA):
            h = GQA * j + g
            o_ref[h * HEAD_DIM_A:(h + 1) * HEAD_DIM_A, t * b:(t + 1) * b] = o[:, g * b:(g + 1) * b]


def _attn_a_consts(sink):
    b = ATT_BLOCK
    qi = np.arange(b)[None, :]
    s = np.arange(WIN_A)[:, None]
    dist = np.abs(b + qi - s).astype(np.float32)
    in_win = dist <= WINDOW
    slopes = jnp.exp2(-8.0 * jnp.arange(1, N_HEADS_A + 1, dtype=F32) / N_HEADS_A) * LOG2E
    alibi = -slopes.reshape(KV_HEADS_A, 1, GQA, 1) * jnp.asarray(dist)[None, :, None, :]
    variants = []
    for valid in (s >= b, s >= 0, s < 2 * b):
        mask = jnp.asarray(in_win & valid)[None, :, None, :]
        variants.append(jnp.where(mask, alibi, NEG_BIG).reshape(KV_HEADS_A, WIN_A, NQ_A))
    bias = jnp.stack(variants)
    sink_cols = jnp.repeat(sink.astype(F32).reshape(KV_HEADS_A, GQA) * LOG2E, b, axis=1)[:, None, :]
    return bias, sink_cols


def _attn_a(qt, ka, vt, sink, batch, seq):
    nq = seq // QB_A
    nblk = seq // ATT_BLOCK
    per = QB_A // ATT_BLOCK
    bias, sink_cols = _attn_a_consts(sink)
    kcol = 0
    prev = lambda i: jnp.maximum(i * per - 1, 0)
    nxt = lambda i: jnp.minimum((i + 1) * per, nblk - 1)
    vrows = KV_HEADS_A * VT_ROWS
    return pl.pallas_call(
        functools.partial(_attn_a_kernel, n_blocks=nblk),
        grid=(batch, nq),
        in_specs=[
            pl.BlockSpec((None, WIDTH_A, QB_A), lambda b, i: (b, 0, i)),
            pl.BlockSpec((QB_A, LANES), lambda b, i: (b * nq + i, kcol)),
            pl.BlockSpec((ATT_BLOCK, LANES), lambda b, i: (b * nblk + prev(i), kcol)),
            pl.BlockSpec((ATT_BLOCK, LANES), lambda b, i: (b * nblk + nxt(i), kcol)),
            pl.BlockSpec((None, vrows, QB_A), lambda b, i: (b, 0, i)),
            pl.BlockSpec((None, vrows, ATT_BLOCK), lambda b, i: (b, 0, prev(i))),
            pl.BlockSpec((None, vrows, ATT_BLOCK), lambda b, i: (b, 0, nxt(i))),
            _const_spec((3, KV_HEADS_A, WIN_A, NQ_A)),
            _const_spec((KV_HEADS_A, 1, NQ_A)),
        ],
        out_specs=pl.BlockSpec((None, WIDTH_A, QB_A), lambda b, i: (b, 0, i)),
        out_shape=jax.ShapeDtypeStruct((batch, WIDTH_A, seq), BF16),
        scratch_shapes=[pltpu.VMEM((QB_A + 2 * ATT_BLOCK, LANES), BF16),
                        pltpu.VMEM((vrows, QB_A + 2 * ATT_BLOCK), BF16),
                        pltpu.VMEM((per * KV_HEADS_A, LANES, NQ_A), BF16)],
        compiler_params=_cparams(("parallel", "parallel")),
        name="attn_a",
    )(qt, ka, ka, ka, vt, vt, vt, bias, sink_cols)


RB_S = 512
HALO = 16


def _ssd_kernel(*refs, rev, n_steps):
    if rev:
        (cur_ref, prev_ref, next_ref, dt_ref, dtt_ref, cw_ref, cb_ref, dtb_c_ref, dtb_r_ref,
         alog_c_ref, alog_r_ref, ltri_ref, utri_ref, dskip_ref, g_ref, yf_ref, out_ref, st_ref) = refs
    else:
        (cur_ref, prev_ref, next_ref, dt_ref, dtt_ref, cw_ref, cb_ref, dtb_c_ref, dtb_r_ref,
         alog_c_ref, alog_r_ref, ltri_ref, utri_ref, out_ref, st_ref) = refs
    i = pl.program_id(1)
    ii = (n_steps - 1 - i) if rev else i
    d = 1 if rev else 0

    @pl.when(i == 0)
    def _():
        st_ref[...] = jnp.zeros_like(st_ref)

    xc = cur_ref[:, 0:CONV_DIM].astype(F32)
    prow = jnp.where(ii == 0, 0.0, prev_ref[...].astype(F32)[HALO - 1:HALO, :])
    nrow = jnp.where(ii == n_steps - 1, 0.0, next_ref[...].astype(F32)[0:1, :])
    rid = lax.broadcasted_iota(jnp.int32, (RB_S, CONV_DIM), 0)
    xm1 = jnp.where(rid == 0, prow, pltpu.roll(xc, 1, 0))
    xp1 = jnp.where(rid == RB_S - 1, nrow, pltpu.roll(xc, RB_S - 1, 0))
    u = xm1 * cw_ref[0:1, :] + xc * cw_ref[1:2, :] + xp1 * cw_ref[2:3, :] + cb_ref[...]
    u = u * _sigmoid(u)

    dt_c = _softplus(dt_ref[:, 0:2 * SSD_HEADS] + dtb_c_ref[...])
    dt_r = _softplus(dtt_ref[...] + dtb_r_ref[...])
    da_c = dt_c * (-jnp.exp(alog_c_ref[...]))
    da_r = dt_r * (-jnp.exp(alog_r_ref[...]))

    lane = lax.broadcasted_iota(jnp.int32, (CHUNK, LANES), 1)
    lo = lane < SSD_HEAD_DIM
    lo_row = lax.broadcasted_iota(jnp.int32, (1, LANES), 1) < SSD_HEAD_DIM
    li = lax.broadcasted_iota(jnp.int32, (CHUNK, CHUNK), 0)
    si = lax.broadcasted_iota(jnp.int32, (CHUNK, CHUNK), 1)
    tri = (si >= li) if rev else (si <= li)

    n_chunks = RB_S // CHUNK
    order = range(n_chunks - 1, -1, -1) if rev else range(n_chunks)
    for c in order:
        r0 = c * CHUNK
        dac = da_c[r0:r0 + CHUNK]
        dar = da_r[:, r0:r0 + CHUNK]
        cs_c = sum(_dot(ltri_ref[...], part.astype(BF16)) for part in _split3(dac))
        cs_r = sum(_dot(part.astype(BF16), utri_ref[...]) for part in _split3(dar))
        tot = cs_c[CHUNK - 1:CHUNK, :]
        if rev:
            e_c = cs_c - dac
            e_r = cs_r - dar
            w_b = jnp.exp(e_c)
            w_c = jnp.exp(tot - e_c)
        else:
            e_c = cs_c
            e_r = cs_r
            w_b = jnp.exp(tot - cs_c)
            w_c = jnp.exp(cs_c)
        dec = jnp.exp(tot)
        dtc = dt_c[r0:r0 + CHUNK]
        uc = u[r0:r0 + CHUNK]
        bm = uc[:, D_INNER:D_INNER + LANES]
        cm = uc[:, D_INNER + LANES:D_INNER + 2 * LANES]
        for pr in range(SSD_GROUPS):
            gmask = lo if pr == 0 else jnp.logical_not(lo)
            cg = jnp.where(gmask, cm, 0.0).astype(BF16)
            bg = jnp.where(gmask, bm, 0.0).astype(BF16)
            gmat = _dot_nt(cg, bg)
            xp = uc[:, pr * LANES:(pr + 1) * LANES]
            l0 = d * SSD_HEADS + 2 * pr
            l1 = l0 + 1
            pair = lambda a: jnp.where(lo, a[:, l0:l0 + 1], a[:, l1:l1 + 1])
            xdt = xp * pair(dtc)
            y = jnp.zeros((CHUNK, LANES), F32)
            for hh, dl in ((0, l0), (1, l1)):
                ecol = e_c[:, dl:dl + 1]
                erow = e_r[dl:dl + 1, :]
                diff = (erow - ecol) if rev else (ecol - erow)
                dmat = jnp.where(tri, jnp.exp(jnp.minimum(diff, 0.0)), 0.0)
                hmask = lo if hh == 0 else jnp.logical_not(lo)
                y = y + _dot((gmat * dmat).astype(BF16), jnp.where(hmask, xdt, 0.0).astype(BF16))
            st = st_ref[pr]
            y = y + pair(w_c) * _dot(cg, st.astype(BF16))
            s_new = _dot_tn(bg, (xdt * pair(w_b)).astype(BF16))
            dec_pair = jnp.where(lo_row, dec[:, l0:l0 + 1], dec[:, l1:l1 + 1])
            st_ref[pr] = st * dec_pair + s_new
            if rev:
                cols = slice(pr * LANES, (pr + 1) * LANES)
                y = y + yf_ref[r0:r0 + CHUNK, cols] + xp * dskip_ref[:, cols]
            out_ref[r0:r0 + CHUNK, pr * LANES:(pr + 1) * LANES] = y

    if rev:
        z = cur_ref[:, CONV_DIM:PB_W].astype(F32)
        yz = out_ref[...] * (z * _sigmoid(z))
        ms = jnp.mean(yz * yz, axis=-1, keepdims=True)
        out_ref[...] = yz * lax.rsqrt(ms + EPS) * g_ref[...]


def _ssd(pb, dt, dtt, params, yf, batch, seq):
    rev = yf is not None
    t = batch * seq
    ns = seq // RB_S
    per = RB_S // HALO
    nh = seq // HALO
    pos = (lambda i: ns - 1 - i) if rev else (lambda i: i)
    ltri = jnp.asarray(np.tril(np.ones((CHUNK, CHUNK), np.float32)), dtype=BF16)
    in_specs = [
        pl.BlockSpec((RB_S, PB_W), lambda b, i: (b * ns + pos(i), 0)),
        pl.BlockSpec((HALO, CONV_DIM), lambda b, i: (b * nh + jnp.maximum(pos(i) * per - 1, 0), 0)),
        pl.BlockSpec((HALO, CONV_DIM), lambda b, i: (b * nh + jnp.minimum((pos(i) + 1) * per, nh - 1), 0)),
        pl.BlockSpec((RB_S, DT_W), lambda b, i: (b * ns + pos(i), 0)),
        pl.BlockSpec((2 * SSD_HEADS, RB_S), lambda b, i: (0, b * ns + pos(i))),
        _const_spec((CONV_K, CONV_DIM)), _const_spec((1, CONV_DIM)),
        _const_spec((1, 2 * SSD_HEADS)), _const_spec((2 * SSD_HEADS, 1)),
        _const_spec((1, 2 * SSD_HEADS)), _const_spec((2 * SSD_HEADS, 1)),
        _const_spec((CHUNK, CHUNK)), _const_spec((CHUNK, CHUNK)),
    ]
    args = [pb, pb, pb, dt, dtt, params["conv_w"], params["conv_b"], params["dtb_c"], params["dtb_r"],
            params["alog_c"], params["alog_r"], ltri, ltri.T]
    if rev:
        in_specs += [_const_spec((1, D_INNER)), _const_spec((1, D_INNER)),
                     pl.BlockSpec((RB_S, D_INNER), lambda b, i: (b * ns + pos(i), 0))]
        args += [params["dskip"], params["norm_g"], yf]
    return pl.pallas_call(
        functools.partial(_ssd_kernel, rev=rev, n_steps=ns),
        grid=(batch, ns),
        in_specs=in_specs,
        out_specs=pl.BlockSpec((RB_S, D_INNER), lambda b, i: (b * ns + pos(i), 0)),
        out_shape=jax.ShapeDtypeStruct((t, D_INNER), F32),
        scratch_shapes=[pltpu.VMEM((SSD_GROUPS, LANES, LANES), F32)],
        compiler_params=_cparams(("parallel", "arbitrary")),
        name="ssd_bwd" if rev else "ssd_fwd",
    )(*args)


TQ_C = 512
TK_C = 512
N_MAPS = 4
KV_SUB = 8


def _attn_c_kernel(slope_ref, lam_ref, qt_ref, k_ref, vt_ref, bias_ref, g_ref, o_ref, qz_s, m_s, acc_s):
    hp = pl.program_id(1)
    qi = pl.program_id(2)
    ki = pl.program_id(3)

    @pl.when(ki == 0)
    def _():
        m_s[...] = jnp.full_like(m_s, NEG_BIG)
        acc_s[...] = jnp.zeros_like(acc_s)
        qz_s[...] = jnp.zeros_like(qz_s)
        for idx in range(N_MAPS):
            c = idx % 2
            qz_s[idx, c * DIFF_QK_DIM:(c + 1) * DIFF_QK_DIM, :] = qt_ref[idx * DIFF_QK_DIM:(idx + 1) * DIFF_QK_DIM, :]

    units = [(s, hh, c) for s in range(KV_SUB) for hh in range(2) for c in range(2)]

    def scores(u):
        s, hh, c = units[u]
        kt = ki * KV_SUB + s
        var = jnp.where(kt < qi, 0, jnp.where(kt == qi, 1, 2))
        return (_dot(k_ref[s * TK_C:(s + 1) * TK_C, hh * LANES:(hh + 1) * LANES], qz_s[2 * hh + c])
                + bias_ref[var, hh])

    t_next = scores(0)
    for u, (s, hh, c) in enumerate(units):
        t = t_next
        if u + 1 < len(units):
            t_next = scores(u + 1)
        idx = 2 * hh + c
        kt = ki * KV_SUB + s
        c_tile = -jnp.abs(qi * TQ_C - kt * TK_C).astype(F32) * slope_ref[2 * hp + hh]
        m_old = m_s[idx]
        m_new = jnp.maximum(m_old, jnp.max(t, axis=0, keepdims=True) + c_tile)
        alpha = jnp.exp2(m_old - m_new)
        p = jnp.exp2(t - (m_new - c_tile))
        vt = vt_ref[hh * VT_ROWS:(hh + 1) * VT_ROWS, s * TK_C:(s + 1) * TK_C]
        acc_s[idx] = alpha * acc_s[idx] + _dot(vt, p.astype(BF16))
        m_s[idx] = m_new

    @pl.when(ki == pl.num_programs(3) - 1)
    def _():
        _attn_c_finalize(lam_ref, g_ref, o_ref, acc_s)


def _attn_c_finalize(lam_ref, g_ref, o_ref, acc_s):
    lam = lam_ref[0]
    for hh in range(2):
        a0 = acc_s[2 * hh]
        a1 = acc_s[2 * hh + 1]
        o = (a0[0:DIFF_V_DIM] * (1.0 / a0[DIFF_V_DIM:DIFF_V_DIM + 1])
             - lam * (a1[0:DIFF_V_DIM] * (1.0 / a1[DIFF_V_DIM:DIFF_V_DIM + 1])))
        ms = jnp.mean(o * o, axis=0, keepdims=True)
        rows = slice(hh * DIFF_V_DIM, (hh + 1) * DIFF_V_DIM)
        o_ref[rows, :] = (o * lax.rsqrt(ms + EPS) * g_ref[rows, :]).astype(BF16)


AUG_ROWS = 16
AUG_LANES = 64
BOUND_SLACK = 1.02
BOUND_LIMIT = 100.0


def _split3(x):
    hi = x.astype(BF16).astype(F32)
    r = x - hi
    mid = r.astype(BF16).astype(F32)
    return hi, mid, (r - mid).astype(BF16).astype(F32)


def _attn_c_bounded_kernel(slope_ref, lam_ref, kmax_ref, qt_ref, k_ref, vt_ref, dtab_ref, g_ref, o_ref,
                           rhs_s, mref_s, acc_s):
    hp = pl.program_id(1)
    qi = pl.program_id(2)
    ki = pl.program_id(3)
    units = [(s, hh, c) for s in range(KV_SUB) for hh in range(2) for c in range(2)]

    @pl.when((pl.program_id(0) == 0) & (hp == 0) & (qi == 0) & (ki == 0))
    def _():
        rhs_s[...] = jnp.zeros_like(rhs_s)

    @pl.when(ki == 0)
    def _():
        acc_s[...] = jnp.zeros_like(acc_s)
        for u, (s, hh, c) in enumerate(units):
            idx = 2 * hh + c
            qseg = qt_ref[idx * DIFF_QK_DIM:(idx + 1) * DIFF_QK_DIM, :]
            rhs_s[u, c * DIFF_QK_DIM:(c + 1) * DIFF_QK_DIM, :] = qseg
            if s == 0:
                qf = qseg.astype(F32)
                kmax = kmax_ref[0] * BOUND_SLACK
                mref_s[idx] = jnp.sqrt(jnp.sum(qf * qf, axis=0, keepdims=True)) * kmax

    qloc = lax.broadcasted_iota(jnp.int32, (1, TQ_C), 1).astype(F32)
    row = lax.broadcasted_iota(jnp.int32, (AUG_ROWS, TQ_C), 0)
    grp = row // 3
    part = row - 3 * grp

    def key_start(s):
        return pl.multiple_of(((qi + s) % KV_SUB) * TK_C, TK_C)

    def scores(u):
        s, hh, c = units[u]
        kt = ki * KV_SUB + (qi + s) % KV_SUB
        h = 2 * hp + hh
        sgn = jnp.where(kt < qi, 1.0, jnp.where(kt == qi, 0.0, -1.0)).astype(F32)
        slope = slope_ref[4 * h + 3]
        tile_dist = jnp.abs(qi * TQ_C - kt * TK_C).astype(F32)
        shift = mref_s[2 * hh + c] + slope * tile_dist + (sgn * slope) * qloc
        hi, mid, lo = _split3(shift)
        shift_part = jnp.where(part == 0, hi, jnp.where(part == 1, mid, lo))
        slope_part = sgn * jnp.where(part == 0, slope_ref[4 * h], jnp.where(part == 1, slope_ref[4 * h + 1],
                                                                           slope_ref[4 * h + 2]))
        aug = jnp.where(grp == 0, -shift_part, jnp.where(grp <= 2, slope_part, 0.0))
        rhs_s[u, 2 * DIFF_QK_DIM:2 * DIFF_QK_DIM + AUG_ROWS, :] = aug.astype(BF16)
        sc = _dot(k_ref[pl.ds(key_start(s), TK_C), hh * LANES:(hh + 1) * LANES], rhs_s[u])
        if s > 0:
            return sc
        return sc + dtab_ref[(kt == qi).astype(jnp.int32), hh]

    t_next = scores(0)
    for u, (s, hh, c) in enumerate(units):
        t = t_next
        if u + 1 < len(units):
            t_next = scores(u + 1)
        idx = 2 * hh + c
        vt = vt_ref[hh * VT_ROWS:(hh + 1) * VT_ROWS, pl.ds(key_start(s), TK_C)]
        acc_s[idx] = acc_s[idx] + _dot(vt, jnp.exp2(t).astype(BF16))

    @pl.when(ki == pl.num_programs(3) - 1)
    def _():
        _attn_c_finalize(lam_ref, g_ref, o_ref, acc_s)


def _attn_c_bias(slopes):
    rel = (np.arange(TQ_C)[None, :] - np.arange(TK_C)[:, None]).astype(np.float32)
    tables = jnp.asarray(np.stack([-rel, -np.abs(rel), rel]))
    s = slopes.reshape(DIFF_HEADS // 2, 1, 2, 1, 1)
    return s * tables[None, :, None]


def _attn_c_call(body, prefetch, tensors, table_variants, scratch, batch, seq, name):
    nq = seq // TQ_C
    kv_step = TK_C * KV_SUB
    nk = seq // kv_step
    n_pre = len(prefetch)
    grid_spec = pltpu.PrefetchScalarGridSpec(
        num_scalar_prefetch=n_pre,
        grid=(batch, DIFF_HEADS // 2, nq, nk),
        in_specs=[
            pl.BlockSpec((None, LANES, TQ_C), lambda b, h, i, j, *_: (b, h, i)),
            pl.BlockSpec((kv_step, 2 * LANES), lambda b, h, i, j, *_: (b * nk + j, h)),
            pl.BlockSpec((None, 2 * VT_ROWS, kv_step), lambda b, h, i, j, *_: (b, h, j)),
            pl.BlockSpec((None, table_variants, 2, TK_C, TQ_C), lambda b, h, i, j, *_: (h, 0, 0, 0, 0)),
            _const_spec((LANES, TQ_C)),
        ],
        out_specs=pl.BlockSpec((None, LANES, TQ_C), lambda b, h, i, j, *_: (b, h, i)),
        scratch_shapes=scratch + [pltpu.VMEM((N_MAPS, 1, TQ_C), F32), pltpu.VMEM((N_MAPS, VT_ROWS, TQ_C), F32)],
    )
    return pl.pallas_call(
        body,
        grid_spec=grid_spec,
        out_shape=jax.ShapeDtypeStruct((batch, WIDTH_C, seq), BF16),
        compiler_params=_cparams(("arbitrary", "arbitrary", "arbitrary", "arbitrary")),
        name=name,
    )(*prefetch, *tensors)


def _attn_c(qt, k_aug, vt, qk_gains, slopes, lam, gain, batch, seq):
    kv_step = TK_C * KV_SUB
    assert seq % kv_step == 0 and seq % TQ_C == 0
    gain_b = jnp.broadcast_to(gain.reshape(LANES, 1), (LANES, TQ_C))
    tables = _attn_c_bias(slopes)
    seg_bound = lambda g: math.sqrt(DIFF_QK_DIM) * jnp.max(jnp.abs(g))
    qmax = seg_bound(qk_gains[:, 0:256])
    kmax = seg_bound(qk_gains[:, 256:512])

    def online():
        scratch = [pltpu.VMEM((N_MAPS, LANES, TQ_C), BF16)]
        return _attn_c_call(_attn_c_kernel, (slopes, lam), (qt, k_aug, vt, tables, gain_b), 3, scratch,
                            batch, seq, "attn_c_online")

    def bounded():
        slope4 = jnp.stack(_split3(slopes) + (slopes,), axis=1).reshape(-1)
        dtab = jnp.stack([jnp.zeros_like(tables[:, 1]), tables[:, 1]], axis=1)
        scratch = [pltpu.VMEM((N_MAPS * KV_SUB, LANES, TQ_C), BF16)]
        return _attn_c_call(_attn_c_bounded_kernel, (slope4, lam, jnp.reshape(kmax, (1,))),
                            (qt, k_aug, vt, dtab, gain_b), 2, scratch, batch, seq, "attn_c_bounded")

    safe = 2.0 * BOUND_SLACK * BOUND_SLACK * qmax * kmax < BOUND_LIMIT
    return lax.cond(safe, bounded, online)


def _out_proj(oat_ref, ob_ref, oct_ref, x_ref, wo_ref):
    untr = lambda ref: ref[...].astype(F32).T.astype(BF16)
    o = jnp.concatenate([untr(oat_ref), ob_ref[...].astype(BF16), untr(oct_ref)], axis=-1)
    return x_ref[...] + _dot(o, wo_ref[...])


def _rms(x, g_ref):
    ms = jnp.mean(x * x, axis=-1, keepdims=True)
    return x * lax.rsqrt(ms + EPS) * g_ref[...]


def _swiglu_acc(h, wg_ref, wu_ref, wd_ref, acc, d_ff, chunks):
    step = d_ff // chunks
    for c0 in range(0, d_ff, step):
        gate = _dot(h, wg_ref[:, c0:c0 + step])
        up = _dot(h, wu_ref[:, c0:c0 + step])
        act = (gate * _sigmoid(gate) * up).astype(BF16)
        acc = acc + _dot(act, wd_ref[c0:c0 + step, :])
    return acc


def _out_ffn_kernel(oa_ref, ob_ref, oc_ref, x_ref, wo_ref, g_ref, wg_ref, wu_ref, wd_ref, out_ref):
    x1 = _out_proj(oa_ref, ob_ref, oc_ref, x_ref, wo_ref)
    h = _rms(x1, g_ref).astype(BF16)
    out_ref[...] = _swiglu_acc(h, wg_ref, wu_ref, wd_ref, x1, D_FF, FFN_CHUNKS)


def _mixer_out_specs(seq):
    nps = seq // TM
    row = lambda w: pl.BlockSpec((TM, w), lambda i: (i, 0))
    tr = lambda r: pl.BlockSpec((None, r, TM), lambda i: (i // nps, 0, i % nps))
    return [tr(WIDTH_A), row(D_INNER), tr(WIDTH_C), row(D_MODEL), _resident_spec((D_MODEL, D_MODEL)),
            _const_spec((1, D_MODEL))]


def _out_ffn(oa, ob, oc, x, w_out, g, wg, wu, wd):
    t = x.shape[0]
    return pl.pallas_call(
        _out_ffn_kernel,
        grid=(t // TM,),
        in_specs=_mixer_out_specs(oa.shape[2]) + [_resident_spec((D_MODEL, D_FF)), _resident_spec((D_MODEL, D_FF)),
                                       _resident_spec((D_FF, D_MODEL))],
        out_specs=pl.BlockSpec((TM, D_MODEL), lambda i: (i, 0)),
        out_shape=jax.ShapeDtypeStruct((t, D_MODEL), F32),
        compiler_params=_cparams(("parallel",)),
        name="out_ffn",
    )(oa, ob, oc, x, w_out, g, wg, wu, wd)


def _out_router_kernel(oa_ref, ob_ref, oc_ref, x_ref, wo_ref, g_ref, wr_hi_ref, wr_lo_ref,
                       x1_ref, h_ref, sel_ref, gate_ref):
    x1 = _out_proj(oa_ref, ob_ref, oc_ref, x_ref, wo_ref)
    x1_ref[...] = x1
    h = _rms(x1, g_ref)
    h_hi = h.astype(BF16)
    h_ref[...] = h_hi
    h_lo = (h - h_hi.astype(F32)).astype(BF16)
    logits = _dot(h_hi, wr_hi_ref[...]) + (_dot(h_lo, wr_hi_ref[...]) + _dot(h_hi, wr_lo_ref[...]))
    lt = logits.T[0:N_EXPERTS, :]
    row_i = lax.broadcasted_iota(jnp.int32, (N_EXPERTS, TM), 0)
    row = row_i.astype(F32)
    v1 = jnp.max(lt, axis=0, keepdims=True)
    i1 = jnp.min(jnp.where(lt == v1, row, float(N_EXPERTS)), axis=0, keepdims=True)
    rest = jnp.where(row == i1, -jnp.inf, lt)
    v2 = jnp.max(rest, axis=0, keepdims=True)
    i2 = jnp.min(jnp.where(rest == v2, row, float(N_EXPERTS)), axis=0, keepdims=True)
    e = jnp.exp(v2 - v1)
    g1 = 1.0 / (1.0 + e)
    sel_ref[...] = jnp.where(row_i == 0, i1, jnp.where(row_i == 1, i2, 0.0)).astype(jnp.int32)
    gate_ref[...] = jnp.where(row_i == 0, g1, jnp.where(row_i == 1, e * g1, 0.0))


def _out_router(oa, ob, oc, x, w_out, g, wr_hi, wr_lo):
    t = x.shape[0]
    row = lambda w: pl.BlockSpec((TM, w), lambda i: (i, 0))
    col = pl.BlockSpec((N_EXPERTS, TM), lambda i: (0, i))
    return pl.pallas_call(
        _out_router_kernel,
        grid=(t // TM,),
        in_specs=_mixer_out_specs(oa.shape[2]) + [_const_spec((D_MODEL, LANES)), _const_spec((D_MODEL, LANES))],
        out_specs=[row(D_MODEL), row(D_MODEL), col, col],
        out_shape=[jax.ShapeDtypeStruct((t, D_MODEL), F32), jax.ShapeDtypeStruct((t, D_MODEL), BF16),
                   jax.ShapeDtypeStruct((N_EXPERTS, t), jnp.int32), jax.ShapeDtypeStruct((N_EXPERTS, t), F32)],
        compiler_params=_cparams(("parallel",)),
        name="out_router",
    )(oa, ob, oc, x, w_out, g, wr_hi, wr_lo)


def _moe_ffn_kernel(blk_exp_ref, n_used_ref, x_ref, wg_ref, wu_ref, wd_ref, out_ref):
    i = pl.program_id(0)

    @pl.when(i < n_used_ref[0])
    def _():
        acc = jnp.zeros((MOE_BLK, D_MODEL), F32)
        out_ref[...] = _swiglu_acc(x_ref[...], wg_ref.at[0], wu_ref.at[0], wd_ref.at[0], acc,
                                   D_FF_EXPERT, MOE_CHUNKS).astype(BF16)

    @pl.when(i >= n_used_ref[0])
    def _():
        out_ref[...] = jnp.zeros_like(out_ref)


def _moe_ffn(xg, blk_exp, n_used, wg, wu, wd):
    rows = xg.shape[0]
    wspec = lambda shape: pl.BlockSpec((1,) + shape, lambda i, be, nu: (be[i], 0, 0),
                                       pipeline_mode=pl.Buffered(1))
    grid_spec = pltpu.PrefetchScalarGridSpec(
        num_scalar_prefetch=2,
        grid=(rows // MOE_BLK,),
        in_specs=[pl.BlockSpec((MOE_BLK, D_MODEL), lambda i, be, nu: (i, 0)),
                  wspec((D_MODEL, D_FF_EXPERT)), wspec((D_MODEL, D_FF_EXPERT)),
                  wspec((D_FF_EXPERT, D_MODEL))],
        out_specs=pl.BlockSpec((MOE_BLK, D_MODEL), lambda i, be, nu: (i, 0)),
    )
    return pl.pallas_call(
        _moe_ffn_kernel,
        grid_spec=grid_spec,
        out_shape=jax.ShapeDtypeStruct((rows, D_MODEL), BF16),
        compiler_params=_cparams(("arbitrary",)),
        name="moe_ffn",
    )(blk_exp, n_used, xg, wg, wu, wd)


def _moe(x1, h, sel, gates, wg, wu, wd):
    t = x1.shape[0]
    i1, i2 = sel[0], sel[1]
    g1, g2 = gates[0][:, None], gates[1][:, None]
    experts = jnp.arange(N_EXPERTS, dtype=jnp.int32)
    onehot = ((i1[:, None] == experts) | (i2[:, None] == experts)).astype(jnp.int32)
    rank = jnp.cumsum(onehot, axis=0) - onehot
    counts = jnp.sum(onehot, axis=0)
    padded = ((counts + MOE_BLK - 1) // MOE_BLK) * MOE_BLK
    pad_end = jnp.cumsum(padded)
    pad_start = pad_end - padded
    dest = pad_start[None, :] + rank
    d1 = jnp.take_along_axis(dest, i1[:, None], axis=1)[:, 0]
    d2 = jnp.take_along_axis(dest, i2[:, None], axis=1)[:, 0]
    n_blocks = (2 * t) // MOE_BLK + N_EXPERTS
    rows = n_blocks * MOE_BLK
    blk_start = jnp.arange(n_blocks, dtype=jnp.int32) * MOE_BLK
    blk_exp = jnp.minimum(jnp.sum((pad_end[None, :] <= blk_start[:, None]).astype(jnp.int32), axis=1),
                          N_EXPERTS - 1)
    order = jnp.argsort(jnp.concatenate([d1, d2])).astype(jnp.int32)
    row_exp = jnp.repeat(blk_exp, MOE_BLK)
    in_exp = jnp.arange(rows, dtype=jnp.int32) - pad_start[row_exp]
    starts = jnp.cumsum(counts) - counts
    packed = jnp.minimum(starts[row_exp] + in_exp, 2 * t - 1)
    row_tok = jnp.where(in_exp < counts[row_exp], order[packed] % t, 0)
    n_used = (pad_end[-1:] // MOE_BLK).astype(jnp.int32)
    yb = _moe_ffn(h[row_tok], blk_exp, n_used, wg, wu, wd)
    return x1 + g1 * yb[d1].astype(F32) + g2 * yb[d2].astype(F32)


def _split_cols(w):
    outs, start = [], 0
    for n in SPLIT_SIZES:
        outs.append(w[:, start:start + n])
        start += n
    return outs


def _layer_params(l, p):
    qa, ka, va, zb, xbc, dtb, qc, kc, vc = _split_cols(p["w_in"][l])
    w_cat = jnp.concatenate(
        [qa, ka, va, xbc, zb, qc, kc, vc, dtb, jnp.zeros((D_MODEL, DT_W - 2 * SSD_HEADS), F32)],
        axis=1).astype(BF16)
    ga = jnp.concatenate([jnp.tile(p["qnorm_a"][l] * (HEAD_DIM_A ** -0.5 * LOG2E), N_HEADS_A),
                          jnp.tile(p["knorm_a"][l], KV_HEADS_A)])[None, :]
    gc = jnp.concatenate([jnp.tile(p["qnorm_c"][l] * (DIFF_QK_DIM ** -0.5 * LOG2E), 2 * DIFF_HEADS),
                          jnp.tile(p["knorm_c"][l], 2 * DIFF_HEADS)])[None, :]
    lam_init = 0.8 - 0.6 * math.exp(-0.3 * l)
    lc = p["lam_c"][l].astype(F32)
    lam = jnp.exp(jnp.sum(lc[0] * lc[1])) - jnp.exp(jnp.sum(lc[2] * lc[3])) + lam_init
    out = {
        "norm_mix_g": p["norm_mix_g"][l][None, :], "w_cat": w_cat, "ga": ga, "gc": gc,
        "sink": p["sink_a"][l],
        "ssd": {
            "conv_w": p["conv_w"][l], "conv_b": p["conv_b"][l][None, :],
            "dtb_c": p["dt_bias"][l].reshape(1, -1), "dtb_r": p["dt_bias"][l].reshape(-1, 1),
            "alog_c": p["a_log"][l].reshape(1, -1), "alog_r": p["a_log"][l].reshape(-1, 1),
            "dskip": jnp.repeat(p["d_skip"][l], SSD_HEAD_DIM)[None, :],
            "norm_g": p["ssd_norm_g"][l][None, :],
        },
        "slopes_c": jnp.exp2(-8.0 * jnp.arange(1, DIFF_HEADS + 1, dtype=F32) / DIFF_HEADS) * LOG2E,
        "lam": jnp.reshape(lam, (1,)).astype(F32),
        "gain_c": (jnp.tile(p["diff_norm_g"][l], 2) * (1.0 - lam_init))[None, :],
        "w_out": p["w_out"][l].astype(BF16),
        "norm_ffn_g": p["norm_ffn_g"][l][None, :],
    }
    i = l // 2
    if l % 2 == 0:
        out["ffn"] = (p["ffn_w_gate"][i].astype(BF16), p["ffn_w_up"][i].astype(BF16),
                      p["ffn_w_down"][i].astype(BF16))
    else:
        wr = jnp.pad(p["router_w"][i], ((0, 0), (0, LANES - N_EXPERTS)))
        wr_hi = wr.astype(BF16)
        out["router"] = (wr_hi, (wr - wr_hi.astype(F32)).astype(BF16))
        out["moe"] = (p["moe_w_gate"][i].astype(BF16), p["moe_w_up"][i].astype(BF16),
                      p["moe_w_down"][i].astype(BF16))
    return out


def _trunk(x3, layers):
    batch, seq, _ = x3.shape
    x = x3.reshape(batch * seq, D_MODEL)
    for l, lp in enumerate(layers):
        qat, ka, vat, pb, qct, kc, vct, dt, dtt = _norm_proj(x, lp["norm_mix_g"], lp["w_cat"], lp["ga"], lp["gc"],
                                                             batch, seq)
        oa = _attn_a(qat, ka, vat, lp["sink"], batch, seq)
        yf = _ssd(pb, dt, dtt, lp["ssd"], None, batch, seq)
        ob = _ssd(pb, dt, dtt, lp["ssd"], yf, batch, seq)
        oc = _attn_c(qct, kc, vct, lp["gc"], lp["slopes_c"], lp["lam"], lp["gain_c"], batch, seq)
        if l % 2 == 0:
            x = _out_ffn(oa, ob, oc, x, lp["w_out"], lp["norm_ffn_g"], *lp["ffn"])
        else:
            x1, h, sel, gates = _out_router(oa, ob, oc, x, lp["w_out"], lp["norm_ffn_g"], *lp["router"])
            x = _moe(x1, h, sel, gates, *lp["moe"])
    return x.reshape(batch, seq, D_MODEL)


def kernel(x_prompt, x_sample, norm_mix_g, w_in, qnorm_a, knorm_a, sink_a, conv_w, conv_b, dt_bias, a_log,
           d_skip, ssd_norm_g, qnorm_c, knorm_c, lam_c, diff_norm_g, w_out, norm_ffn_g, ffn_w_gate, ffn_w_up,
           ffn_w_down, router_w, moe_w_gate, moe_w_up, moe_w_down):
    p = dict(norm_mix_g=norm_mix_g, w_in=w_in, qnorm_a=qnorm_a, knorm_a=knorm_a, sink_a=sink_a,
             conv_w=conv_w, conv_b=conv_b, dt_bias=dt_bias, a_log=a_log, d_skip=d_skip,
             ssd_norm_g=ssd_norm_g, qnorm_c=qnorm_c, knorm_c=knorm_c, lam_c=lam_c, diff_norm_g=diff_norm_g,
             w_out=w_out, norm_ffn_g=norm_ffn_g, ffn_w_gate=ffn_w_gate, ffn_w_up=ffn_w_up,
             ffn_w_down=ffn_w_down, router_w=router_w, moe_w_gate=moe_w_gate, moe_w_up=moe_w_up,
             moe_w_down=moe_w_down)
    layers = [_layer_params(l, p) for l in range(DEPTH)]
    return _trunk(x_prompt, layers), _trunk(x_sample, layers)
```
